```python
import jax, jax.numpy as jnp
from jax import lax
import numpy as np

D_MODEL = 2048
BATCH = 2
SEQ = 4096
DEPTH = 4

N_MIXERS = 3
D_FF = 4 * D_MODEL
NORM_EPS = 1e-6
ROPE_THETA = 500000.0
ROPE_FRACTION = 4
NEG_INF = -1e30

NSA_HEAD_DIM = 128
NSA_HEADS = D_MODEL // NSA_HEAD_DIM
NSA_KV_GROUPS = 4
NSA_CMP_BLOCK = 32
NSA_CMP_STRIDE = 16
NSA_SEL_BLOCK = 64
NSA_TOP_N = 16
NSA_WINDOW = 512
NSA_WIN_BLOCK = 128
NSA_QUERY_CHUNK = 64
NSA_FORCE_BONUS = 1e4
NSA_IN = (NSA_HEADS + 6 * NSA_KV_GROUPS) * NSA_HEAD_DIM + 3 * NSA_HEADS

GLA_HEADS = 4
GLA_KEY_DIM = D_MODEL // 2 // GLA_HEADS
GLA_VAL_DIM = D_MODEL // GLA_HEADS
GLA_GATE_RANK = 16
GLA_TAU = 16.0
GLA_CHUNK = 64
GLA_IN = GLA_HEADS * (2 * GLA_KEY_DIM + 2 * GLA_VAL_DIM) + GLA_GATE_RANK

SWA_HEAD_DIM = 64
SWA_HEADS = D_MODEL // SWA_HEAD_DIM
SWA_KV_HEADS = 4
SWA_WINDOW = 128
SWA_BLOCK = 128
SWA_IN = (SWA_HEADS + 2 * SWA_KV_HEADS) * SWA_HEAD_DIM

N_A = (DEPTH + N_MIXERS - 1) // N_MIXERS
N_B = (DEPTH + N_MIXERS - 2) // N_MIXERS
N_C = DEPTH // N_MIXERS

kernel_name = "hybrid_nsa_gla_swasink_trunk"


def rms_norm(x, gain):
    x32 = x.astype(jnp.float32)
    y = x32 * lax.rsqrt(jnp.mean(x32 * x32, axis=-1, keepdims=True) + NORM_EPS)
    return (y * gain.astype(jnp.float32)).astype(x.dtype)


def partial_rope(x, pos):
    d = x.shape[-1]
    rot = d // ROPE_FRACTION
    half = rot // 2
    inv_freq = jnp.power(jnp.float32(ROPE_THETA), -jnp.arange(half, dtype=jnp.float32) / half)
    ang = pos.astype(jnp.float32)[:, None] * inv_freq[None, :]
    ang = ang.reshape(ang.shape[:1] + (1,) * (x.ndim - 3) + (half,))
    cos, sin = jnp.cos(ang), jnp.sin(ang)
    x32 = x.astype(jnp.float32)
    x1, x2 = x32[..., :half], x32[..., half:rot]
    out = jnp.concatenate([x1 * cos - x2 * sin, x2 * cos + x1 * sin, x32[..., rot:]], axis=-1)
    return out.astype(x.dtype)


def masked_softmax(s, mask):
    return jax.nn.softmax(jnp.where(mask, s.astype(jnp.float32), NEG_INF), axis=-1)


def split_cols(t, sizes):
    offs = [int(o) for o in np.cumsum(sizes)[:-1]]
    return jnp.split(t, offs, axis=-1)


def banded_blocks(x, blk, n_prev):
    b, s = x.shape[:2]
    nb = s // blk
    xb = x.reshape((b, nb, blk) + x.shape[2:])
    parts = []
    for p in range(n_prev, 0, -1):
        pw = [(0, 0)] * xb.ndim
        pw[1] = (p, 0)
        parts.append(jnp.pad(xb, pw)[:, :nb])
    parts.append(xb)
    return jnp.concatenate(parts, axis=2)


def band_mask(s, blk, n_prev, window):
    nb = s // blk
    qpos = jnp.arange(s).reshape(nb, blk)
    kpos = (jnp.arange(nb)[:, None] - n_prev) * blk + jnp.arange((n_prev + 1) * blk)[None, :]
    qp, kp = qpos[:, :, None], kpos[:, None, :]
    return (kp <= qp) & (kp > qp - window) & (kp >= 0)


def selection_overlap(n_cmp, n_sel):
    c0 = np.arange(n_cmp) * NSA_CMP_STRIDE
    s0 = np.arange(n_sel) * NSA_SEL_BLOCK
    ov = np.minimum(c0[:, None] + NSA_CMP_BLOCK, s0[None, :] + NSA_SEL_BLOCK) - np.maximum(c0[:, None], s0[None, :])
    return jnp.asarray(np.clip(ov, 0, None) / NSA_CMP_BLOCK, dtype=jnp.float32)


def compress_blocks(x, idx, pe, w1, w2):
    blk = x[:, idx] + pe[:, None, :]
    b, n, l, g, d = blk.shape
    flat = blk.transpose(0, 1, 3, 2, 4).reshape(b, n, g, l * d)
    return jax.nn.gelu(flat @ w1) @ w2


def nsa_mixer(h, w_in, w_out, q_norm, k_norm, cmp_pe, cmp_w1, cmp_w2):
    b, s, _ = h.shape
    g, d = NSA_KV_GROUPS, NSA_HEAD_DIM
    hpg = NSA_HEADS // g
    scale = d ** -0.5
    kv = g * d
    q, kc, vc, ks, vs, kw, vw, gate = split_cols(h @ w_in, [NSA_HEADS * d] + [kv] * 6 + [3 * NSA_HEADS])
    pos = jnp.arange(s)
    q = partial_rope(rms_norm(q.reshape(b, s, g, hpg, d), q_norm), pos)
    kc, vc, ks, vs, kw, vw = [t.reshape(b, s, g, d) for t in (kc, vc, ks, vs, kw, vw)]

    n_cmp = (s - NSA_CMP_BLOCK) // NSA_CMP_STRIDE + 1
    cmp_idx = np.arange(n_cmp)[:, None] * NSA_CMP_STRIDE + np.arange(NSA_CMP_BLOCK)[None, :]
    cmp_end = jnp.asarray(cmp_idx[:, -1])
    k_cmp = compress_blocks(kc, cmp_idx, cmp_pe[0], cmp_w1[0], cmp_w2[0])
    k_cmp = partial_rope(rms_norm(k_cmp, k_norm[0]), cmp_end)
    v_cmp = compress_blocks(vc, cmp_idx, cmp_pe[1], cmp_w1[1], cmp_w2[1])
    s_cmp = jnp.einsum('bsghd,bcgd->bghsc', q, k_cmp).astype(jnp.float32) * scale
    has_cmp = (pos >= NSA_CMP_BLOCK - 1).astype(jnp.float32)
    p_cmp = masked_softmax(s_cmp, cmp_end[None, :] <= pos[:, None]) * has_cmp[:, None]
    o_cmp = jnp.einsum('bghsc,bcgd->bsghd', p_cmp.astype(v_cmp.dtype), v_cmp)

    n_sel = s // NSA_SEL_BLOCK
    n_top = min(NSA_TOP_N, n_sel)
    imp = jnp.einsum('bghsc,cn->bgsn', p_cmp, selection_overlap(n_cmp, n_sel))
    blk_q = (pos // NSA_SEL_BLOCK)[:, None]
    j = jnp.arange(n_sel)[None, :]
    forced = (j == 0) | (j == blk_q) | (j == blk_q - 1)
    score = jnp.where(j <= blk_q, imp + jnp.where(forced, NSA_FORCE_BONUS, 0.0), NEG_INF)
    _, sel_idx = lax.top_k(score, n_top)

    ks = partial_rope(rms_norm(ks, k_norm[1]), pos)
    kb = ks.reshape(b, n_sel, NSA_SEL_BLOCK, g, d).transpose(0, 3, 1, 2, 4)
    vb = vs.reshape(b, n_sel, NSA_SEL_BLOCK, g, d).transpose(0, 3, 1, 2, 4)
    qc_len = NSA_QUERY_CHUNK
    nq = s // qc_len
    q_chunks = q.transpose(0, 2, 3, 1, 4).reshape(b, g, hpg, nq, qc_len, d).transpose(3, 0, 1, 2, 4, 5)
    idx_chunks = sel_idx.reshape(b, g, nq, qc_len, n_top).transpose(2, 0, 1, 3, 4)
    pos_chunks = pos.reshape(nq, qc_len)
    gather = jax.vmap(jax.vmap(lambda blocks, ix: blocks[ix]))
    n_keys = n_top * NSA_SEL_BLOCK

    def sel_chunk(args):
        qc, ic, pc = args
        kg = gather(kb, ic).reshape(b, g, qc_len, n_keys, d)
        vg = gather(vb, ic).reshape(b, g, qc_len, n_keys, d)
        kpos = (ic[..., None] * NSA_SEL_BLOCK + jnp.arange(NSA_SEL_BLOCK)).reshape(b, g, qc_len, n_keys)
        mask = (kpos <= pc[None, None, :, None])[:, :, None]
        sc = jnp.einsum('bghqd,bgqkd->bghqk', qc, kg).astype(jnp.float32) * scale
        p = masked_softmax(sc, mask)
        return jnp.einsum('bghqk,bgqkd->bghqd', p.astype(vg.dtype), vg)

    o_sel = lax.map(sel_chunk, (q_chunks, idx_chunks, pos_chunks))
    o_sel = o_sel.transpose(1, 0, 4, 2, 3, 5).reshape(b, s, g, hpg, d)

    kw = partial_rope(rms_norm(kw, k_norm[2]), pos)
    n_prev = NSA_WINDOW // NSA_WIN_BLOCK
    nb = s // NSA_WIN_BLOCK
    kwb = banded_blocks(kw, NSA_WIN_BLOCK, n_prev)
    vwb = banded_blocks(vw, NSA_WIN_BLOCK, n_prev)
    qwb = q.reshape(b, nb, NSA_WIN_BLOCK, g, hpg, d)
    s_win = jnp.einsum('bnqghd,bnkgd->bghnqk', qwb, kwb).astype(jnp.float32) * scale
    p_win = masked_softmax(s_win, band_mask(s, NSA_WIN_BLOCK, n_prev, NSA_WINDOW))
    o_win = jnp.einsum('bghnqk,bnkgd->bnqghd', p_win.astype(vwb.dtype), vwb).reshape(b, s, g, hpg, d)

    gts = jax.nn.sigmoid(gate.astype(jnp.float32)).reshape(b, s, g, hpg, 3)
    o = gts[..., 0:1] * o_cmp + gts[..., 1:2] * o_sel + gts[..., 2:3] * o_win
    return o.astype(h.dtype).reshape(b, s, NSA_HEADS * d) @ w_out


def gla_mixer(h, w_in, w_gate_up, b_gate, o_norm, w_out):
    b, s, _ = h.shape
    nh, dk, dv, c = GLA_HEADS, GLA_KEY_DIM, GLA_VAL_DIM, GLA_CHUNK
    q, k, v, g_lr, r = split_cols(h @ w_in, [nh * dk, nh * dk, nh * dv, GLA_GATE_RANK, nh * dv])
    log_a = jax.nn.log_sigmoid((g_lr @ w_gate_up + b_gate).astype(jnp.float32)) / GLA_TAU
    nc = s // c

    def to_chunks(t, dd):
        return t.astype(jnp.float32).reshape(b, nc, c, nh, dd).transpose(1, 0, 3, 2, 4)

    qs = to_chunks(q, dk) * (dk ** -0.5)
    ks_, vs_, gs = to_chunks(k, dk), to_chunks(v, dv), to_chunks(log_a, dk)
    causal = jnp.tril(jnp.ones((c, c), dtype=bool))[:, :, None]

    def step(state, inp):
        qc, kc, vc, gc = inp
        cum = jnp.cumsum(gc, axis=2)
        o_inter = jnp.einsum('bhcd,bhde->bhce', qc * jnp.exp(cum), state)
        decay = jnp.exp(jnp.where(causal, cum[:, :, :, None, :] - cum[:, :, None, :, :], NEG_INF))
        att = jnp.einsum('bhid,bhjd,bhijd->bhij', qc, kc, decay)
        o_intra = jnp.einsum('bhij,bhje->bhie', att, vc)
        last = cum[:, :, -1:, :]
        state = jnp.exp(last[:, :, 0, :])[..., None] * state + jnp.einsum('bhcd,bhce->bhde', kc * jnp.exp(last - cum), vc)
        return state, o_inter + o_intra

    state0 = jnp.zeros((b, nh, dk, dv), jnp.float32)
    _, o = lax.scan(step, state0, (qs, ks_, vs_, gs))
    o = o.transpose(1, 0, 3, 2, 4).reshape(b, s, nh, dv).astype(h.dtype)
    o = rms_norm(o, o_norm) * jax.nn.silu(r.reshape(b, s, nh, dv))
    return o.reshape(b, s, nh * dv) @ w_out


def swa_sink_mixer(h, w_in, w_out, q_norm, k_norm, sinks):
    b, s, _ = h.shape
    g, d = SWA_KV_HEADS, SWA_HEAD_DIM
    hpg = SWA_HEADS // g
    q, k, v = split_cols(h @ w_in, [SWA_HEADS * d, g * d, g * d])
    pos = jnp.arange(s)
    q = partial_rope(rms_norm(q.reshape(b, s, g, hpg, d), q_norm), pos)
    k = partial_rope(rms_norm(k.reshape(b, s, g, d), k_norm), pos)
    v = v.reshape(b, s, g, d)
    n_prev = 1
    nb = s // SWA_BLOCK
    kb = banded_blocks(k, SWA_BLOCK, n_prev)
    vb = banded_blocks(v, SWA_BLOCK, n_prev)
    qb = q.reshape(b, nb, SWA_BLOCK, g, hpg, d)
    sc = jnp.einsum('bnqghd,bnkgd->bghnqk', qb, kb).astype(jnp.float32) * (d ** -0.5)
    sc = jnp.where(band_mask(s, SWA_BLOCK, n_prev, SWA_WINDOW), sc, NEG_INF)
    sink = jnp.broadcast_to(sinks.astype(jnp.float32).reshape(1, g, hpg, 1, 1, 1), sc.shape[:-1] + (1,))
    p = jax.nn.softmax(jnp.concatenate([sc, sink], axis=-1), axis=-1)[..., :-1]
    o = jnp.einsum('bghnqk,bnkgd->bnqghd', p.astype(vb.dtype), vb)
    return o.reshape(b, s, SWA_HEADS * d).astype(h.dtype) @ w_out


def squared_relu_mlp(h, w_up, w_down):
    return jnp.square(jax.nn.relu(h @ w_up)) @ w_down


def setup_inputs(seed: int = 0) -> dict:
    key = jax.random.key(seed)
    keys = iter(jax.random.split(key, 32))

    def normal(shape, scale):
        return scale * jax.random.normal(next(keys), shape, jnp.float32)

    def gain(shape):
        return jnp.ones(shape, jnp.float32) + normal(shape, 0.05)

    nd = NSA_HEAD_DIM
    return {
        "x": normal((BATCH, SEQ, D_MODEL), 1.0),
        "norm_mix": gain((DEPTH, D_MODEL)),
        "norm_mlp": gain((DEPTH, D_MODEL)),
        "mlp_w_up": normal((DEPTH, D_MODEL, D_FF), D_MODEL ** -0.5),
        "mlp_w_down": normal((DEPTH, D_FF, D_MODEL), D_FF ** -0.5),
        "nsa_w_in": normal((N_A, D_MODEL, NSA_IN), D_MODEL ** -0.5),
        "nsa_w_out": normal((N_A, NSA_HEADS * nd, D_MODEL), (NSA_HEADS * nd) ** -0.5),
        "nsa_q_norm": gain((N_A, nd)),
        "nsa_k_norm": gain((N_A, 3, nd)),
        "nsa_cmp_pe": normal((N_A, 2, NSA_CMP_BLOCK, nd), 0.1),
        "nsa_cmp_w1": normal((N_A, 2, NSA_CMP_BLOCK * nd, nd), (NSA_CMP_BLOCK * nd) ** -0.5),
        "nsa_cmp_w2": normal((N_A, 2, nd, nd), nd ** -0.5),
        "gla_w_in": normal((N_B, D_MODEL, GLA_IN), D_MODEL ** -0.5),
        "gla_w_gate_up": normal((N_B, GLA_GATE_RANK, GLA_HEADS * GLA_KEY_DIM), GLA_GATE_RANK ** -0.5),
        "gla_b_gate": normal((N_B, GLA_HEADS * GLA_KEY_DIM), 0.01),
        "gla_o_norm": gain((N_B, GLA_VAL_DIM)),
        "gla_w_out": normal((N_B, GLA_HEADS * GLA_VAL_DIM, D_MODEL), (GLA_HEADS * GLA_VAL_DIM) ** -0.5),
        "swa_w_in": normal((N_C, D_MODEL, SWA_IN), D_MODEL ** -0.5),
        "swa_w_out": normal((N_C, SWA_HEADS * SWA_HEAD_DIM, D_MODEL), (SWA_HEADS * SWA_HEAD_DIM) ** -0.5),
        "swa_q_norm": gain((N_C, SWA_HEAD_DIM)),
        "swa_k_norm": gain((N_C, SWA_HEAD_DIM)),
        "swa_sinks": normal((N_C, SWA_HEADS), 0.5),
    }


def reference(x, norm_mix, norm_mlp, mlp_w_up, mlp_w_down,
              nsa_w_in, nsa_w_out, nsa_q_norm, nsa_k_norm, nsa_cmp_pe, nsa_cmp_w1, nsa_cmp_w2,
              gla_w_in, gla_w_gate_up, gla_b_gate, gla_o_norm, gla_w_out,
              swa_w_in, swa_w_out, swa_q_norm, swa_k_norm, swa_sinks):
    ia = ib = ic = 0
    for i in range(DEPTH):
        h = rms_norm(x, norm_mix[i])
        kind = i % N_MIXERS
        if kind == 0:
            y = nsa_mixer(h, nsa_w_in[ia], nsa_w_out[ia], nsa_q_norm[ia], nsa_k_norm[ia],
                          nsa_cmp_pe[ia], nsa_cmp_w1[ia], nsa_cmp_w2[ia])
            ia += 1
        elif kind == 1:
            y = gla_mixer(h, gla_w_in[ib], gla_w_gate_up[ib], gla_b_gate[ib], gla_o_norm[ib], gla_w_out[ib])
            ib += 1
        else:
            y = swa_sink_mixer(h, swa_w_in[ic], swa_w_out[ic], swa_q_norm[ic], swa_k_norm[ic], swa_sinks[ic])
            ic += 1
        x = x + y.astype(x.dtype)
        h = rms_norm(x, norm_mlp[i])
        x = x + squared_relu_mlp(h, mlp_w_up[i], mlp_w_down[i]).astype(x.dtype)
    return x
```

```python
import functools

import numpy as np
import jax
import jax.numpy as jnp
from jax import lax
from jax.experimental import pallas as pl
from jax.experimental.pallas import tpu as pltpu

F32 = jnp.float32
BF16 = jnp.bfloat16

NORM_EPS = 1e-6
ROPE_THETA = 500000.0
ROPE_FRACTION = 4
NEG_INF = -1e30
N_MIXERS = 3

NSA_HEAD_DIM = 128
NSA_HEADS = 16
NSA_KV_GROUPS = 4
NSA_HPG = NSA_HEADS // NSA_KV_GROUPS
NSA_CMP_BLOCK = 32
NSA_CMP_STRIDE = 16
NSA_SEL_BLOCK = 64
NSA_TOP_N = 16
NSA_WINDOW = 512
NSA_FORCE_BONUS = 1e4

GLA_HEADS = 4
GLA_KEY_DIM = 256
GLA_VAL_DIM = 512
GLA_GATE_RANK = 16
GLA_TAU = 16.0
GLA_CHUNK = 64
GLA_SUB = 16

SWA_HEAD_DIM = 64
SWA_HEADS = 32
SWA_KV_HEADS = 4
SWA_HPG = SWA_HEADS // SWA_KV_HEADS
SWA_WINDOW = 128

LANES = 128
VMEM_LIMIT = 56 * 1024 * 1024

NT_DIMS = (((1,), (1,)), ((), ()))
TN_DIMS = (((0,), (0,)), ((), ()))


def _params(sem):
    return pltpu.CompilerParams(dimension_semantics=sem, vmem_limit_bytes=VMEM_LIMIT)


def _dot(a, b, dims=None, precision=None):
    if dims is None:
        return jnp.dot(a, b, preferred_element_type=F32, precision=precision)
    return lax.dot_general(a, b, dims, preferred_element_type=F32, precision=precision)


def _rms_rows_to(h_scr, x_ref, g_ref, rows):
    n = x_ref.shape[0] // rows

    def body(i, c):
        r0 = pl.multiple_of(i * rows, rows)
        x = x_ref[pl.ds(r0, rows), :]
        ms = jnp.mean(x * x, axis=-1, keepdims=True)
        h_scr[pl.ds(r0, rows), :] = (x * lax.rsqrt(ms + NORM_EPS) * g_ref[...]).astype(BF16)
        return c

    lax.fori_loop(0, n, body, 0)


def _norm_matmul_kernel(x_ref, g_ref, w_ref, *rest, has_extra):
    if has_extra:
        wx_ref, o_ref, ox_ref, h_scr = rest
    else:
        o_ref, h_scr = rest

    @pl.when(pl.program_id(1) == 0)
    def _():
        _rms_rows_to(h_scr, x_ref, g_ref, 128)
        if has_extra:
            ox_ref[...] = _dot(h_scr[...], wx_ref[...].astype(BF16))

    o_ref[...] = _dot(h_scr[...], w_ref[...].astype(BF16))


def norm_matmul(x, gain, w, layer, n_cols, w_extra=None, tm=1024, tn=512):
    m, d = x.shape
    has_extra = w_extra is not None
    in_specs = [pl.BlockSpec((tm, d), lambda i, j: (i, 0)),
                pl.BlockSpec((1, d), lambda i, j: (0, 0)),
                pl.BlockSpec((None, d, tn), lambda i, j: (layer, 0, j))]
    out_shape = [jax.ShapeDtypeStruct((m, n_cols), F32)]
    out_specs = [pl.BlockSpec((tm, tn), lambda i, j: (i, j))]
    args = [x, gain.reshape(1, d), w]
    if has_extra:
        in_specs.append(pl.BlockSpec((d, LANES), lambda i, j: (0, 0)))
        out_shape.append(jax.ShapeDtypeStruct((m, LANES), F32))
        out_specs.append(pl.BlockSpec((tm, LANES), lambda i, j: (i, 0)))
        args.append(w_extra)
    return pl.pallas_call(
        functools.partial(_norm_matmul_kernel, has_extra=has_extra),
        grid=(m // tm, n_cols // tn),
        in_specs=in_specs, out_specs=out_specs, out_shape=out_shape,
        scratch_shapes=[pltpu.VMEM((tm, d), BF16)],
        compiler_params=_params(("parallel", "arbitrary")),
        name="norm_matmul",
    )(*args)


def _matmul_residual_kernel(a_ref, w_ref, r_ref, o_ref):
    o_ref[...] = r_ref[...] + _dot(a_ref[...], w_ref[...].astype(BF16))


def matmul_residual(a, w, layer, res, tm=1024, tn=512):
    m, k = a.shape
    n = w.shape[2]
    return pl.pallas_call(
        _matmul_residual_kernel,
        grid=(m // tm, n // tn),
        in_specs=[pl.BlockSpec((tm, k), lambda i, j: (i, 0)),
                  pl.BlockSpec((None, k, tn), lambda i, j: (layer, 0, j)),
                  pl.BlockSpec((tm, tn), lambda i, j: (i, j))],
        out_specs=pl.BlockSpec((tm, tn), lambda i, j: (i, j)),
        out_shape=jax.ShapeDtypeStruct((m, n), F32),
        compiler_params=_params(("parallel", "parallel")),
        name="matmul_residual",
    )(a, w, res)


def _mlp_kernel(x_ref, g_ref, wu_ref, wd_ref, o_ref, h_scr):
    f = pl.program_id(1)

    @pl.when(f == 0)
    def _():
        _rms_rows_to(h_scr, x_ref, g_ref, 128)
        o_ref[...] = x_ref[...]

    u = jnp.maximum(_dot(h_scr[...], wu_ref[...].astype(BF16)), 0.0)
    o_ref[...] += _dot((u * u).astype(BF16), wd_ref[...].astype(BF16))


def mlp_block(x, gain, w_up, w_down, layer, tm=1024, tf=256):
    m, d = x.shape
    ff = w_up.shape[2]
    return pl.pallas_call(
        _mlp_kernel,
        grid=(m // tm, ff // tf),
        in_specs=[pl.BlockSpec((tm, d), lambda i, f: (i, 0)),
                  pl.BlockSpec((1, d), lambda i, f: (0, 0)),
                  pl.BlockSpec((None, d, tf), lambda i, f: (layer, 0, f)),
                  pl.BlockSpec((None, tf, d), lambda i, f: (layer, f, 0))],
        out_specs=pl.BlockSpec((tm, d), lambda i, f: (i, 0)),
        out_shape=jax.ShapeDtypeStruct((m, d), F32),
        scratch_shapes=[pltpu.VMEM((tm, d), BF16)],
        compiler_params=_params(("parallel", "arbitrary")),
        name="mlp_block",
    )(x, gain.reshape(1, d), w_up, w_down)


def rope_tables(pos, head_dim):
    rot = head_dim // ROPE_FRACTION
    half = rot // 2
    inv_freq = jnp.power(jnp.float32(ROPE_THETA), -jnp.arange(half, dtype=F32) / half)
    ang = pos.astype(F32)[:, None] * inv_freq[None, :]
    cos, sin = jnp.cos(ang), jnp.sin(ang)
    n = pos.shape[0]
    zeros = jnp.zeros((n, head_dim - rot), F32)
    c = jnp.concatenate([cos, cos, jnp.ones((n, head_dim - rot), F32)], axis=-1)
    s1 = jnp.concatenate([-sin, jnp.zeros((n, half), F32), zeros], axis=-1)
    s2 = jnp.concatenate([jnp.zeros((n, half), F32), sin, zeros], axis=-1)
    reps = LANES // head_dim
    return tuple(jnp.tile(t, (1, reps)) for t in (c, s1, s2))


def _norm_rope(x, gain, c, s1, s2, half, seg_mean):
    if seg_mean is None:
        ms = jnp.mean(x * x, axis=-1, keepdims=True)
    else:
        ms = _dot(x * x, seg_mean, precision=lax.Precision.HIGHEST)
    y = x * lax.rsqrt(ms + NORM_EPS) * gain
    return y * c + pltpu.roll(y, LANES - half, 1) * s1 + pltpu.roll(y, half, 1) * s2


def _prep_kernel(x_ref, g_ref, c_ref, s1_ref, s2_ref, *rest, half, head_dim, out_scale, split):
    if head_dim < LANES:
        bd_ref, o_ref = rest
        seg_mean = bd_ref[...]
    else:
        (o_ref,) = rest
        seg_mean = None
    y = _norm_rope(x_ref[...], g_ref[...], c_ref[...], s1_ref[...], s2_ref[...], half, seg_mean) * out_scale
    if split:
        for t in range(LANES // head_dim):
            o_ref[t] = y[:, t * head_dim:(t + 1) * head_dim].astype(BF16)
    else:
        o_ref[...] = y.astype(BF16)


def prep_heads(proj, col0, n_heads, head_dim, gain, tables, out_scale=1.0, split=False, tr=512):
    b, s, _ = proj.shape
    n_tiles = n_heads * head_dim // LANES
    half = head_dim // ROPE_FRACTION // 2
    c, s1, s2 = tables
    gain_row = jnp.tile(gain.reshape(1, head_dim), (1, LANES // head_dim))
    tab_spec = pl.BlockSpec((tr, LANES), lambda bi, si, hj: (si, 0))
    in_specs = [pl.BlockSpec((None, tr, LANES), lambda bi, si, hj: (bi, si, col0 // LANES + hj)),
                pl.BlockSpec((1, LANES), lambda bi, si, hj: (0, 0)), tab_spec, tab_spec, tab_spec]
    args = [proj, gain_row, c, s1, s2]
    if head_dim < LANES:
        seg = np.arange(LANES) // head_dim
        args.append(jnp.asarray((seg[:, None] == seg[None, :]).astype(np.float32) / head_dim))
        in_specs.append(pl.BlockSpec((LANES, LANES), lambda bi, si, hj: (0, 0)))
    if split:
        per = LANES // head_dim
        out_shape = jax.ShapeDtypeStruct((b, n_heads, s, head_dim), BF16)
        out_spec = pl.BlockSpec((None, per, tr, head_dim), lambda bi, si, hj: (bi, hj, si, 0))
    else:
        out_shape = jax.ShapeDtypeStruct((b, s, n_heads * head_dim), BF16)
        out_spec = pl.BlockSpec((None, tr, LANES), lambda bi, si, hj: (bi, si, hj))
    return pl.pallas_call(
        functools.partial(_prep_kernel, half=half, head_dim=head_dim, out_scale=out_scale, split=split),
        grid=(b, s // tr, n_tiles),
        in_specs=in_specs, out_specs=out_spec, out_shape=out_shape,
        compiler_params=_params(("parallel", "parallel", "arbitrary")),
        name="prep_heads",
    )(*args)


def _split_kernel(x_ref, o_ref, *, head_dim):
    x = x_ref[...]
    for t in range(LANES // head_dim):
        o_ref[t] = x[:, t * head_dim:(t + 1) * head_dim].astype(BF16)


def split_heads(proj, col0, n_heads, head_dim, tr=512):
    b, s, _ = proj.shape
    per = LANES // head_dim
    return pl.pallas_call(
        functools.partial(_split_kernel, head_dim=head_dim),
        grid=(b, s // tr, n_heads // per),
        in_specs=[pl.BlockSpec((None, tr, LANES), lambda bi, si, hj: (bi, si, col0 // LANES + hj))],
        out_specs=pl.BlockSpec((None, per, tr, head_dim), lambda bi, si, hj: (bi, hj, si, 0)),
        out_shape=jax.ShapeDtypeStruct((b, n_heads, s, head_dim), BF16),
        compiler_params=_params(("parallel", "parallel", "parallel")),
        name="split_heads",
    )(proj)


def _gelu_tanh(x):
    return 0.5 * x * (1.0 + jnp.tanh(np.sqrt(2.0 / np.pi) * (x + 0.044715 * (x * x * x))))


def _nsa_compress_kernel(x_ref, pe_ref, w1_ref, w2_ref, g_ref, c_ref, s1_ref, s2_ref, o_ref):
    n_rows = x_ref.shape[0] // NSA_CMP_STRIDE
    d = NSA_HEAD_DIM
    y0 = jnp.zeros((n_rows, d), F32)
    y1 = jnp.zeros((n_rows, d), F32)
    for r in range(NSA_CMP_STRIDE):
        xr = x_ref[pl.ds(r, n_rows, stride=NSA_CMP_STRIDE), :]
        a0 = (xr + pe_ref[r:r + 1, :]).astype(BF16)
        a1 = (xr + pe_ref[NSA_CMP_STRIDE + r:NSA_CMP_STRIDE + r + 1, :]).astype(BF16)
        y0 = y0 + _dot(a0, w1_ref[r * d:(r + 1) * d, :].astype(BF16))
        y1 = y1 + _dot(a1, w1_ref[(NSA_CMP_STRIDE + r) * d:(NSA_CMP_STRIDE + r + 1) * d, :].astype(BF16))
    pre = y0 + pltpu.roll(y1, n_rows - 1, 0)
    out = _dot(_gelu_tanh(pre).astype(BF16), w2_ref[...].astype(BF16))
    roped = _norm_rope(out, g_ref[...], c_ref[...], s1_ref[...], s2_ref[...],
                       NSA_HEAD_DIM // ROPE_FRACTION // 2, None)
    o_ref[...] = jnp.where(pl.program_id(0) == 0, roped, out).astype(BF16)


def nsa_compress(proj, pe, w1, w2, k_gain, tables):
    b, s, _ = proj.shape
    g, d = NSA_KV_GROUPS, NSA_HEAD_DIM
    n_rows = s // NSA_CMP_STRIDE
    col_blk = NSA_HEADS
    tab_spec = pl.BlockSpec((n_rows, d), lambda kv, bi, gi: (0, 0))
    return pl.pallas_call(
        _nsa_compress_kernel,
        grid=(2, b, g),
        in_specs=[pl.BlockSpec((None, s, d), lambda kv, bi, gi: (bi, 0, col_blk + kv * g + gi)),
                  pl.BlockSpec((None, NSA_CMP_BLOCK, d), lambda kv, bi, gi: (kv, 0, 0)),
                  pl.BlockSpec((None, NSA_CMP_BLOCK * d, d), lambda kv, bi, gi: (kv, 0, 0)),
                  pl.BlockSpec((None, d, d), lambda kv, bi, gi: (kv, 0, 0)),
                  pl.BlockSpec((1, d), lambda kv, bi, gi: (0, 0)),
                  tab_spec, tab_spec, tab_spec],
        out_specs=pl.BlockSpec((None, None, None, n_rows, d), lambda kv, bi, gi: (kv, bi, gi, 0, 0)),
        out_shape=jax.ShapeDtypeStruct((2, b, g, n_rows, d), BF16),
        compiler_params=_params(("parallel", "parallel", "parallel")),
        name="nsa_compress",
    )(proj, pe, w1, w2, k_gain.reshape(1, d), *tables)


def _nsa_attn_kernel(q_ref, kc_ref, vc_ref, ks_ref, vs_ref, kw_ref, vw_ref, gate_ref, ov_ref, o_ref,
                     *, tq, tk, seq):
    gi = pl.program_id(1)
    qi = pl.program_id(2)
    d = NSA_HEAD_DIM
    hpg = NSA_HPG
    scale = d ** -0.5
    n_sel = seq // NSA_SEL_BLOCK
    n_top = min(NSA_TOP_N, n_sel)

    q = q_ref[...]
    q4 = jnp.concatenate([q[:, h * d:(h + 1) * d] for h in range(hpg)], axis=0)
    pos = qi * tq + lax.broadcasted_iota(jnp.int32, (tq, 1), 0)
    pos4 = jnp.concatenate([pos] * hpg, axis=0)

    def tile_heads(m):
        return jnp.concatenate([m] * hpg, axis=0)

    n_c = kc_ref.shape[0]
    s = _dot(q4, kc_ref[...], NT_DIMS) * scale
    cend = lax.broadcasted_iota(jnp.int32, (1, n_c), 1) * NSA_CMP_STRIDE + (NSA_CMP_BLOCK - 1)
    s = jnp.where(cend <= pos4, s, NEG_INF)
    p = jnp.exp(s - jnp.max(s, axis=-1, keepdims=True))
    p = p / jnp.sum(p, axis=-1, keepdims=True)
    p = jnp.where(pos4 >= NSA_CMP_BLOCK - 1, p, 0.0)
    o_cmp = _dot(p.astype(BF16), vc_ref[...])
    p_sum = p[0:tq]
    for h in range(1, hpg):
        p_sum = p_sum + p[h * tq:(h + 1) * tq]
    imp = _dot(p_sum, ov_ref[...], precision=lax.Precision.HIGHEST)

    j = lax.broadcasted_iota(jnp.int32, (1, n_sel), 1)
    bq = pos // NSA_SEL_BLOCK
    forced = (j == 0) | (j == bq) | (j == bq - 1)
    score = jnp.where(j <= bq, imp + jnp.where(forced, NSA_FORCE_BONUS, 0.0), NEG_INF)
    rank = jnp.zeros((tq, n_sel), F32)
    for jp in range(n_sel):
        col = score[:, jp:jp + 1]
        beats = (col > score) | ((col == score) & (j > jp))
        rank = rank + jnp.where(beats, 1.0, 0.0)
    sel = jnp.where(rank < n_top, 1.0, 0.0).astype(BF16)

    blk_row = lax.broadcasted_iota(jnp.int32, (n_sel, tk), 0)
    key_col = lax.broadcasted_iota(jnp.int32, (n_sel, tk), 1)
    key_row = lax.broadcasted_iota(jnp.int32, (1, tk), 1)

    def sel_body(kt, carry):
        m, l, acc = carry
        k0 = pl.multiple_of(kt * tk, tk)
        k = ks_ref[pl.ds(k0, tk), :]
        v = vs_ref[pl.ds(k0, tk), :].astype(BF16)
        sc = _dot(q4, k, NT_DIMS) * scale
        expand = jnp.where((k0 + key_col) // NSA_SEL_BLOCK == blk_row, 1.0, 0.0).astype(BF16)
        picked = _dot(sel, expand)
        ok = (picked > 0.5) & (k0 + key_row <= pos)
        sc = jnp.where(tile_heads(ok), sc, NEG_INF)
        m_new = jnp.maximum(m, jnp.max(sc, axis=-1, keepdims=True))
        alpha = jnp.exp(m - m_new)
        pe = jnp.exp(sc - m_new)
        l = alpha * l + jnp.sum(pe, axis=-1, keepdims=True)
        acc = alpha * acc + _dot(pe.astype(BF16), v)
        return m_new, l, acc

    n_kt = (qi * tq + tq - 1) // tk + 1
    init = (jnp.full((hpg * tq, 1), NEG_INF, F32), jnp.zeros((hpg * tq, 1), F32), jnp.zeros((hpg * tq, d), F32))
    _, l_sel, acc_sel = lax.fori_loop(0, n_kt, sel_body, init)
    o_sel = acc_sel / l_sel

    span = NSA_WINDOW + tq
    w0 = pl.multiple_of(jnp.clip(qi * tq - NSA_WINDOW, 0, seq - span), tq)
    kw = kw_ref[pl.ds(w0, span), :]
    vw = vw_ref[pl.ds(w0, span), :].astype(BF16)
    sw = _dot(q4, kw, NT_DIMS) * scale
    kp = w0 + lax.broadcasted_iota(jnp.int32, (1, span), 1)
    sw = jnp.where((kp <= pos4) & (kp > pos4 - NSA_WINDOW), sw, NEG_INF)
    pw = jnp.exp(sw - jnp.max(sw, axis=-1, keepdims=True))
    o_win = _dot(pw.astype(BF16), vw) / jnp.sum(pw, axis=-1, keepdims=True)

    gt = jax.nn.sigmoid(gate_ref[...])
    lane = lax.broadcasted_iota(jnp.int32, (1, LANES), 1)
    for h in range(hpg):
        rows = slice(h * tq, (h + 1) * tq)
        o_h = jnp.zeros((tq, d), F32)
        for br, o_br in enumerate((o_cmp, o_sel, o_win)):
            g_col = jnp.sum(jnp.where(lane == gi * (3 * hpg) + h * 3 + br, gt, 0.0), axis=-1, keepdims=True)
            o_h = o_h + g_col * o_br[rows]
        o_ref[:, h * d:(h + 1) * d] = o_h.astype(BF16)


def selection_overlap(n_rows, n_sel):
    n_cmp = n_rows - 1
    c0 = np.arange(n_rows) * NSA_CMP_STRIDE
    s0 = np.arange(n_sel) * NSA_SEL_BLOCK
    ov = np.minimum(c0[:, None] + NSA_CMP_BLOCK, s0[None, :] + NSA_SEL_BLOCK) - np.maximum(c0[:, None], s0[None, :])
    ov = np.clip(ov, 0, None) / NSA_CMP_BLOCK
    ov[n_cmp:] = 0.0
    return jnp.asarray(ov, dtype=F32)


def nsa_attention(q, kv_cmp, ks, kw, proj, gate, tq=128, tk=128):
    b, s, _ = q.shape
    g, d, hpg = NSA_KV_GROUPS, NSA_HEAD_DIM, NSA_HPG
    n_rows = s // NSA_CMP_STRIDE
    n_sel = s // NSA_SEL_BLOCK
    vs_blk = NSA_HEADS + 3 * g
    vw_blk = NSA_HEADS + 5 * g
    cmp_spec = lambda kv: pl.BlockSpec((None, None, None, n_rows, d), lambda bi, gi, qi: (kv, bi, gi, 0, 0))
    full_bf = pl.BlockSpec((None, s, d), lambda bi, gi, qi: (bi, 0, gi))
    return pl.pallas_call(
        functools.partial(_nsa_attn_kernel, tq=tq, tk=tk, seq=s),
        grid=(b, g, s // tq),
        in_specs=[pl.BlockSpec((None, tq, hpg * d), lambda bi, gi, qi: (bi, qi, gi)),
                  cmp_spec(0), cmp_spec(1),
                  full_bf,
                  pl.BlockSpec((None, s, d), lambda bi, gi, qi: (bi, 0, vs_blk + gi)),
                  full_bf,
                  pl.BlockSpec((None, s, d), lambda bi, gi, qi: (bi, 0, vw_blk + gi)),
                  pl.BlockSpec((None, tq, LANES), lambda bi, gi, qi: (bi, qi, 0)),
                  pl.BlockSpec((n_rows, n_sel), lambda bi, gi, qi: (0, 0))],
        out_specs=pl.BlockSpec((None, tq, hpg * d), lambda bi, gi, qi: (bi, qi, gi)),
        out_shape=jax.ShapeDtypeStruct((b, s, NSA_HEADS * d), BF16),
        compiler_params=_params(("parallel", "parallel", "parallel")),
        name="nsa_attention",
    )(q, kv_cmp, kv_cmp, ks, proj, kw, proj, gate, selection_overlap(n_rows, n_sel))


def nsa_mixer(x2, b, s, gain, w_in, w_out, layer, q_norm, k_norm, cmp_pe, cmp_w1, cmp_w2):
    d, g = NSA_HEAD_DIM, NSA_KV_GROUPS
    n_main = (NSA_HEADS + 6 * g) * d
    w_gate = jnp.pad(w_in[layer, :, n_main:], ((0, 0), (0, LANES - 3 * NSA_HEADS)))
    proj, gate = norm_matmul(x2, gain, w_in, layer, n_main, w_gate)
    proj = proj.reshape(b, s, n_main)
    gate = gate.reshape(b, s, LANES)
    tabs = rope_tables(jnp.arange(s), d)
    q = prep_heads(proj, 0, NSA_HEADS, d, q_norm, tabs)
    ks = prep_heads(proj, (NSA_HEADS + 2 * g) * d, g, d, k_norm[1], tabs)
    kw = prep_heads(proj, (NSA_HEADS + 4 * g) * d, g, d, k_norm[2], tabs)
    cmp_end = jnp.arange(s // NSA_CMP_STRIDE) * NSA_CMP_STRIDE + (NSA_CMP_BLOCK - 1)
    kv_cmp = nsa_compress(proj, cmp_pe, cmp_w1, cmp_w2, k_norm[0], rope_tables(cmp_end, d))
    o = nsa_attention(q, kv_cmp, ks, kw, proj, gate)
    return matmul_residual(o.reshape(b * s, NSA_HEADS * d), w_out, layer, x2)


def _log_sigmoid(z):
    return jnp.minimum(z, 0.0) - jnp.log1p(jnp.exp(-jnp.abs(z)))


def _gla_kernel(q_ref, k_ref, v_ref, r_ref, glr_ref, wg_ref, bg_ref, on_ref, o_ref, state_ref, *, rows):
    c = GLA_CHUNK
    sub = GLA_SUB

    @pl.when(pl.program_id(2) == 0)
    def _():
        state_ref[...] = jnp.zeros_like(state_ref)

    tri = jnp.where(lax.broadcasted_iota(jnp.int32, (c, c), 0) >= lax.broadcasted_iota(jnp.int32, (c, c), 1), 1.0, 0.0)
    sub_row = lax.broadcasted_iota(jnp.int32, (sub, 1), 0)

    def chunk(ci, carry):
        r0 = pl.multiple_of(ci * c, c)
        qc = q_ref[pl.ds(r0, c), :] * (GLA_KEY_DIM ** -0.5)
        kc = k_ref[pl.ds(r0, c), :]
        vc = v_ref[pl.ds(r0, c), :]
        z = _dot(glr_ref[pl.ds(r0, c), :], wg_ref[...], precision=lax.Precision.HIGHEST) + bg_ref[...]
        log_a = _log_sigmoid(z) / GLA_TAU
        cum = _dot(tri, log_a, precision=lax.Precision.HIGHEST)
        vb = vc.astype(BF16)

        o_inter = _dot((qc * jnp.exp(cum)).astype(BF16), state_ref[...].astype(BF16), NT_DIMS)

        parts = []
        for bi in range(c // sub):
            lo = bi * sub
            q_i, k_i, v_i, c_i = qc[lo:lo + sub], kc[lo:lo + sub], vc[lo:lo + sub], cum[lo:lo + sub]
            o_i = o_inter[lo:lo + sub]
            if bi > 0:
                c0 = c_i[0:1]
                qs = (q_i * jnp.exp(c_i - c0)).astype(BF16)
                kp = (kc[:lo] * jnp.exp(c0 - cum[:lo])).astype(BF16)
                att = _dot(qs, kp, NT_DIMS)
                o_i = o_i + _dot(att.astype(BF16), vb[:lo])
            for jj in range(sub):
                e = jnp.exp(jnp.where(sub_row >= jj, c_i - c_i[jj:jj + 1], NEG_INF))
                col = jnp.sum(q_i * k_i[jj:jj + 1] * e, axis=-1, keepdims=True)
                o_i = o_i + col * v_i[jj:jj + 1]
            parts.append(o_i)
        o = jnp.concatenate(parts, axis=0)

        last = cum[c - 1:c]
        kd = (kc * jnp.exp(last - cum)).astype(BF16)
        state_ref[...] = state_ref[...] * jnp.exp(last) + _dot(vb, kd, TN_DIMS)

        ms = jnp.mean(o * o, axis=-1, keepdims=True)
        on = o * lax.rsqrt(ms + NORM_EPS) * on_ref[...]
        r = r_ref[pl.ds(r0, c), :]
        o_ref[pl.ds(r0, c), :] = (on * (r * jax.nn.sigmoid(r))).astype(BF16)
        return carry

    lax.fori_loop(0, rows // c, chunk, 0)


def gla_attention(proj, glr, w_gate_up, b_gate, o_norm, rows=512):
    b, s, _ = proj.shape
    nh, dk, dv = GLA_HEADS, GLA_KEY_DIM, GLA_VAL_DIM
    wg = jnp.pad(w_gate_up, ((0, LANES - GLA_GATE_RANK), (0, 0)))
    k_blk = nh * dk // dk
    v_blk = 2 * nh * dk // dv
    r_blk = (2 * nh * dk + nh * dv) // dv
    return pl.pallas_call(
        functools.partial(_gla_kernel, rows=rows),
        grid=(b, nh, s // rows),
        in_specs=[pl.BlockSpec((None, rows, dk), lambda bi, h, ci: (bi, ci, h)),
                  pl.BlockSpec((None, rows, dk), lambda bi, h, ci: (bi, ci, k_blk + h)),
                  pl.BlockSpec((None, rows, dv), lambda bi, h, ci: (bi, ci, v_blk + h)),
                  pl.BlockSpec((None, rows, dv), lambda bi, h, ci: (bi, ci, r_blk + h)),
                  pl.BlockSpec((None, rows, LANES), lambda bi, h, ci: (bi, ci, 0)),
                  pl.BlockSpec((LANES, dk), lambda bi, h, ci: (0, h)),
                  pl.BlockSpec((1, dk), lambda bi, h, ci: (0, h)),
                  pl.BlockSpec((1, dv), lambda bi, h, ci: (0, 0))],
        out_specs=pl.BlockSpec((None, rows, dv), lambda bi, h, ci: (bi, ci, h)),
        out_shape=jax.ShapeDtypeStruct((b, s, nh * dv), BF16),
        scratch_shapes=[pltpu.VMEM((dv, dk), F32)],
        compiler_params=_params(("parallel", "parallel", "arbitrary")),
        name="gla_attention",
    )(proj, proj, proj, proj, glr, wg, b_gate.reshape(1, nh * dk), o_norm.reshape(1, dv))


def gla_mixer(x2, b, s, gain, w_in, w_gate_up, b_gate, o_norm, w_out, layer):
    nh, dk, dv = GLA_HEADS, GLA_KEY_DIM, GLA_VAL_DIM
    n_qkv = 2 * nh * dk + nh * dv
    w_main = jnp.concatenate([w_in[:, :n_qkv], w_in[:, n_qkv + GLA_GATE_RANK:]], axis=1)
    w_glr = jnp.pad(w_in[:, n_qkv:n_qkv + GLA_GATE_RANK], ((0, 0), (0, LANES - GLA_GATE_RANK)))
    proj, glr = norm_matmul(x2, gain, w_main[None], 0, n_qkv + nh * dv, w_glr)
    o = gla_attention(proj.reshape(b, s, -1), glr.reshape(b, s, LANES), w_gate_up, b_gate, o_norm)
    return matmul_residual(o.reshape(b * s, nh * dv), w_out, layer, x2)


def _swa_kernel(q_ref, kp_ref, kc_ref, vp_ref, vc_ref, sink_ref, o_ref, *, blk):
    qi = pl.program_id(2)
    hpg, d = SWA_HPG, SWA_HEAD_DIM
    q = q_ref[...].reshape(hpg * blk, d)
    k = jnp.concatenate([kp_ref[...], kc_ref[...]], axis=0)
    v = jnp.concatenate([vp_ref[...], vc_ref[...]], axis=0)
    s = _dot(q, k, NT_DIMS).reshape(hpg, blk, 2 * blk)
    qp = qi * blk + lax.broadcasted_iota(jnp.int32, (blk, 1), 0)
    kp = (qi - 1) * blk + lax.broadcasted_iota(jnp.int32, (1, 2 * blk), 1)
    ok = (kp <= qp) & (kp > qp - SWA_WINDOW) & (kp >= 0)
    s = jnp.where(ok[None], s, NEG_INF)
    sink = sink_ref[...]
    m = jnp.maximum(jnp.max(s, axis=-1, keepdims=True), sink)
    p = jnp.exp(s - m)
    denom = jnp.sum(p, axis=-1, keepdims=True) + jnp.exp(sink - m)
    o = _dot(p.reshape(hpg * blk, 2 * blk).astype(BF16), v).reshape(hpg, blk, d) / denom
    o_ref[...] = o.astype(BF16)


def swa_attention(q, k, v, sinks, blk=128):
    b, nh, s, d = q.shape
    g, hpg = SWA_KV_HEADS, SWA_HPG
    prev = lambda bi, gi, qi: (bi, gi, jnp.maximum(qi - 1, 0), 0)
    cur = lambda bi, gi, qi: (bi, gi, qi, 0)
    kv_spec = lambda im: pl.BlockSpec((None, None, blk, d), im)
    return pl.pallas_call(
        functools.partial(_swa_kernel, blk=blk),
        grid=(b, g, s // blk),
        in_specs=[pl.BlockSpec((None, hpg, blk, d), cur),
                  kv_spec(prev), kv_spec(cur), kv_spec(prev), kv_spec(cur),
                  pl.BlockSpec((None, hpg, 1, 1), lambda bi, gi, qi: (gi, 0, 0, 0))],
        out_specs=pl.BlockSpec((None, hpg, blk, d), cur),
        out_shape=jax.ShapeDtypeStruct((b, nh, s, d), BF16),
        compiler_params=_params(("parallel", "parallel", "parallel")),
        name="swa_attention",
    )(q, k, k, v, v, sinks.astype(F32).reshape(g, hpg, 1, 1))


def swa_mixer(x2, b, s, gain, w_in, w_out, layer, q_norm, k_norm, sinks):
    d, g = SWA_HEAD_DIM, SWA_KV_HEADS
    n_in = (SWA_HEADS + 2 * g) * d
    (proj,) = norm_matmul(x2, gain, w_in, layer, n_in)
    proj = proj.reshape(b, s, n_in)
    tabs = rope_tables(jnp.arange(s), d)
    q = prep_heads(proj, 0, SWA_HEADS, d, q_norm, tabs, out_scale=d ** -0.5, split=True)
    k = prep_heads(proj, SWA_HEADS * d, g, d, k_norm, tabs, split=True)
    v = split_heads(proj, (SWA_HEADS + g) * d, g, d)
    o = swa_attention(q, k, v, sinks)
    o = o.transpose(0, 2, 1, 3).reshape(b * s, SWA_HEADS * d)
    return matmul_residual(o, w_out, layer, x2)


def kernel(x, norm_mix, norm_mlp, mlp_w_up, mlp_w_down, nsa_w_in, nsa_w_out, nsa_q_norm, nsa_k_norm, nsa_cmp_pe, nsa_cmp_w1, nsa_cmp_w2, gla_w_in, gla_w_gate_up, gla_b_gate, gla_o_norm, gla_w_out, swa_w_in, swa_w_out, swa_q_norm, swa_k_norm, swa_sinks):
    b, s, d = x.shape
    x2 = x.reshape(b * s, d)
    ia = ib = ic = 0
    for i in range(norm_mix.shape[0]):
        kind = i % N_MIXERS
        if kind == 0:
            x2 = nsa_mixer(x2, b, s, norm_mix[i], nsa_w_in, nsa_w_out, ia, nsa_q_norm[ia], nsa_k_norm[ia],
                           nsa_cmp_pe[ia], nsa_cmp_w1[ia], nsa_cmp_w2[ia])
            ia += 1
        elif kind == 1:
            x2 = gla_mixer(x2, b, s, norm_mix[i], gla_w_in[ib], gla_w_gate_up[ib], gla_b_gate[ib],
                           gla_o_norm[ib], gla_w_out, ib)
            ib += 1
        else:
            x2 = swa_mixer(x2, b, s, norm_mix[i], swa_w_in, swa_w_out, ic, swa_q_norm[ic], swa_k_norm[ic],
                           swa_sinks[ic])
            ic += 1
        x2 = mlp_block(x2, norm_mlp[i], mlp_w_up, mlp_w_down, i)
    return x2.reshape(b, s, d)
```

```python
import functools

import numpy as np
import jax
import jax.numpy as jnp
from jax import lax
from jax.experimental import pallas as pl
from jax.experimental.pallas import tpu as pltpu

F32 = jnp.float32
BF16 = jnp.bfloat16

NORM_EPS = 1e-6
ROPE_THETA = 500000.0
ROPE_FRACTION = 4
NEG_INF = -1e30
N_MIXERS = 3

NSA_HEAD_DIM = 128
NSA_HEADS = 16
NSA_KV_GROUPS = 4
NSA_HPG = NSA_HEADS // NSA_KV_GROUPS
NSA_CMP_BLOCK = 32
NSA_CMP_STRIDE = 16
NSA_SEL_BLOCK = 64
NSA_TOP_N = 16
NSA_WINDOW = 512
NSA_FORCE_BONUS = 1e4

GLA_HEADS = 4
GLA_KEY_DIM = 256
GLA_VAL_DIM = 512
GLA_GATE_RANK = 16
GLA_TAU = 16.0
GLA_CHUNK = 64
GLA_SUB = 16

SWA_HEAD_DIM = 64
SWA_HEADS = 32
SWA_KV_HEADS = 4
SWA_HPG = SWA_HEADS // SWA_KV_HEADS
SWA_WINDOW = 128

LANES = 128
VMEM_LIMIT = 56 * 1024 * 1024

NT_DIMS = (((1,), (1,)), ((), ()))
TN_DIMS = (((0,), (0,)), ((), ()))


def _params(sem):
    return pltpu.CompilerParams(dimension_semantics=sem, vmem_limit_bytes=VMEM_LIMIT)


def _dot(a, b, dims=None, precision=None):
    if dims is None:
        return jnp.dot(a, b, preferred_element_type=F32, precision=precision)
    return lax.dot_general(a, b, dims, preferred_element_type=F32, precision=precision)


def _rms_rows_to(h_scr, x_ref, g_ref, rows):
    n = x_ref.shape[0] // rows

    def body(i, c):
        r0 = pl.multiple_of(i * rows, rows)
        x = x_ref[pl.ds(r0, rows), :]
        ms = jnp.mean(x * x, axis=-1, keepdims=True)
        h_scr[pl.ds(r0, rows), :] = (x * lax.rsqrt(ms + NORM_EPS) * g_ref[...]).astype(BF16)
        return c

    lax.fori_loop(0, n, body, 0)


def _norm_matmul_kernel(x_ref, g_ref, w_ref, *rest, has_extra):
    if has_extra:
        wx_ref, o_ref, ox_ref, h_scr = rest
    else:
        o_ref, h_scr = rest

    @pl.when(pl.program_id(1) == 0)
    def _():
        _rms_rows_to(h_scr, x_ref, g_ref, 128)
        if has_extra:
            ox_ref[...] = _dot(h_scr[...], wx_ref[...].astype(BF16))

    o_ref[...] = _dot(h_scr[...], w_ref[...].astype(BF16))


def norm_matmul(x, gain, w, layer, n_cols, w_extra=None, tm=1024, tn=512):
    m, d = x.shape
    has_extra = w_extra is not None
    in_specs = [pl.BlockSpec((tm, d), lambda i, j: (i, 0)),
                pl.BlockSpec((1, d), lambda i, j: (0, 0)),
                pl.BlockSpec((None, d, tn), lambda i, j: (layer, 0, j))]
    out_shape = [jax.ShapeDtypeStruct((m, n_cols), F32)]
    out_specs = [pl.BlockSpec((tm, tn), lambda i, j: (i, j))]
    args = [x, gain.reshape(1, d), w]
    if has_extra:
        in_specs.append(pl.BlockSpec((d, LANES), lambda i, j: (0, 0)))
        out_shape.append(jax.ShapeDtypeStruct((m, LANES), F32))
        out_specs.append(pl.BlockSpec((tm, LANES), lambda i, j: (i, 0)))
        args.append(w_extra)
    return pl.pallas_call(
        functools.partial(_norm_matmul_kernel, has_extra=has_extra),
        grid=(m // tm, n_cols // tn),
        in_specs=in_specs, out_specs=out_specs, out_shape=out_shape,
        scratch_shapes=[pltpu.VMEM((tm, d), BF16)],
        compiler_params=_params(("parallel", "arbitrary")),
        name="norm_matmul",
    )(*args)


def _matmul_residual_kernel(a_ref, w_ref, r_ref, o_ref):
    o_ref[...] = r_ref[...] + _dot(a_ref[...], w_ref[...].astype(BF16))


def matmul_residual(a, w, layer, res, tm=1024, tn=512):
    m, k = a.shape
    n = w.shape[2]
    return pl.pallas_call(
        _matmul_residual_kernel,
        grid=(m // tm, n // tn),
        in_specs=[pl.BlockSpec((tm, k), lambda i, j: (i, 0)),
                  pl.BlockSpec((None, k, tn), lambda i, j: (layer, 0, j)),
                  pl.BlockSpec((tm, tn), lambda i, j: (i, j))],
        out_specs=pl.BlockSpec((tm, tn), lambda i, j: (i, j)),
        out_shape=jax.ShapeDtypeStruct((m, n), F32),
        compiler_params=_params(("parallel", "parallel")),
        name="matmul_residual",
    )(a, w, res)


def _mlp_kernel(x_ref, g_ref, wu_ref, xt_ref, wd_ref, o_ref, h_scr, u_scr, *, nf, tf):
    j = pl.program_id(1)

    @pl.when(j == 0)
    def _():
        _rms_rows_to(h_scr, x_ref, g_ref, 128)

    @pl.when(j < nf)
    def _():
        u = jnp.maximum(_dot(h_scr[...], wu_ref[...]), 0.0)
        u_scr[j] = (u * u).astype(BF16)

    @pl.when(j >= nf)
    def _():
        acc = xt_ref[...]
        for f in range(nf):
            acc = acc + _dot(u_scr[f], wd_ref[f * tf:(f + 1) * tf, :])
        o_ref[...] = acc


def mlp_block(x, gain, w_up, w_down, layer, tm=1024, tf=512, tn=256):
    m, d = x.shape
    ff = w_up.shape[2]
    nf, nn = ff // tf, d // tn
    down = lambda j: jnp.maximum(j - nf, 0)
    return pl.pallas_call(
        functools.partial(_mlp_kernel, nf=nf, tf=tf),
        grid=(m // tm, nf + nn),
        in_specs=[pl.BlockSpec((tm, d), lambda i, j: (i, 0), pipeline_mode=pl.Buffered(1)),
                  pl.BlockSpec((1, d), lambda i, j: (0, 0)),
                  pl.BlockSpec((None, d, tf), lambda i, j: (layer, 0, jnp.minimum(j, nf - 1))),
                  pl.BlockSpec((tm, tn), lambda i, j: (i, down(j))),
                  pl.BlockSpec((None, ff, tn), lambda i, j: (layer, 0, down(j)))],
        out_specs=pl.BlockSpec((tm, tn), lambda i, j: (i, down(j))),
        out_shape=jax.ShapeDtypeStruct((m, d), F32),
        scratch_shapes=[pltpu.VMEM((tm, d), BF16), pltpu.VMEM((nf, tm, tf), BF16)],
        compiler_params=_params(("parallel", "arbitrary")),
        name="mlp_block",
    )(x, gain.reshape(1, d), w_up, x, w_down)


def rope_tables(pos, head_dim):
    rot = head_dim // ROPE_FRACTION
    half = rot // 2
    inv_freq = jnp.power(jnp.float32(ROPE_THETA), -jnp.arange(half, dtype=F32) / half)
    ang = pos.astype(F32)[:, None] * inv_freq[None, :]
    cos, sin = jnp.cos(ang), jnp.sin(ang)
    n = pos.shape[0]
    zeros = jnp.zeros((n, head_dim - rot), F32)
    c = jnp.concatenate([cos, cos, jnp.ones((n, head_dim - rot), F32)], axis=-1)
    s1 = jnp.concatenate([-sin, jnp.zeros((n, half), F32), zeros], axis=-1)
    s2 = jnp.concatenate([jnp.zeros((n, half), F32), sin, zeros], axis=-1)
    reps = LANES // head_dim
    return tuple(jnp.tile(t, (1, reps)) for t in (c, s1, s2))


def _norm_rope(x, gain, c, s1, s2, half, ms=None):
    if ms is None:
        ms = jnp.mean(x * x, axis=-1, keepdims=True)
    y = x * lax.rsqrt(ms + NORM_EPS) * gain
    return y * c + pltpu.roll(y, LANES - half, 1) * s1 + pltpu.roll(y, half, 1) * s2


def _gelu_tanh(x):
    return 0.5 * x * (1.0 + jnp.tanh(np.sqrt(2.0 / np.pi) * (x + 0.044715 * (x * x * x))))


def _nsa_compress_kernel(x_ref, pe_ref, w1_ref, w2_ref, g_ref, c_ref, s1_ref, s2_ref, o_ref):
    n_rows = x_ref.shape[0] // NSA_CMP_STRIDE
    d = NSA_HEAD_DIM
    y0 = jnp.zeros((n_rows, d), F32)
    y1 = jnp.zeros((n_rows, d), F32)
    for r in range(NSA_CMP_STRIDE):
        xr = x_ref[pl.ds(r, n_rows, stride=NSA_CMP_STRIDE), :]
        a0 = (xr + pe_ref[r:r + 1, :]).astype(BF16)
        a1 = (xr + pe_ref[NSA_CMP_STRIDE + r:NSA_CMP_STRIDE + r + 1, :]).astype(BF16)
        y0 = y0 + _dot(a0, w1_ref[r * d:(r + 1) * d, :].astype(BF16))
        y1 = y1 + _dot(a1, w1_ref[(NSA_CMP_STRIDE + r) * d:(NSA_CMP_STRIDE + r + 1) * d, :].astype(BF16))
    pre = y0 + pltpu.roll(y1, n_rows - 1, 0)
    out = _dot(_gelu_tanh(pre).astype(BF16), w2_ref[...].astype(BF16))
    roped = _norm_rope(out, g_ref[...], c_ref[...], s1_ref[...], s2_ref[...],
                       NSA_HEAD_DIM // ROPE_FRACTION // 2, None)
    o_ref[...] = jnp.where(pl.program_id(0) == 0, roped, out).astype(BF16)


def nsa_compress(proj, pe, w1, w2, k_gain, tables):
    b, s, _ = proj.shape
    g, d = NSA_KV_GROUPS, NSA_HEAD_DIM
    n_rows = s // NSA_CMP_STRIDE
    col_blk = NSA_HEADS
    tab_spec = pl.BlockSpec((n_rows, d), lambda kv, bi, gi: (0, 0))
    return pl.pallas_call(
        _nsa_compress_kernel,
        grid=(2, b, g),
        in_specs=[pl.BlockSpec((None, s, d), lambda kv, bi, gi: (bi, 0, col_blk + kv * g + gi)),
                  pl.BlockSpec((None, NSA_CMP_BLOCK, d), lambda kv, bi, gi: (kv, 0, 0)),
                  pl.BlockSpec((None, NSA_CMP_BLOCK * d, d), lambda kv, bi, gi: (kv, 0, 0)),
                  pl.BlockSpec((None, d, d), lambda kv, bi, gi: (kv, 0, 0)),
                  pl.BlockSpec((1, d), lambda kv, bi, gi: (0, 0)),
                  tab_spec, tab_spec, tab_spec],
        out_specs=pl.BlockSpec((None, None, None, n_rows, d), lambda kv, bi, gi: (kv, bi, gi, 0, 0)),
        out_shape=jax.ShapeDtypeStruct((2, b, g, n_rows, d), BF16),
        compiler_params=_params(("parallel", "parallel", "parallel")),
        name="nsa_compress",
    )(proj, pe, w1, w2, k_gain.reshape(1, d), *tables)


def _nsa_attn_kernel(q_ref, kc_ref, vc_ref, ksr_ref, vsr_ref, kwr_ref, vwr_ref, gate_ref, ovt_ref,
                     qg_ref, kg_ref, c_ref, s1_ref, s2_ref, o_ref, ks_ref, vs_ref, kw_ref, vw_ref,
                     *, tq, tk, seq):
    gi = pl.program_id(1)
    qi = pl.program_id(2)
    d = NSA_HEAD_DIM
    hpg = NSA_HPG
    scale = d ** -0.5
    n_sel = seq // NSA_SEL_BLOCK
    n_top = min(NSA_TOP_N, n_sel)
    half = d // ROPE_FRACTION // 2

    def tables(rows):
        return c_ref[rows, :], s1_ref[rows, :], s2_ref[rows, :]

    @pl.when(qi == 0)
    def _():
        chunk = 512

        def body(i, carry):
            rows = pl.ds(pl.multiple_of(i * chunk, chunk), chunk)
            ks_ref[rows, :] = _norm_rope(ksr_ref[rows, :], kg_ref[0:1, :], *tables(rows), half, None).astype(BF16)
            kw_ref[rows, :] = _norm_rope(kwr_ref[rows, :], kg_ref[1:2, :], *tables(rows), half, None).astype(BF16)
            vs_ref[rows, :] = vsr_ref[rows, :].astype(BF16)
            vw_ref[rows, :] = vwr_ref[rows, :].astype(BF16)
            return carry

        lax.fori_loop(0, seq // chunk, body, 0)

    q_tabs = tables(pl.ds(pl.multiple_of(qi * tq, tq), tq))
    q4 = jnp.concatenate(
        [_norm_rope(q_ref[:, h * d:(h + 1) * d], qg_ref[...], *q_tabs, half, None).astype(BF16) for h in range(hpg)],
        axis=0)
    pos = qi * tq + lax.broadcasted_iota(jnp.int32, (tq, 1), 0)
    pos_l = qi * tq + lax.broadcasted_iota(jnp.int32, (1, tq), 1)

    def masked_scores(k, bias):
        sc = _dot(q4, k, NT_DIMS)
        return sc.reshape(hpg, tq, -1) * scale + bias[None]

    def stack(t):
        return t.reshape(hpg * tq, -1)

    n_c = kc_ref.shape[0]
    cend = lax.broadcasted_iota(jnp.int32, (1, n_c), 1) * NSA_CMP_STRIDE + (NSA_CMP_BLOCK - 1)
    s = masked_scores(kc_ref[...], jnp.where(cend <= pos, 0.0, NEG_INF))
    p = jnp.exp(s - jnp.max(s, axis=-1, keepdims=True))
    p = p / jnp.sum(p, axis=-1, keepdims=True)
    p = jnp.where((pos >= NSA_CMP_BLOCK - 1)[None], p, 0.0)
    o_cmp = _dot(stack(p).astype(BF16), vc_ref[...])
    p_sum = jnp.sum(p, axis=0)
    imp_t = _dot(ovt_ref[...], p_sum, NT_DIMS, precision=lax.Precision.HIGHEST)

    j = lax.broadcasted_iota(jnp.int32, (n_sel, 1), 0)
    bq = pos_l // NSA_SEL_BLOCK
    forced = (j == 0) | (j == bq) | (j == bq - 1)
    score = jnp.where(j <= bq, imp_t + jnp.where(forced, NSA_FORCE_BONUS, 0.0), NEG_INF)
    rank = jnp.zeros((n_sel, tq), F32)
    j_full = lax.broadcasted_iota(jnp.int32, (n_sel, tq), 0)
    for jp in range(n_sel):
        row = score[jp:jp + 1, :]
        beats = (row > score) | ((row == score) & (j_full > jp))
        rank = rank + jnp.where(beats, 1.0, 0.0)
    sel = jnp.where(rank < n_top, 1.0, 0.0).T.astype(BF16)

    blk_row = lax.broadcasted_iota(jnp.int32, (n_sel, tk), 0)
    key_col = lax.broadcasted_iota(jnp.int32, (n_sel, tk), 1)
    key_row = lax.broadcasted_iota(jnp.int32, (1, tk), 1)

    def sel_body(kt, carry):
        m, l, acc = carry
        k0 = pl.multiple_of(kt * tk, tk)
        v = vs_ref[pl.ds(k0, tk), :]
        expand = jnp.where((k0 + key_col) // NSA_SEL_BLOCK == blk_row, 1.0, 0.0).astype(BF16)
        picked = _dot(sel, expand)
        bias = jnp.where((picked > 0.5) & (k0 + key_row <= pos), 0.0, NEG_INF)
        sc = masked_scores(ks_ref[pl.ds(k0, tk), :], bias)
        m_new = jnp.maximum(m, jnp.max(sc, axis=-1, keepdims=True))
        alpha = jnp.exp(m - m_new)
        pe = jnp.exp(sc - m_new)
        l = alpha * l + jnp.sum(pe, axis=-1, keepdims=True)
        acc = stack(alpha) * acc + _dot(stack(pe).astype(BF16), v)
        return m_new, l, acc

    n_kt = (qi * tq + tq - 1) // tk + 1
    init = (jnp.full((hpg, tq, 1), NEG_INF, F32), jnp.zeros((hpg, tq, 1), F32), jnp.zeros((hpg * tq, d), F32))
    _, l_sel, acc_sel = lax.fori_loop(0, n_kt, sel_body, init)
    o_sel = acc_sel / stack(l_sel)

    span = NSA_WINDOW + tq
    w0 = pl.multiple_of(jnp.clip(qi * tq - NSA_WINDOW, 0, seq - span), tq)
    vw = vw_ref[pl.ds(w0, span), :]
    kp = w0 + lax.broadcasted_iota(jnp.int32, (1, span), 1)
    sw = masked_scores(kw_ref[pl.ds(w0, span), :],
                       jnp.where((kp <= pos) & (kp > pos - NSA_WINDOW), 0.0, NEG_INF))
    pw = jnp.exp(sw - jnp.max(sw, axis=-1, keepdims=True))
    o_win = _dot(stack(pw).astype(BF16), vw) / stack(jnp.sum(pw, axis=-1, keepdims=True))

    gt = jax.nn.sigmoid(gate_ref[...])
    lane = lax.broadcasted_iota(jnp.int32, (1, LANES), 1)
    for h in range(hpg):
        rows = slice(h * tq, (h + 1) * tq)
        o_h = jnp.zeros((tq, d), F32)
        for br, o_br in enumerate((o_cmp, o_sel, o_win)):
            g_col = jnp.sum(jnp.where(lane == gi * (3 * hpg) + h * 3 + br, gt, 0.0), axis=-1, keepdims=True)
            o_h = o_h + g_col * o_br[rows]
        o_ref[:, h * d:(h + 1) * d] = o_h.astype(BF16)


def selection_overlap(n_rows, n_sel):
    n_cmp = n_rows - 1
    c0 = np.arange(n_rows) * NSA_CMP_STRIDE
    s0 = np.arange(n_sel) * NSA_SEL_BLOCK
    ov = np.minimum(c0[:, None] + NSA_CMP_BLOCK, s0[None, :] + NSA_SEL_BLOCK) - np.maximum(c0[:, None], s0[None, :])
    ov = np.clip(ov, 0, None) / NSA_CMP_BLOCK
    ov[n_cmp:] = 0.0
    return jnp.asarray(ov, dtype=F32)


def nsa_attention(proj, kv_cmp, gate, q_gain, k_gains, tabs, tq=128, tk=512):
    b, s, _ = proj.shape
    g, d, hpg = NSA_KV_GROUPS, NSA_HEAD_DIM, NSA_HPG
    n_rows = s // NSA_CMP_STRIDE
    n_sel = s // NSA_SEL_BLOCK
    seg_blk = lambda i: NSA_HEADS + i * g
    cmp_spec = lambda kv: pl.BlockSpec((None, None, None, n_rows, d), lambda bi, gi, qi: (kv, bi, gi, 0, 0))
    seq_spec = lambda i: pl.BlockSpec((None, s, d), lambda bi, gi, qi: (bi, 0, seg_blk(i) + gi))
    const = lambda shape: pl.BlockSpec(shape, lambda bi, gi, qi: (0, 0))
    return pl.pallas_call(
        functools.partial(_nsa_attn_kernel, tq=tq, tk=tk, seq=s),
        grid=(b, g, s // tq),
        in_specs=[pl.BlockSpec((None, tq, hpg * d), lambda bi, gi, qi: (bi, qi, gi)),
                  cmp_spec(0), cmp_spec(1),
                  seq_spec(2), seq_spec(3), seq_spec(4), seq_spec(5),
                  pl.BlockSpec((None, tq, LANES), lambda bi, gi, qi: (bi, qi, 0)),
                  const((n_sel, n_rows)), const((1, d)), const((2, d)),
                  const((s, d)), const((s, d)), const((s, d))],
        out_specs=pl.BlockSpec((None, tq, hpg * d), lambda bi, gi, qi: (bi, qi, gi)),
        out_shape=jax.ShapeDtypeStruct((b, s, NSA_HEADS * d), BF16),
        scratch_shapes=[pltpu.VMEM((s, d), BF16)] * 4,
        compiler_params=_params(("parallel", "parallel", "arbitrary")),
        name="nsa_attention",
    )(proj, kv_cmp, kv_cmp, proj, proj, proj, proj, gate, selection_overlap(n_rows, n_sel).T,
      q_gain.reshape(1, d), k_gains, *tabs)


def nsa_mixer(x2, b, s, gain, w_in, w_out, layer, q_norm, k_norm, cmp_pe, cmp_w1, cmp_w2):
    d, g = NSA_HEAD_DIM, NSA_KV_GROUPS
    n_main = (NSA_HEADS + 6 * g) * d
    w_gate = jnp.pad(w_in[layer, :, n_main:], ((0, 0), (0, LANES - 3 * NSA_HEADS)))
    proj, gate = norm_matmul(x2, gain, w_in, layer, n_main, w_gate)
    proj = proj.reshape(b, s, n_main)
    gate = gate.reshape(b, s, LANES)
    tabs = rope_tables(jnp.arange(s), d)
    cmp_end = jnp.arange(s // NSA_CMP_STRIDE) * NSA_CMP_STRIDE + (NSA_CMP_BLOCK - 1)
    kv_cmp = nsa_compress(proj, cmp_pe, cmp_w1, cmp_w2, k_norm[0], rope_tables(cmp_end, d))
    o = nsa_attention(proj, kv_cmp, gate, q_norm, k_norm[1:3], tabs)
    return matmul_residual(o.reshape(b * s, NSA_HEADS * d), w_out, layer, x2)


def _log_sigmoid(z):
    return jnp.minimum(z, 0.0) - jnp.log1p(jnp.exp(-jnp.abs(z)))


def _gla_kernel(q_ref, k_ref, v_ref, r_ref, glr_ref, wg_ref, bg_ref, on_ref, o_ref, state_ref, *, rows):
    c = GLA_CHUNK
    sub = GLA_SUB

    @pl.when(pl.program_id(2) == 0)
    def _():
        state_ref[...] = jnp.zeros_like(state_ref)

    tri = jnp.where(lax.broadcasted_iota(jnp.int32, (c, c), 0) >= lax.broadcasted_iota(jnp.int32, (c, c), 1), 1.0, 0.0)
    sub_row = lax.broadcasted_iota(jnp.int32, (sub, 1), 0)

    def chunk(ci, carry):
        r0 = pl.multiple_of(ci * c, c)
        qc = q_ref[pl.ds(r0, c), :] * (GLA_KEY_DIM ** -0.5)
        kc = k_ref[pl.ds(r0, c), :]
        vc = v_ref[pl.ds(r0, c), :]
        z = _dot(glr_ref[pl.ds(r0, c), :], wg_ref[...], precision=lax.Precision.HIGHEST) + bg_ref[...]
        log_a = _log_sigmoid(z) / GLA_TAU
        cum = _dot(tri, log_a, precision=lax.Precision.HIGHEST)
        vb = vc.astype(BF16)

        o_inter = _dot((qc * jnp.exp(cum)).astype(BF16), state_ref[...].astype(BF16), NT_DIMS)

        parts = []
        for bi in range(c // sub):
            lo = bi * sub
            q_i, k_i, v_i, c_i = qc[lo:lo + sub], kc[lo:lo + sub], vc[lo:lo + sub], cum[lo:lo + sub]
            o_i = o_inter[lo:lo + sub]
            if bi > 0:
                c0 = c_i[0:1]
                qs = (q_i * jnp.exp(c_i - c0)).astype(BF16)
                kp = (kc[:lo] * jnp.exp(c0 - cum[:lo])).astype(BF16)
                att = _dot(qs, kp, NT_DIMS)
                o_i = o_i + _dot(att.astype(BF16), vb[:lo])
            for jj in range(sub):
                e = jnp.exp(jnp.where(sub_row >= jj, c_i - c_i[jj:jj + 1], NEG_INF))
                col = jnp.sum(q_i * k_i[jj:jj + 1] * e, axis=-1, keepdims=True)
                o_i = o_i + col * v_i[jj:jj + 1]
            parts.append(o_i)
        o = jnp.concatenate(parts, axis=0)

        last = cum[c - 1:c]
        kd = (kc * jnp.exp(last - cum)).astype(BF16)
        state_ref[...] = state_ref[...] * jnp.exp(last) + _dot(vb, kd, TN_DIMS)

        ms = jnp.mean(o * o, axis=-1, keepdims=True)
        on = o * lax.rsqrt(ms + NORM_EPS) * on_ref[...]
        r = r_ref[pl.ds(r0, c), :]
        o_ref[pl.ds(r0, c), :] = (on * (r * jax.nn.sigmoid(r))).astype(BF16)
        return carry

    lax.fori_loop(0, rows // c, chunk, 0)


def gla_attention(proj, glr, w_gate_up, b_gate, o_norm, rows=512):
    b, s, _ = proj.shape
    nh, dk, dv = GLA_HEADS, GLA_KEY_DIM, GLA_VAL_DIM
    wg = jnp.pad(w_gate_up, ((0, LANES - GLA_GATE_RANK), (0, 0)))
    k_blk = nh * dk // dk
    v_blk = 2 * nh * dk // dv
    r_blk = (2 * nh * dk + nh * dv) // dv
    return pl.pallas_call(
        functools.partial(_gla_kernel, rows=rows),
        grid=(b, nh, s // rows),
        in_specs=[pl.BlockSpec((None, rows, dk), lambda bi, h, ci: (bi, ci, h)),
                  pl.BlockSpec((None, rows, dk), lambda bi, h, ci: (bi, ci, k_blk + h)),
                  pl.BlockSpec((None, rows, dv), lambda bi, h, ci: (bi, ci, v_blk + h)),
                  pl.BlockSpec((None, rows, dv), lambda bi, h, ci: (bi, ci, r_blk + h)),
                  pl.BlockSpec((None, rows, LANES), lambda bi, h, ci: (bi, ci, 0)),
                  pl.BlockSpec((LANES, dk), lambda bi, h, ci: (0, h)),
                  pl.BlockSpec((1, dk), lambda bi, h, ci: (0, h)),
                  pl.BlockSpec((1, dv), lambda bi, h, ci: (0, 0))],
        out_specs=pl.BlockSpec((None, rows, dv), lambda bi, h, ci: (bi, ci, h)),
        out_shape=jax.ShapeDtypeStruct((b, s, nh * dv), BF16),
        scratch_shapes=[pltpu.VMEM((dv, dk), F32)],
        compiler_params=_params(("parallel", "parallel", "arbitrary")),
        name="gla_attention",
    )(proj, proj, proj, proj, glr, wg, b_gate.reshape(1, nh * dk), o_norm.reshape(1, dv))


def gla_mixer(x2, b, s, gain, w_in, w_gate_up, b_gate, o_norm, w_out, layer):
    nh, dk, dv = GLA_HEADS, GLA_KEY_DIM, GLA_VAL_DIM
    n_qkv = 2 * nh * dk + nh * dv
    w_main = jnp.concatenate([w_in[:, :n_qkv], w_in[:, n_qkv + GLA_GATE_RANK:]], axis=1)
    w_glr = jnp.pad(w_in[:, n_qkv:n_qkv + GLA_GATE_RANK], ((0, 0), (0, LANES - GLA_GATE_RANK)))
    proj, glr = norm_matmul(x2, gain, w_main[None], 0, n_qkv + nh * dv, w_glr)
    o = gla_attention(proj.reshape(b, s, -1), glr.reshape(b, s, LANES), w_gate_up, b_gate, o_norm)
    return matmul_residual(o.reshape(b * s, nh * dv), w_out, layer, x2)


def _swa_heads(x, gain, tabs, seg, out_scale):
    blk = x.shape[0]
    d = SWA_HEAD_DIM
    n = x.shape[1] // LANES
    tiles = jnp.concatenate([x[:, t * LANES:(t + 1) * LANES] for t in range(n)], axis=0)
    sq = tiles * tiles
    hi = sq.astype(BF16)
    lo = (sq - hi.astype(F32)).astype(BF16)
    ms = (_dot(hi, seg) + _dot(lo, seg)) * (1.0 / d)
    heads = []
    for t in range(n):
        rows = slice(t * blk, (t + 1) * blk)
        y = _norm_rope(tiles[rows], gain, *tabs, d // ROPE_FRACTION // 2, ms[rows]) * out_scale
        heads += [y[:, u * d:(u + 1) * d] for u in range(LANES // d)]
    return heads


def _swa_kernel(q_ref, kp_ref, kc_ref, vp_ref, vc_ref, sink_ref, qg_ref, kg_ref, seg_ref,
                cc_ref, s1c_ref, s2c_ref, cp_ref, s1p_ref, s2p_ref, o_ref, *, blk):
    qi = pl.program_id(1)
    hpg, d, g = SWA_HPG, SWA_HEAD_DIM, SWA_KV_HEADS
    seg = seg_ref[...]
    tabs_c = (cc_ref[...], s1c_ref[...], s2c_ref[...])
    tabs_p = (cp_ref[...], s1p_ref[...], s2p_ref[...])
    q_heads = _swa_heads(q_ref[...], qg_ref[...], tabs_c, seg, d ** -0.5)
    kp_heads = _swa_heads(kp_ref[...], kg_ref[...], tabs_p, seg, 1.0)
    kc_heads = _swa_heads(kc_ref[...], kg_ref[...], tabs_c, seg, 1.0)
    vp, vc = vp_ref[...], vc_ref[...]

    qp = qi * blk + lax.broadcasted_iota(jnp.int32, (blk, 1), 0)
    kpos = (qi - 1) * blk + lax.broadcasted_iota(jnp.int32, (1, 2 * blk), 1)
    bias = jnp.where((kpos <= qp) & (kpos > qp - SWA_WINDOW) & (kpos >= 0), 0.0, NEG_INF)

    for gi in range(g):
        q = jnp.concatenate(q_heads[gi * hpg:(gi + 1) * hpg], axis=0).astype(BF16)
        k = jnp.concatenate([kp_heads[gi], kc_heads[gi]], axis=0).astype(BF16)
        v = jnp.concatenate([vp[:, gi * d:(gi + 1) * d], vc[:, gi * d:(gi + 1) * d]], axis=0).astype(BF16)
        s = _dot(q, k, NT_DIMS).reshape(hpg, blk, 2 * blk) + bias[None]
        sink = sink_ref[gi]
        m = jnp.maximum(jnp.max(s, axis=-1, keepdims=True), sink)
        p = jnp.exp(s - m)
        denom = jnp.sum(p, axis=-1, keepdims=True) + jnp.exp(sink - m)
        o = _dot(p.reshape(hpg * blk, 2 * blk).astype(BF16), v).reshape(hpg, blk, d) / denom
        o_ref[:, gi * hpg * d:(gi + 1) * hpg * d] = jnp.concatenate([o[h] for h in range(hpg)], axis=1).astype(BF16)


def swa_attention(proj, sinks, q_gain, k_gain, tabs, blk=128):
    b, s, _ = proj.shape
    g, hpg, d = SWA_KV_HEADS, SWA_HPG, SWA_HEAD_DIM
    nq, nkv = SWA_HEADS * d, g * d
    k_blk = nq // nkv
    prev = lambda qi: jnp.maximum(qi - 1, 0)
    kv_spec = lambda col, row: pl.BlockSpec((None, blk, nkv), lambda bi, qi: (bi, row(qi), col))
    const = lambda shape: pl.BlockSpec(shape, lambda bi, qi: (0,) * len(shape))
    tab_spec = lambda row: pl.BlockSpec((blk, LANES), lambda bi, qi: (row(qi), 0))
    cur = lambda qi: qi
    seg_id = np.arange(LANES) // d
    seg = jnp.asarray(seg_id[:, None] == seg_id[None, :], dtype=BF16)
    tile_gain = lambda gn: jnp.tile(gn.reshape(1, d), (1, LANES // d))
    return pl.pallas_call(
        functools.partial(_swa_kernel, blk=blk),
        grid=(b, s // blk),
        in_specs=[pl.BlockSpec((None, blk, nq), lambda bi, qi: (bi, qi, 0)),
                  kv_spec(k_blk, prev), kv_spec(k_blk, cur), kv_spec(k_blk + 1, prev), kv_spec(k_blk + 1, cur),
                  const((g, hpg, 1, 1)), const((1, LANES)), const((1, LANES)), const((LANES, LANES)),
                  tab_spec(cur), tab_spec(cur), tab_spec(cur), tab_spec(prev), tab_spec(prev), tab_spec(prev)],
        out_specs=pl.BlockSpec((None, blk, nq), lambda bi, qi: (bi, qi, 0)),
        out_shape=jax.ShapeDtypeStruct((b, s, nq), BF16),
        compiler_params=_params(("parallel", "parallel")),
        name="swa_attention",
    )(proj, proj, proj, proj, proj, sinks.astype(F32).reshape(g, hpg, 1, 1), tile_gain(q_gain), tile_gain(k_gain),
      seg, *tabs, *tabs)


def swa_mixer(x2, b, s, gain, w_in, w_out, layer, q_norm, k_norm, sinks):
    d, g = SWA_HEAD_DIM, SWA_KV_HEADS
    n_in = (SWA_HEADS + 2 * g) * d
    (proj,) = norm_matmul(x2, gain, w_in, layer, n_in)
    o = swa_attention(proj.reshape(b, s, n_in), sinks, q_norm, k_norm, rope_tables(jnp.arange(s), d))
    return matmul_residual(o.reshape(b * s, SWA_HEADS * d), w_out, layer, x2)


def kernel(x, norm_mix, norm_mlp, mlp_w_up, mlp_w_down, nsa_w_in, nsa_w_out, nsa_q_norm, nsa_k_norm, nsa_cmp_pe, nsa_cmp_w1, nsa_cmp_w2, gla_w_in, gla_w_gate_up, gla_b_gate, gla_o_norm, gla_w_out, swa_w_in, swa_w_out, swa_q_norm, swa_k_norm, swa_sinks):
    b, s, d = x.shape
    x2 = x.reshape(b * s, d)
    mlp_w_up = mlp_w_up.astype(BF16)
    mlp_w_down = mlp_w_down.astype(BF16)
    ia = ib = ic = 0
    for i in range(norm_mix.shape[0]):
        kind = i % N_MIXERS
        if kind == 0:
            x2 = nsa_mixer(x2, b, s, norm_mix[i], nsa_w_in, nsa_w_out, ia, nsa_q_norm[ia], nsa_k_norm[ia],
                           nsa_cmp_pe[ia], nsa_cmp_w1[ia], nsa_cmp_w2[ia])
            ia += 1
        elif kind == 1:
            x2 = gla_mixer(x2, b, s, norm_mix[i], gla_w_in[ib], gla_w_gate_up[ib], gla_b_gate[ib],
                           gla_o_norm[ib], gla_w_out, ib)
            ib += 1
        else:
            x2 = swa_mixer(x2, b, s, norm_mix[i], swa_w_in, swa_w_out, ic, swa_q_norm[ic], swa_k_norm[ic],
                           swa_sinks[ic])
            ic += 1
        x2 = mlp_block(x2, norm_mlp[i], mlp_w_up, mlp_w_down, i)
    return x2.reshape(b, s, d)
```

```python
import functools

import numpy as np
import jax
import jax.numpy as jnp
from jax import lax
from jax.experimental import pallas as pl
from jax.experimental.pallas import tpu as pltpu

F32 = jnp.float32
BF16 = jnp.bfloat16

NORM_EPS = 1e-6
ROPE_THETA = 500000.0
ROPE_FRACTION = 4
NEG_INF = -1e30
N_MIXERS = 3

NSA_HEAD_DIM = 128
NSA_HEADS = 16
NSA_KV_GROUPS = 4
NSA_HPG = NSA_HEADS // NSA_KV_GROUPS
NSA_CMP_BLOCK = 32
NSA_CMP_STRIDE = 16
NSA_SEL_BLOCK = 64
NSA_TOP_N = 16
NSA_WINDOW = 512
NSA_FORCE_BONUS = 1e4
NSA_SCALE2 = float(NSA_HEAD_DIM ** -0.5 * np.log2(np.e))
NSA_MASK_NEG = -2.0 ** 100

GLA_HEADS = 4
GLA_KEY_DIM = 256
GLA_VAL_DIM = 512
GLA_GATE_RANK = 16
GLA_TAU = 16.0
GLA_CHUNK = 64
GLA_SUB = 16

SWA_HEAD_DIM = 64
SWA_HEADS = 32
SWA_KV_HEADS = 4
SWA_HPG = SWA_HEADS // SWA_KV_HEADS
SWA_WINDOW = 128

LANES = 128
VMEM_LIMIT = 56 * 1024 * 1024

NT_DIMS = (((1,), (1,)), ((), ()))
TN_DIMS = (((0,), (0,)), ((), ()))


def _params(sem):
    return pltpu.CompilerParams(dimension_semantics=sem, vmem_limit_bytes=VMEM_LIMIT)


def _dot(a, b, dims=None, precision=None):
    if dims is None:
        return jnp.dot(a, b, preferred_element_type=F32, precision=precision)
    return lax.dot_general(a, b, dims, preferred_element_type=F32, precision=precision)


def _rms_rows_to(h_scr, x_ref, g_ref, rows):
    n = x_ref.shape[0] // rows

    def body(i, c):
        r0 = pl.multiple_of(i * rows, rows)
        x = x_ref[pl.ds(r0, rows), :]
        ms = jnp.mean(x * x, axis=-1, keepdims=True)
        h_scr[pl.ds(r0, rows), :] = (x * lax.rsqrt(ms + NORM_EPS) * g_ref[...]).astype(BF16)
        return c

    lax.fori_loop(0, n, body, 0)


def _norm_matmul_kernel(x_ref, g_ref, w_ref, *rest, has_extra):
    if has_extra:
        wx_ref, o_ref, ox_ref, h_scr = rest
    else:
        o_ref, h_scr = rest

    @pl.when(pl.program_id(1) == 0)
    def _():
        _rms_rows_to(h_scr, x_ref, g_ref, 128)
        if has_extra:
            ox_ref[...] = _dot(h_scr[...], wx_ref[...].astype(BF16))

    o_ref[...] = _dot(h_scr[...], w_ref[...].astype(BF16))


def norm_matmul(x, gain, w, layer, n_cols, w_extra=None, tm=2048, tn=512):
    m, d = x.shape
    tm = min(tm, m)
    has_extra = w_extra is not None
    in_specs = [pl.BlockSpec((tm, d), lambda i, j: (i, 0), pipeline_mode=pl.Buffered(1)),
                pl.BlockSpec((1, d), lambda i, j: (0, 0)),
                pl.BlockSpec((None, d, tn), lambda i, j: (layer, 0, j))]
    out_shape = [jax.ShapeDtypeStruct((m, n_cols), F32)]
    out_specs = [pl.BlockSpec((tm, tn), lambda i, j: (i, j))]
    args = [x, gain.reshape(1, d), w]
    if has_extra:
        in_specs.append(pl.BlockSpec((d, LANES), lambda i, j: (0, 0)))
        out_shape.append(jax.ShapeDtypeStruct((m, LANES), F32))
        out_specs.append(pl.BlockSpec((tm, LANES), lambda i, j: (i, 0)))
        args.append(w_extra)
    return pl.pallas_call(
        functools.partial(_norm_matmul_kernel, has_extra=has_extra),
        grid=(m // tm, n_cols // tn),
        in_specs=in_specs, out_specs=out_specs, out_shape=out_shape,
        scratch_shapes=[pltpu.VMEM((tm, d), BF16)],
        compiler_params=_params(("parallel", "arbitrary")),
        name="norm_matmul",
    )(*args)


def _matmul_residual_kernel(a_ref, w_ref, r_ref, o_ref):
    o_ref[...] = r_ref[...] + _dot(a_ref[...], w_ref[...].astype(BF16))


def matmul_residual(a, w, layer, res, tm=2048, tn=512):
    m, k = a.shape
    tm = min(tm, m)
    n = w.shape[2]
    return pl.pallas_call(
        _matmul_residual_kernel,
        grid=(m // tm, n // tn),
        in_specs=[pl.BlockSpec((tm, k), lambda i, j: (i, 0)),
                  pl.BlockSpec((None, k, tn), lambda i, j: (layer, 0, j)),
                  pl.BlockSpec((tm, tn), lambda i, j: (i, j))],
        out_specs=pl.BlockSpec((tm, tn), lambda i, j: (i, j)),
        out_shape=jax.ShapeDtypeStruct((m, n), F32),
        compiler_params=_params(("parallel", "parallel")),
        name="matmul_residual",
    )(a, w, res)


def _mlp_kernel(x_ref, g_ref, wu_ref, xt_ref, wd_ref, o_ref, h_scr, u_scr, *, nf, tf):
    j = pl.program_id(1)

    @pl.when(j == 0)
    def _():
        _rms_rows_to(h_scr, x_ref, g_ref, 128)

    @pl.when(j < nf)
    def _():
        u = jnp.maximum(_dot(h_scr[...], wu_ref[...]), 0.0)
        u_scr[j] = (u * u).astype(BF16)

    @pl.when(j >= nf)
    def _():
        acc = xt_ref[...]
        for f in range(nf):
            acc = acc + _dot(u_scr[f], wd_ref[f * tf:(f + 1) * tf, :])
        o_ref[...] = acc


def mlp_block(x, gain, w_up, w_down, layer, tm=1024, tf=512, tn=256):
    m, d = x.shape
    ff = w_up.shape[2]
    nf, nn = ff // tf, d // tn
    down = lambda j: jnp.maximum(j - nf, 0)
    return pl.pallas_call(
        functools.partial(_mlp_kernel, nf=nf, tf=tf),
        grid=(m // tm, nf + nn),
        in_specs=[pl.BlockSpec((tm, d), lambda i, j: (i, 0), pipeline_mode=pl.Buffered(1)),
                  pl.BlockSpec((1, d), lambda i, j: (0, 0)),
                  pl.BlockSpec((None, d, tf), lambda i, j: (layer, 0, jnp.minimum(j, nf - 1))),
                  pl.BlockSpec((tm, tn), lambda i, j: (i, down(j))),
                  pl.BlockSpec((None, ff, tn), lambda i, j: (layer, 0, down(j)))],
        out_specs=pl.BlockSpec((tm, tn), lambda i, j: (i, down(j))),
        out_shape=jax.ShapeDtypeStruct((m, d), F32),
        scratch_shapes=[pltpu.VMEM((tm, d), BF16), pltpu.VMEM((nf, tm, tf), BF16)],
        compiler_params=_params(("parallel", "arbitrary")),
        name="mlp_block",
    )(x, gain.reshape(1, d), w_up, x, w_down)


def rope_tables(pos, head_dim):
    rot = head_dim // ROPE_FRACTION
    half = rot // 2
    inv_freq = jnp.power(jnp.float32(ROPE_THETA), -jnp.arange(half, dtype=F32) / half)
    ang = pos.astype(F32)[:, None] * inv_freq[None, :]
    cos, sin = jnp.cos(ang), jnp.sin(ang)
    n = pos.shape[0]
    zeros = jnp.zeros((n, head_dim - rot), F32)
    c = jnp.concatenate([cos, cos, jnp.ones((n, head_dim - rot), F32)], axis=-1)
    s1 = jnp.concatenate([-sin, jnp.zeros((n, half), F32), zeros], axis=-1)
    s2 = jnp.concatenate([jnp.zeros((n, half), F32), sin, zeros], axis=-1)
    reps = LANES // head_dim
    return tuple(jnp.tile(t, (1, reps)) for t in (c, s1, s2))


def _norm_rope(x, gain, c, s1, s2, half, ms=None):
    if ms is None:
        ms = jnp.mean(x * x, axis=-1, keepdims=True)
    y = x * lax.rsqrt(ms + NORM_EPS) * gain
    return y * c + pltpu.roll(y, LANES - half, 1) * s1 + pltpu.roll(y, half, 1) * s2


def _gelu_tanh(x):
    return 0.5 * x * (1.0 + jnp.tanh(np.sqrt(2.0 / np.pi) * (x + 0.044715 * (x * x * x))))


def _nsa_compress_kernel(x_ref, pe_ref, w1_ref, w2_ref, g_ref, c_ref, s1_ref, s2_ref, o_ref):
    n_rows = x_ref.shape[0] // NSA_CMP_STRIDE
    d = NSA_HEAD_DIM
    y0 = jnp.zeros((n_rows, d), F32)
    y1 = jnp.zeros((n_rows, d), F32)
    for r in range(NSA_CMP_STRIDE):
        xr = x_ref[pl.ds(r, n_rows, stride=NSA_CMP_STRIDE), :]
        a0 = (xr + pe_ref[r:r + 1, :]).astype(BF16)
        a1 = (xr + pe_ref[NSA_CMP_STRIDE + r:NSA_CMP_STRIDE + r + 1, :]).astype(BF16)
        y0 = y0 + _dot(a0, w1_ref[r * d:(r + 1) * d, :].astype(BF16))
        y1 = y1 + _dot(a1, w1_ref[(NSA_CMP_STRIDE + r) * d:(NSA_CMP_STRIDE + r + 1) * d, :].astype(BF16))
    pre = y0 + pltpu.roll(y1, n_rows - 1, 0)
    out = _dot(_gelu_tanh(pre).astype(BF16), w2_ref[...].astype(BF16))
    roped = _norm_rope(out, g_ref[...], c_ref[...], s1_ref[...], s2_ref[...],
                       NSA_HEAD_DIM // ROPE_FRACTION // 2, None)
    o_ref[...] = jnp.where(pl.program_id(0) == 0, roped * NSA_SCALE2, out).astype(BF16)


def nsa_compress(proj, pe, w1, w2, k_gain, tables):
    b, s, _ = proj.shape
    g, d = NSA_KV_GROUPS, NSA_HEAD_DIM
    n_rows = s // NSA_CMP_STRIDE
    col_blk = NSA_HEADS
    tab_spec = pl.BlockSpec((n_rows, d), lambda kv, bi, gi: (0, 0))
    return pl.pallas_call(
        _nsa_compress_kernel,
        grid=(2, b, g),
        in_specs=[pl.BlockSpec((None, s, d), lambda kv, bi, gi: (bi, 0, col_blk + kv * g + gi)),
                  pl.BlockSpec((None, NSA_CMP_BLOCK, d), lambda kv, bi, gi: (kv, 0, 0)),
                  pl.BlockSpec((None, NSA_CMP_BLOCK * d, d), lambda kv, bi, gi: (kv, 0, 0)),
                  pl.BlockSpec((None, d, d), lambda kv, bi, gi: (kv, 0, 0)),
                  pl.BlockSpec((1, d), lambda kv, bi, gi: (0, 0)),
                  tab_spec, tab_spec, tab_spec],
        out_specs=pl.BlockSpec((None, None, None, n_rows, d), lambda kv, bi, gi: (kv, bi, gi, 0, 0)),
        out_shape=jax.ShapeDtypeStruct((2, b, g, n_rows, d), BF16),
        compiler_params=_params(("parallel", "parallel", "parallel")),
        name="nsa_compress",
    )(proj, pe, w1, w2, k_gain.reshape(1, d), *tables)


def _nsa_attn_kernel(q_ref, kc_ref, vc_ref, ksr_ref, vsr_ref, kwr_ref, vwr_ref, gate_ref, ovt_ref,
                     qg_ref, kg_ref, c_ref, s1_ref, s2_ref, o_ref, ks_ref, vs_ref, kw_ref, vw_ref,
                     *, tq, tk, seq):
    gi = pl.program_id(1)
    qi = pl.program_id(2)
    d = NSA_HEAD_DIM
    hpg = NSA_HPG
    n_sel = seq // NSA_SEL_BLOCK
    n_top = min(NSA_TOP_N, n_sel)
    half = d // ROPE_FRACTION // 2

    def tables(rows):
        return c_ref[rows, :], s1_ref[rows, :], s2_ref[rows, :]

    scale2 = NSA_SCALE2

    @pl.when(qi == 0)
    def _():
        chunk = 512
        lane = lax.broadcasted_iota(jnp.int32, (chunk, d), 1)
        row = lax.broadcasted_iota(jnp.int32, (chunk, d), 0)

        def body(i, carry):
            r0 = pl.multiple_of(i * chunk, chunk)
            rows = pl.ds(r0, chunk)
            ks = _norm_rope(ksr_ref[rows, :], kg_ref[0:1, :], *tables(rows), half, None) * scale2
            kw = _norm_rope(kwr_ref[rows, :], kg_ref[1:2, :], *tables(rows), half, None) * scale2
            ks_ref[rows, 0:d] = ks.astype(BF16)
            ks_ref[rows, d:2 * d] = jnp.where((r0 + row) // NSA_SEL_BLOCK == lane, NSA_MASK_NEG, 0.0).astype(BF16)
            kw_ref[rows, :] = kw.astype(BF16)
            vs_ref[rows, :] = vsr_ref[rows, :].astype(BF16)
            vw_ref[rows, :] = vwr_ref[rows, :].astype(BF16)
            return carry

        lax.fori_loop(0, seq // chunk, body, 0)

    q_tabs = tables(pl.ds(pl.multiple_of(qi * tq, tq), tq))
    q4 = jnp.concatenate(
        [_norm_rope(q_ref[:, h * d:(h + 1) * d], qg_ref[...], *q_tabs, half, None).astype(BF16) for h in range(hpg)],
        axis=0)
    pos = qi * tq + lax.broadcasted_iota(jnp.int32, (tq, 1), 0)
    pos_l = qi * tq + lax.broadcasted_iota(jnp.int32, (1, tq), 1)

    def masked_scores(qs, k, bias=None):
        sc = _dot(qs, k, NT_DIMS)
        sc = sc.reshape(-1, tq, sc.shape[-1])
        return sc if bias is None else sc + bias[None]

    def stack(t):
        return t.reshape(t.shape[0] * tq, -1)

    span = NSA_WINDOW + tq
    w0 = pl.multiple_of(jnp.clip(qi * tq - NSA_WINDOW, 0, seq - span), tq)
    vw = vw_ref[pl.ds(w0, span), :]
    kp = w0 + lax.broadcasted_iota(jnp.int32, (1, span), 1)
    sw = masked_scores(q4, kw_ref[pl.ds(w0, span), :],
                       jnp.where((kp <= pos) & (kp > pos - NSA_WINDOW), 0.0, NEG_INF))
    pw = jnp.exp2(sw - jnp.max(sw, axis=-1, keepdims=True))
    o_win = _dot(stack(pw).astype(BF16), vw) / stack(jnp.sum(pw, axis=-1, keepdims=True))

    n_c = kc_ref.shape[0]
    cend = lax.broadcasted_iota(jnp.int32, (1, n_c), 1) * NSA_CMP_STRIDE + (NSA_CMP_BLOCK - 1)
    s = masked_scores(q4, kc_ref[...], jnp.where(cend <= pos, 0.0, NEG_INF))
    p = jnp.exp2(s - jnp.max(s, axis=-1, keepdims=True))
    p = p / jnp.sum(p, axis=-1, keepdims=True)
    p = jnp.where((pos >= NSA_CMP_BLOCK - 1)[None], p, 0.0)
    o_cmp = _dot(stack(p).astype(BF16), vc_ref[...])
    p_sum = jnp.sum(p, axis=0)
    imp_t = _dot(ovt_ref[...], p_sum, NT_DIMS, precision=lax.Precision.HIGHEST)

    j = lax.broadcasted_iota(jnp.int32, (n_sel, 1), 0)
    bq = pos_l // NSA_SEL_BLOCK
    forced = (j == 0) | (j == bq) | (j == bq - 1)
    score = jnp.where(j <= bq, imp_t + jnp.where(forced, NSA_FORCE_BONUS, 0.0), NEG_INF)
    sub = 8
    groups = [score[v * sub:(v + 1) * sub] for v in range(n_sel // sub)]
    ranks = [jnp.zeros((sub, tq), F32) for _ in groups]
    j_loc = lax.broadcasted_iota(jnp.int32, (sub, tq), 0)
    for jp in range(n_sel):
        row = score[jp:jp + 1, :]
        for v, sg in enumerate(groups):
            if v * sub > jp:
                beats = row >= sg
            elif (v + 1) * sub - 1 <= jp:
                beats = row > sg
            else:
                beats = (row > sg) | ((row == sg) & (j_loc > jp - v * sub))
            ranks[v] = ranks[v] + jnp.where(beats, 1.0, 0.0)
    rank = jnp.concatenate(ranks, axis=0)
    not_sel = jnp.where(rank < n_top, 0.0, 1.0)
    not_sel = jnp.concatenate([not_sel, jnp.zeros((LANES - n_sel, tq), F32)], axis=0).T.astype(BF16)

    q_aug = jnp.concatenate([q4, jnp.concatenate([not_sel] * hpg, axis=0)], axis=1)

    def sel_tile(kt, carry, bias):
        m, l, acc = carry
        k0 = pl.multiple_of(kt * tk, tk)
        sc = masked_scores(q_aug, ks_ref[pl.ds(k0, tk), :], bias)
        m_new = jnp.maximum(m, jnp.max(sc, axis=-1, keepdims=True))
        alpha = jnp.exp2(m - m_new)
        pe = jnp.exp2(sc - m_new)
        l = alpha * l + jnp.sum(pe, axis=-1, keepdims=True)
        acc = stack(alpha) * acc + _dot(stack(pe).astype(BF16), vs_ref[pl.ds(k0, tk), :])
        return m_new, l, acc

    last = (qi * tq + tq - 1) // tk
    init = (jnp.full((hpg, tq, 1), NEG_INF, F32), jnp.zeros((hpg, tq, 1), F32), jnp.zeros((hpg * tq, d), F32))
    carry = lax.fori_loop(0, last, lambda kt, c: sel_tile(kt, c, None), init)
    key_pos = last * tk + lax.broadcasted_iota(jnp.int32, (1, tk), 1)
    _, l_sel, acc_sel = sel_tile(last, carry, jnp.where(key_pos <= pos, 0.0, NEG_INF))
    o_sel = acc_sel / stack(l_sel)

    gt = jax.nn.sigmoid(gate_ref[...])
    lane = lax.broadcasted_iota(jnp.int32, (1, LANES), 1)
    for h in range(hpg):
        rows = slice(h * tq, (h + 1) * tq)
        o_h = jnp.zeros((tq, d), F32)
        for br, o_br in enumerate((o_cmp, o_sel, o_win)):
            g_col = jnp.sum(jnp.where(lane == gi * (3 * hpg) + h * 3 + br, gt, 0.0), axis=-1, keepdims=True)
            o_h = o_h + g_col * o_br[rows]
        o_ref[:, h * d:(h + 1) * d] = o_h.astype(BF16)


def selection_overlap(n_rows, n_sel):
    n_cmp = n_rows - 1
    c0 = np.arange(n_rows) * NSA_CMP_STRIDE
    s0 = np.arange(n_sel) * NSA_SEL_BLOCK
    ov = np.minimum(c0[:, None] + NSA_CMP_BLOCK, s0[None, :] + NSA_SEL_BLOCK) - np.maximum(c0[:, None], s0[None, :])
    ov = np.clip(ov, 0, None) / NSA_CMP_BLOCK
    ov[n_cmp:] = 0.0
    return jnp.asarray(ov, dtype=F32)


def nsa_attention(proj, kv_cmp, gate, q_gain, k_gains, tabs, tq=128, tk=512):
    b, s, _ = proj.shape
    g, d, hpg = NSA_KV_GROUPS, NSA_HEAD_DIM, NSA_HPG
    n_rows = s // NSA_CMP_STRIDE
    n_sel = s // NSA_SEL_BLOCK
    seg_blk = lambda i: NSA_HEADS + i * g
    cmp_spec = lambda kv: pl.BlockSpec((None, None, None, n_rows, d), lambda bi, gi, qi: (kv, bi, gi, 0, 0))
    seq_spec = lambda i: pl.BlockSpec((None, s, d), lambda bi, gi, qi: (bi, 0, seg_blk(i) + gi))
    const = lambda shape: pl.BlockSpec(shape, lambda bi, gi, qi: (0, 0))
    return pl.pallas_call(
        functools.partial(_nsa_attn_kernel, tq=tq, tk=tk, seq=s),
        grid=(b, g, s // tq),
        in_specs=[pl.BlockSpec((None, tq, hpg * d), lambda bi, gi, qi: (bi, qi, gi)),
                  cmp_spec(0), cmp_spec(1),
                  seq_spec(2), seq_spec(3), seq_spec(4), seq_spec(5),
                  pl.BlockSpec((None, tq, LANES), lambda bi, gi, qi: (bi, qi, 0)),
                  const((n_sel, n_rows)), const((1, d)), const((2, d)),
                  const((s, d)), const((s, d)), const((s, d))],
        out_specs=pl.BlockSpec((None, tq, hpg * d), lambda bi, gi, qi: (bi, qi, gi)),
        out_shape=jax.ShapeDtypeStruct((b, s, NSA_HEADS * d), BF16),
        scratch_shapes=[pltpu.VMEM((s, 2 * d), BF16)] + [pltpu.VMEM((s, d), BF16)] * 3,
        compiler_params=_params(("parallel", "parallel", "arbitrary")),
        name="nsa_attention",
    )(proj, kv_cmp, kv_cmp, proj, proj, proj, proj, gate, selection_overlap(n_rows, n_sel).T,
      q_gain.reshape(1, d), k_gains, *tabs)


def nsa_mixer(x2, b, s, gain, w_in, w_out, layer, q_norm, k_norm, cmp_pe, cmp_w1, cmp_w2):
    d, g = NSA_HEAD_DIM, NSA_KV_GROUPS
    n_main = (NSA_HEADS + 6 * g) * d
    w_gate = jnp.pad(w_in[layer, :, n_main:], ((0, 0), (0, LANES - 3 * NSA_HEADS)))
    proj, gate = norm_matmul(x2, gain, w_in, layer, n_main, w_gate)
    proj = proj.reshape(b, s, n_main)
    gate = gate.reshape(b, s, LANES)
    tabs = rope_tables(jnp.arange(s), d)
    cmp_end = jnp.arange(s // NSA_CMP_STRIDE) * NSA_CMP_STRIDE + (NSA_CMP_BLOCK - 1)
    kv_cmp = nsa_compress(proj, cmp_pe, cmp_w1, cmp_w2, k_norm[0], rope_tables(cmp_end, d))
    o = nsa_attention(proj, kv_cmp, gate, q_norm, k_norm[1:3], tabs)
    return matmul_residual(o.reshape(b * s, NSA_HEADS * d), w_out, layer, x2)


def _log_sigmoid(z):
    return jnp.minimum(z, 0.0) - jnp.log1p(jnp.exp(-jnp.abs(z)))


def _gla_kernel(q_ref, k_ref, v_ref, r_ref, glr_ref, wg_ref, bg_ref, on_ref, o_ref, state_ref, *, rows, hps):
    c = GLA_CHUNK
    sub = GLA_SUB
    dk, dv = GLA_KEY_DIM, GLA_VAL_DIM

    @pl.when(pl.program_id(2) == 0)
    def _():
        state_ref[...] = jnp.zeros_like(state_ref)

    tri = jnp.where(lax.broadcasted_iota(jnp.int32, (c, c), 0) >= lax.broadcasted_iota(jnp.int32, (c, c), 1), 1.0, 0.0)
    sub_row = lax.broadcasted_iota(jnp.int32, (sub, 1), 0)

    def head_chunk(hh, r0):
        kcols = slice(hh * dk, (hh + 1) * dk)
        vcols = slice(hh * dv, (hh + 1) * dv)
        qc = q_ref[pl.ds(r0, c), kcols] * (dk ** -0.5)
        kc = k_ref[pl.ds(r0, c), kcols]
        vc = v_ref[pl.ds(r0, c), vcols]
        z = _dot(glr_ref[pl.ds(r0, c), :], wg_ref[:, kcols], precision=lax.Precision.HIGHEST) + bg_ref[:, kcols]
        log_a = _log_sigmoid(z) / GLA_TAU
        cum = _dot(tri, log_a, precision=lax.Precision.HIGHEST)
        vb = vc.astype(BF16)

        state = state_ref[hh]
        o_inter = _dot((qc * jnp.exp(cum)).astype(BF16), state.astype(BF16), NT_DIMS)

        parts = []
        for bi in range(c // sub):
            lo = bi * sub
            q_i, k_i, v_i, c_i = qc[lo:lo + sub], kc[lo:lo + sub], vc[lo:lo + sub], cum[lo:lo + sub]
            o_i = o_inter[lo:lo + sub]
            if bi > 0:
                c0 = c_i[0:1]
                qs = (q_i * jnp.exp(c_i - c0)).astype(BF16)
                kp = (kc[:lo] * jnp.exp(c0 - cum[:lo])).astype(BF16)
                att = _dot(qs, kp, NT_DIMS)
                o_i = o_i + _dot(att.astype(BF16), vb[:lo])
            for jj in range(sub):
                e = jnp.exp(jnp.where(sub_row >= jj, c_i - c_i[jj:jj + 1], NEG_INF))
                col = jnp.sum(q_i * k_i[jj:jj + 1] * e, axis=-1, keepdims=True)
                o_i = o_i + col * v_i[jj:jj + 1]
            parts.append(o_i)
        o = jnp.concatenate(parts, axis=0)

        last = cum[c - 1:c]
        kd = (kc * jnp.exp(last - cum)).astype(BF16)
        state_ref[hh] = state * jnp.exp(last) + _dot(vb, kd, TN_DIMS)

        ms = jnp.mean(o * o, axis=-1, keepdims=True)
        on = o * lax.rsqrt(ms + NORM_EPS) * on_ref[...]
        r = r_ref[pl.ds(r0, c), vcols]
        o_ref[pl.ds(r0, c), vcols] = (on * (r * jax.nn.sigmoid(r))).astype(BF16)

    def chunk(ci, carry):
        r0 = pl.multiple_of(ci * c, c)
        for hh in range(hps):
            head_chunk(hh, r0)
        return carry

    lax.fori_loop(0, rows // c, chunk, 0)


def gla_attention(proj, glr, w_gate_up, b_gate, o_norm, rows=512, hps=2):
    b, s, _ = proj.shape
    nh, dk, dv = GLA_HEADS, GLA_KEY_DIM, GLA_VAL_DIM
    wg = jnp.pad(w_gate_up, ((0, LANES - GLA_GATE_RANK), (0, 0)))
    kw, vw = hps * dk, hps * dv
    k_blk = nh * dk // kw
    v_blk = 2 * nh * dk // vw
    r_blk = (2 * nh * dk + nh * dv) // vw
    return pl.pallas_call(
        functools.partial(_gla_kernel, rows=rows, hps=hps),
        grid=(b, nh // hps, s // rows),
        in_specs=[pl.BlockSpec((None, rows, kw), lambda bi, h, ci: (bi, ci, h)),
                  pl.BlockSpec((None, rows, kw), lambda bi, h, ci: (bi, ci, k_blk + h)),
                  pl.BlockSpec((None, rows, vw), lambda bi, h, ci: (bi, ci, v_blk + h)),
                  pl.BlockSpec((None, rows, vw), lambda bi, h, ci: (bi, ci, r_blk + h)),
                  pl.BlockSpec((None, rows, LANES), lambda bi, h, ci: (bi, ci, 0)),
                  pl.BlockSpec((LANES, kw), lambda bi, h, ci: (0, h)),
                  pl.BlockSpec((1, kw), lambda bi, h, ci: (0, h)),
                  pl.BlockSpec((1, dv), lambda bi, h, ci: (0, 0))],
        out_specs=pl.BlockSpec((None, rows, vw), lambda bi, h, ci: (bi, ci, h)),
        out_shape=jax.ShapeDtypeStruct((b, s, nh * dv), BF16),
        scratch_shapes=[pltpu.VMEM((hps, dv, dk), F32)],
        compiler_params=_params(("parallel", "parallel", "arbitrary")),
        name="gla_attention",
    )(proj, proj, proj, proj, glr, wg, b_gate.reshape(1, nh * dk), o_norm.reshape(1, dv))


def gla_mixer(x2, b, s, gain, w_in, w_gate_up, b_gate, o_norm, w_out, layer):
    nh, dk, dv = GLA_HEADS, GLA_KEY_DIM, GLA_VAL_DIM
    n_qkv = 2 * nh * dk + nh * dv
    w_main = jnp.concatenate([w_in[:, :n_qkv], w_in[:, n_qkv + GLA_GATE_RANK:]], axis=1)
    w_glr = jnp.pad(w_in[:, n_qkv:n_qkv + GLA_GATE_RANK], ((0, 0), (0, LANES - GLA_GATE_RANK)))
    proj, glr = norm_matmul(x2, gain, w_main[None], 0, n_qkv + nh * dv, w_glr)
    o = gla_attention(proj.reshape(b, s, -1), glr.reshape(b, s, LANES), w_gate_up, b_gate, o_norm)
    return matmul_residual(o.reshape(b * s, nh * dv), w_out, layer, x2)


def _swa_heads(x, gain, tabs, seg, out_scale):
    blk = x.shape[0]
    d = SWA_HEAD_DIM
    n = x.shape[1] // LANES
    tiles = jnp.concatenate([x[:, t * LANES:(t + 1) * LANES] for t in range(n)], axis=0)
    sq = tiles * tiles
    hi = sq.astype(BF16)
    lo = (sq - hi.astype(F32)).astype(BF16)
    ms = (_dot(hi, seg) + _dot(lo, seg)) * (1.0 / d)
    heads = []
    for t in range(n):
        rows = slice(t * blk, (t + 1) * blk)
        y = _norm_rope(tiles[rows], gain, *tabs, d // ROPE_FRACTION // 2, ms[rows]) * out_scale
        heads += [y[:, u * d:(u + 1) * d] for u in range(LANES // d)]
    return heads


def _swa_kernel(q_ref, kp_ref, kc_ref, vp_ref, vc_ref, sink_ref, qg_ref, kg_ref, seg_ref,
                cc_ref, s1c_ref, s2c_ref, cp_ref, s1p_ref, s2p_ref, o_ref, *, blk):
    qi = pl.program_id(1)
    hpg, d, g = SWA_HPG, SWA_HEAD_DIM, SWA_KV_HEADS
    seg = seg_ref[...]
    tabs_c = (cc_ref[...], s1c_ref[...], s2c_ref[...])
    tabs_p = (cp_ref[...], s1p_ref[...], s2p_ref[...])
    q_heads = _swa_heads(q_ref[...], qg_ref[...], tabs_c, seg, d ** -0.5)
    kp_heads = _swa_heads(kp_ref[...], kg_ref[...], tabs_p, seg, 1.0)
    kc_heads = _swa_heads(kc_ref[...], kg_ref[...], tabs_c, seg, 1.0)
    vp, vc = vp_ref[...], vc_ref[...]

    qp = qi * blk + lax.broadcasted_iota(jnp.int32, (blk, 1), 0)
    kpos = (qi - 1) * blk + lax.broadcasted_iota(jnp.int32, (1, 2 * blk), 1)
    bias = jnp.where((kpos <= qp) & (kpos > qp - SWA_WINDOW) & (kpos >= 0), 0.0, NEG_INF)

    for gi in range(g):
        q = jnp.concatenate(q_heads[gi * hpg:(gi + 1) * hpg], axis=0).astype(BF16)
        k = jnp.concatenate([kp_heads[gi], kc_heads[gi]], axis=0).astype(BF16)
        v = jnp.concatenate([vp[:, gi * d:(gi + 1) * d], vc[:, gi * d:(gi + 1) * d]], axis=0).astype(BF16)
        s = _dot(q, k, NT_DIMS).reshape(hpg, blk, 2 * blk) + bias[None]
        sink = sink_ref[gi]
        m = jnp.maximum(jnp.max(s, axis=-1, keepdims=True), sink)
        p = jnp.exp(s - m)
        denom = jnp.sum(p, axis=-1, keepdims=True) + jnp.exp(sink - m)
        o = _dot(p.reshape(hpg * blk, 2 * blk).astype(BF16), v).reshape(hpg, blk, d) / denom
        o_ref[:, gi * hpg * d:(gi + 1) * hpg * d] = jnp.concatenate([o[h] for h in range(hpg)], axis=1).astype(BF16)


def swa_attention(proj, sinks, q_gain, k_gain, tabs, blk=128):
    b, s, _ = proj.shape
    g, hpg, d = SWA_KV_HEADS, SWA_HPG, SWA_HEAD_DIM
    nq, nkv = SWA_HEADS * d, g * d
    k_blk = nq // nkv
    prev = lambda qi: jnp.maximum(qi - 1, 0)
    kv_spec = lambda col, row: pl.BlockSpec((None, blk, nkv), lambda bi, qi: (bi, row(qi), col))
    const = lambda shape: pl.BlockSpec(shape, lambda bi, qi: (0,) * len(shape))
    tab_spec = lambda row: pl.BlockSpec((blk, LANES), lambda bi, qi: (row(qi), 0))
    cur = lambda qi: qi
    seg_id = np.arange(LANES) // d
    seg = jnp.asarray(seg_id[:, None] == seg_id[None, :], dtype=BF16)
    tile_gain = lambda gn: jnp.tile(gn.reshape(1, d), (1, LANES // d))
    return pl.pallas_call(
        functools.partial(_swa_kernel, blk=blk),
        grid=(b, s // blk),
        in_specs=[pl.BlockSpec((None, blk, nq), lambda bi, qi: (bi, qi, 0)),
                  kv_spec(k_blk, prev), kv_spec(k_blk, cur), kv_spec(k_blk + 1, prev), kv_spec(k_blk + 1, cur),
                  const((g, hpg, 1, 1)), const((1, LANES)), const((1, LANES)), const((LANES, LANES)),
                  tab_spec(cur), tab_spec(cur), tab_spec(cur), tab_spec(prev), tab_spec(prev), tab_spec(prev)],
        out_specs=pl.BlockSpec((None, blk, nq), lambda bi, qi: (bi, qi, 0)),
        out_shape=jax.ShapeDtypeStruct((b, s, nq), BF16),
        compiler_params=_params(("parallel", "parallel")),
        name="swa_attention",
    )(proj, proj, proj, proj, proj, sinks.astype(F32).reshape(g, hpg, 1, 1), tile_gain(q_gain), tile_gain(k_gain),
      seg, *tabs, *tabs)


def swa_mixer(x2, b, s, gain, w_in, w_out, layer, q_norm, k_norm, sinks):
    d, g = SWA_HEAD_DIM, SWA_KV_HEADS
    n_in = (SWA_HEADS + 2 * g) * d
    (proj,) = norm_matmul(x2, gain, w_in, layer, n_in)
    o = swa_attention(proj.reshape(b, s, n_in), sinks, q_norm, k_norm, rope_tables(jnp.arange(s), d))
    return matmul_residual(o.reshape(b * s, SWA_HEADS * d), w_out, layer, x2)


def kernel(x, norm_mix, norm_mlp, mlp_w_up, mlp_w_down, nsa_w_in, nsa_w_out, nsa_q_norm, nsa_k_norm, nsa_cmp_pe, nsa_cmp_w1, nsa_cmp_w2, gla_w_in, gla_w_gate_up, gla_b_gate, gla_o_norm, gla_w_out, swa_w_in, swa_w_out, swa_q_norm, swa_k_norm, swa_sinks):
    b, s, d = x.shape
    x2 = x.reshape(b * s, d)
    mlp_w_up = mlp_w_up.astype(BF16)
    mlp_w_down = mlp_w_down.astype(BF16)
    ia = ib = ic = 0
    for i in range(norm_mix.shape[0]):
        kind = i % N_MIXERS
        if kind == 0:
            x2 = nsa_mixer(x2, b, s, norm_mix[i], nsa_w_in, nsa_w_out, ia, nsa_q_norm[ia], nsa_k_norm[ia],
                           nsa_cmp_pe[ia], nsa_cmp_w1[ia], nsa_cmp_w2[ia])
            ia += 1
        elif kind == 1:
            x2 = gla_mixer(x2, b, s, norm_mix[i], gla_w_in[ib], gla_w_gate_up[ib], gla_b_gate[ib],
                           gla_o_norm[ib], gla_w_out, ib)
            ib += 1
        else:
            x2 = swa_mixer(x2, b, s, norm_mix[i], swa_w_in, swa_w_out, ic, swa_q_norm[ic], swa_k_norm[ic],
                           swa_sinks[ic])
            ic += 1
        x2 = mlp_block(x2, norm_mlp[i], mlp_w_up, mlp_w_down, i)
    return x2.reshape(b, s, d)
```

```python
import functools

import numpy as np
import jax
import jax.numpy as jnp
from jax import lax
from jax.experimental import pallas as pl
from jax.experimental.pallas import tpu as pltpu

F32 = jnp.float32
BF16 = jnp.bfloat16

NORM_EPS = 1e-6
ROPE_THETA = 500000.0
ROPE_FRACTION = 4
NEG_INF = -1e30
N_MIXERS = 3

NSA_HEAD_DIM = 128
NSA_HEADS = 16
NSA_KV_GROUPS = 4
NSA_HPG = NSA_HEADS // NSA_KV_GROUPS
NSA_CMP_BLOCK = 32
NSA_CMP_STRIDE = 16
NSA_SEL_BLOCK = 64
NSA_TOP_N = 16
NSA_WINDOW = 512
NSA_FORCE_BONUS = 1e4
NSA_SCALE2 = float(NSA_HEAD_DIM ** -0.5 * np.log2(np.e))
NSA_MASK_NEG = -2.0 ** 100

GLA_HEADS = 4
GLA_KEY_DIM = 256
GLA_VAL_DIM = 512
GLA_GATE_RANK = 16
GLA_TAU = 16.0
GLA_CHUNK = 64
GLA_SUB = 16

SWA_HEAD_DIM = 64
SWA_HEADS = 32
SWA_KV_HEADS = 4
SWA_HPG = SWA_HEADS // SWA_KV_HEADS
SWA_WINDOW = 128

LANES = 128
VMEM_LIMIT = 56 * 1024 * 1024

NT_DIMS = (((1,), (1,)), ((), ()))
TN_DIMS = (((0,), (0,)), ((), ()))


def _params(sem):
    return pltpu.CompilerParams(dimension_semantics=sem, vmem_limit_bytes=VMEM_LIMIT)


def _dot(a, b, dims=None, precision=None):
    if dims is None:
        return jnp.dot(a, b, preferred_element_type=F32, precision=precision)
    return lax.dot_general(a, b, dims, preferred_element_type=F32, precision=precision)


def _rms_rows_to(h_scr, x_ref, g_ref, rows):
    n = x_ref.shape[0] // rows

    def body(i, c):
        r0 = pl.multiple_of(i * rows, rows)
        x = x_ref[pl.ds(r0, rows), :]
        ms = jnp.mean(x * x, axis=-1, keepdims=True)
        h_scr[pl.ds(r0, rows), :] = (x * lax.rsqrt(ms + NORM_EPS) * g_ref[...]).astype(BF16)
        return c

    lax.fori_loop(0, n, body, 0)


def _norm_matmul_kernel(x_ref, g_ref, w_ref, *rest, has_extra, n_head_tiles):
    rest = list(rest)
    wt_ref = rest.pop(0) if n_head_tiles is not None else None
    if has_extra:
        wx_ref, o_ref, ox_ref, h_scr = rest
    else:
        o_ref, h_scr = rest
    j = pl.program_id(1)

    @pl.when(j == 0)
    def _():
        _rms_rows_to(h_scr, x_ref, g_ref, 128)
        if has_extra:
            ox_ref[...] = _dot(h_scr[...], wx_ref[...].astype(BF16))

    if wt_ref is None:
        o_ref[...] = _dot(h_scr[...], w_ref[...].astype(BF16))
    else:
        @pl.when(j < n_head_tiles)
        def _():
            o_ref[...] = _dot(h_scr[...], w_ref[...].astype(BF16))

        @pl.when(j >= n_head_tiles)
        def _():
            o_ref[...] = _dot(h_scr[...], wt_ref[...].astype(BF16))


def norm_matmul(x, gain, w, layer, n_cols, w_extra=None, w_tail=None, tm=2048, tn=512):
    m, d = x.shape
    tm = min(tm, m)
    has_extra = w_extra is not None
    n_head = n_cols // tn
    n_tail = 0 if w_tail is None else w_tail.shape[1] // tn
    in_specs = [pl.BlockSpec((tm, d), lambda i, j: (i, 0), pipeline_mode=pl.Buffered(1)),
                pl.BlockSpec((1, d), lambda i, j: (0, 0)),
                pl.BlockSpec((None, d, tn), lambda i, j: (layer, 0, jnp.minimum(j, n_head - 1)))]
    out_shape = [jax.ShapeDtypeStruct((m, (n_head + n_tail) * tn), F32)]
    out_specs = [pl.BlockSpec((tm, tn), lambda i, j: (i, j))]
    args = [x, gain.reshape(1, d), w]
    if w_tail is not None:
        in_specs.append(pl.BlockSpec((d, tn), lambda i, j: (0, jnp.maximum(j - n_head, 0))))
        args.append(w_tail)
    if has_extra:
        in_specs.append(pl.BlockSpec((d, LANES), lambda i, j: (0, 0)))
        out_shape.append(jax.ShapeDtypeStruct((m, LANES), F32))
        out_specs.append(pl.BlockSpec((tm, LANES), lambda i, j: (i, 0)))
        args.append(w_extra)
    return pl.pallas_call(
        functools.partial(_norm_matmul_kernel, has_extra=has_extra,
                          n_head_tiles=None if w_tail is None else n_head),
        grid=(m // tm, n_head + n_tail),
        in_specs=in_specs, out_specs=out_specs, out_shape=out_shape,
        scratch_shapes=[pltpu.VMEM((tm, d), BF16)],
        compiler_params=_params(("parallel", "arbitrary")),
        name="norm_matmul",
    )(*args)


def _matmul_residual_kernel(a_ref, w_ref, r_ref, o_ref):
    o_ref[...] = r_ref[...] + _dot(a_ref[...], w_ref[...].astype(BF16))


def matmul_residual(a, w, layer, res, tm=2048, tn=512):
    m, k = a.shape
    tm = min(tm, m)
    n = w.shape[2]
    return pl.pallas_call(
        _matmul_residual_kernel,
        grid=(m // tm, n // tn),
        in_specs=[pl.BlockSpec((tm, k), lambda i, j: (i, 0)),
                  pl.BlockSpec((None, k, tn), lambda i, j: (layer, 0, j)),
                  pl.BlockSpec((tm, tn), lambda i, j: (i, j))],
        out_specs=pl.BlockSpec((tm, tn), lambda i, j: (i, j)),
        out_shape=jax.ShapeDtypeStruct((m, n), F32),
        compiler_params=_params(("parallel", "parallel")),
        name="matmul_residual",
    )(a, w, res)


def _mlp_kernel(x_ref, g_ref, wu_ref, xt_ref, wd_ref, o_ref, h_scr, u_scr, *, nf, tf, splits):
    j = pl.program_id(1)
    d = h_scr.shape[1]
    per = nf // splits

    @pl.when(j == 0)
    def _():
        _rms_rows_to(h_scr, x_ref, g_ref, 128)

    @pl.when(j < nf)
    def _():
        u = _dot(h_scr[:, 0:tf], wu_ref[0:tf, :].astype(BF16))
        for c in range(1, d // tf):
            u = u + _dot(h_scr[:, c * tf:(c + 1) * tf], wu_ref[c * tf:(c + 1) * tf, :].astype(BF16))
        u = jnp.maximum(u, 0.0)
        u_scr[j] = (u * u).astype(BF16)

    for part in range(splits):
        @pl.when((j >= nf) & ((j - nf) % splits == part))
        def _(part=part):
            acc = xt_ref[...] if part == 0 else o_ref[...]
            for f in range(per):
                acc = acc + _dot(u_scr[part * per + f], wd_ref[f * tf:(f + 1) * tf, :].astype(BF16))
            o_ref[...] = acc


def mlp_block(x, gain, w_up, w_down, layer, tm=1024, tf=512, tn=256, splits=2):
    m, d = x.shape
    ff = w_up.shape[2]
    nf, nn = ff // tf, d // tn
    down = lambda j: jnp.maximum(j - nf, 0)
    return pl.pallas_call(
        functools.partial(_mlp_kernel, nf=nf, tf=tf, splits=splits),
        grid=(m // tm, nf + nn * splits),
        in_specs=[pl.BlockSpec((tm, d), lambda i, j: (i, 0), pipeline_mode=pl.Buffered(1)),
                  pl.BlockSpec((1, d), lambda i, j: (0, 0)),
                  pl.BlockSpec((None, d, tf), lambda i, j: (layer, 0, jnp.minimum(j, nf - 1))),
                  pl.BlockSpec((tm, tn), lambda i, j: (i, down(j) // splits)),
                  pl.BlockSpec((None, ff // splits, tn), lambda i, j: (layer, down(j) % splits, down(j) // splits))],
        out_specs=pl.BlockSpec((tm, tn), lambda i, j: (i, down(j) // splits)),
        out_shape=jax.ShapeDtypeStruct((m, d), F32),
        scratch_shapes=[pltpu.VMEM((tm, d), BF16), pltpu.VMEM((nf, tm, tf), BF16)],
        compiler_params=_params(("parallel", "arbitrary")),
        name="mlp_block",
    )(x, gain.reshape(1, d), w_up, x, w_down)


def rope_tables(pos, head_dim):
    rot = head_dim // ROPE_FRACTION
    half = rot // 2
    inv_freq = jnp.power(jnp.float32(ROPE_THETA), -jnp.arange(half, dtype=F32) / half)
    ang = pos.astype(F32)[:, None] * inv_freq[None, :]
    cos, sin = jnp.cos(ang), jnp.sin(ang)
    n = pos.shape[0]
    zeros = jnp.zeros((n, head_dim - rot), F32)
    c = jnp.concatenate([cos, cos, jnp.ones((n, head_dim - rot), F32)], axis=-1)
    s1 = jnp.concatenate([-sin, jnp.zeros((n, half), F32), zeros], axis=-1)
    s2 = jnp.concatenate([jnp.zeros((n, half), F32), sin, zeros], axis=-1)
    reps = LANES // head_dim
    return tuple(jnp.tile(t, (1, reps)) for t in (c, s1, s2))


def _norm_rope(x, gain, c, s1, s2, half, ms=None):
    if ms is None:
        ms = jnp.mean(x * x, axis=-1, keepdims=True)
    y = x * lax.rsqrt(ms + NORM_EPS) * gain
    return y * c + pltpu.roll(y, LANES - half, 1) * s1 + pltpu.roll(y, half, 1) * s2


def _gelu_tanh(x):
    return 0.5 * x * (1.0 + jnp.tanh(np.sqrt(2.0 / np.pi) * (x + 0.044715 * (x * x * x))))


def _nsa_compress_kernel(x_ref, pe_ref, w1_ref, w2_ref, g_ref, c_ref, s1_ref, s2_ref, o_ref):
    n_rows = x_ref.shape[0] // NSA_CMP_STRIDE
    d = NSA_HEAD_DIM
    y0 = jnp.zeros((n_rows, d), F32)
    y1 = jnp.zeros((n_rows, d), F32)
    for r in range(NSA_CMP_STRIDE):
        xr = x_ref[pl.ds(r, n_rows, stride=NSA_CMP_STRIDE), :]
        a0 = (xr + pe_ref[r:r + 1, :]).astype(BF16)
        a1 = (xr + pe_ref[NSA_CMP_STRIDE + r:NSA_CMP_STRIDE + r + 1, :]).astype(BF16)
        y0 = y0 + _dot(a0, w1_ref[r * d:(r + 1) * d, :].astype(BF16))
        y1 = y1 + _dot(a1, w1_ref[(NSA_CMP_STRIDE + r) * d:(NSA_CMP_STRIDE + r + 1) * d, :].astype(BF16))
    pre = y0 + pltpu.roll(y1, n_rows - 1, 0)
    out = _dot(_gelu_tanh(pre).astype(BF16), w2_ref[...].astype(BF16))
    roped = _norm_rope(out, g_ref[...], c_ref[...], s1_ref[...], s2_ref[...],
                       NSA_HEAD_DIM // ROPE_FRACTION // 2, None)
    o_ref[...] = jnp.where(pl.program_id(0) == 0, roped * NSA_SCALE2, out).astype(BF16)


def nsa_compress(proj, pe, w1, w2, k_gain, tables):
    b, s, _ = proj.shape
    g, d = NSA_KV_GROUPS, NSA_HEAD_DIM
    n_rows = s // NSA_CMP_STRIDE
    col_blk = NSA_HEADS
    tab_spec = pl.BlockSpec((n_rows, d), lambda kv, bi, gi: (0, 0))
    return pl.pallas_call(
        _nsa_compress_kernel,
        grid=(2, b, g),
        in_specs=[pl.BlockSpec((None, s, d), lambda kv, bi, gi: (bi, 0, col_blk + kv * g + gi)),
                  pl.BlockSpec((None, NSA_CMP_BLOCK, d), lambda kv, bi, gi: (kv, 0, 0)),
                  pl.BlockSpec((None, NSA_CMP_BLOCK * d, d), lambda kv, bi, gi: (kv, 0, 0)),
                  pl.BlockSpec((None, d, d), lambda kv, bi, gi: (kv, 0, 0)),
                  pl.BlockSpec((1, d), lambda kv, bi, gi: (0, 0)),
                  tab_spec, tab_spec, tab_spec],
        out_specs=pl.BlockSpec((None, None, None, n_rows, d), lambda kv, bi, gi: (kv, bi, gi, 0, 0)),
        out_shape=jax.ShapeDtypeStruct((2, b, g, n_rows, d), BF16),
        compiler_params=_params(("parallel", "parallel", "parallel")),
        name="nsa_compress",
    )(proj, pe, w1, w2, k_gain.reshape(1, d), *tables)


def _nsa_attn_kernel(q_ref, kc_ref, vc_ref, ksr_ref, vsr_ref, kwr_ref, vwr_ref, gate_ref, ovt_ref,
                     qg_ref, kg_ref, c_ref, s1_ref, s2_ref, o_ref, ks_ref, vs_ref, kw_ref, vw_ref,
                     *, tq, tk, seq):
    gi = pl.program_id(1)
    qi = pl.program_id(2)
    d = NSA_HEAD_DIM
    hpg = NSA_HPG
    n_sel = seq // NSA_SEL_BLOCK
    n_top = min(NSA_TOP_N, n_sel)
    half = d // ROPE_FRACTION // 2

    def tables(rows):
        return c_ref[rows, :], s1_ref[rows, :], s2_ref[rows, :]

    scale2 = NSA_SCALE2

    @pl.when(qi == 0)
    def _():
        chunk = 512
        lane = lax.broadcasted_iota(jnp.int32, (chunk, d), 1)
        row = lax.broadcasted_iota(jnp.int32, (chunk, d), 0)

        def body(i, carry):
            r0 = pl.multiple_of(i * chunk, chunk)
            rows = pl.ds(r0, chunk)
            ks = _norm_rope(ksr_ref[rows, :], kg_ref[0:1, :], *tables(rows), half, None) * scale2
            kw = _norm_rope(kwr_ref[rows, :], kg_ref[1:2, :], *tables(rows), half, None) * scale2
            ks_ref[rows, 0:d] = ks.astype(BF16)
            ks_ref[rows, d:2 * d] = jnp.where((r0 + row) // NSA_SEL_BLOCK == lane, NSA_MASK_NEG, 0.0).astype(BF16)
            kw_ref[rows, :] = kw.astype(BF16)
            vs_ref[rows, :] = vsr_ref[rows, :].astype(BF16)
            vw_ref[rows, :] = vwr_ref[rows, :].astype(BF16)
            return carry

        lax.fori_loop(0, seq // chunk, body, 0)

    q_tabs = tables(pl.ds(pl.multiple_of(qi * tq, tq), tq))
    q4 = jnp.concatenate(
        [_norm_rope(q_ref[:, h * d:(h + 1) * d], qg_ref[...], *q_tabs, half, None).astype(BF16) for h in range(hpg)],
        axis=0)
    pos = qi * tq + lax.broadcasted_iota(jnp.int32, (tq, 1), 0)
    pos_l = qi * tq + lax.broadcasted_iota(jnp.int32, (1, tq), 1)

    def masked_scores(qs, k, bias=None):
        sc = _dot(qs, k, NT_DIMS)
        sc = sc.reshape(-1, tq, sc.shape[-1])
        return sc if bias is None else sc + bias[None]

    def stack(t):
        return t.reshape(t.shape[0] * tq, -1)

    span = NSA_WINDOW + tq
    w0 = pl.multiple_of(jnp.clip(qi * tq - NSA_WINDOW, 0, seq - span), tq)
    vw = vw_ref[pl.ds(w0, span), :]
    kp = w0 + lax.broadcasted_iota(jnp.int32, (1, span), 1)
    sw = masked_scores(q4, kw_ref[pl.ds(w0, span), :],
                       jnp.where((kp <= pos) & (kp > pos - NSA_WINDOW), 0.0, NEG_INF))
    pw = jnp.exp2(sw - jnp.max(sw, axis=-1, keepdims=True))
    o_win = _dot(stack(pw).astype(BF16), vw) / stack(jnp.sum(pw, axis=-1, keepdims=True))

    n_c = kc_ref.shape[0]
    cend = lax.broadcasted_iota(jnp.int32, (1, n_c), 1) * NSA_CMP_STRIDE + (NSA_CMP_BLOCK - 1)
    s = masked_scores(q4, kc_ref[...], jnp.where(cend <= pos, 0.0, NEG_INF))
    p = jnp.exp2(s - jnp.max(s, axis=-1, keepdims=True))
    p = p / jnp.sum(p, axis=-1, keepdims=True)
    p = jnp.where((pos >= NSA_CMP_BLOCK - 1)[None], p, 0.0)
    o_cmp = _dot(stack(p).astype(BF16), vc_ref[...])
    p_sum = jnp.sum(p, axis=0)
    imp_t = _dot(ovt_ref[...], p_sum, NT_DIMS, precision=lax.Precision.HIGHEST)

    j = lax.broadcasted_iota(jnp.int32, (n_sel, 1), 0)
    bq = pos_l // NSA_SEL_BLOCK
    forced = (j == 0) | (j == bq) | (j == bq - 1)
    score = jnp.where(j <= bq, imp_t + jnp.where(forced, NSA_FORCE_BONUS, 0.0), NEG_INF)
    sub = 8
    groups = [score[v * sub:(v + 1) * sub] for v in range(n_sel // sub)]
    ranks = [jnp.zeros((sub, tq), F32) for _ in groups]
    j_loc = lax.broadcasted_iota(jnp.int32, (sub, tq), 0)
    for jp in range(n_sel):
        row = score[jp:jp + 1, :]
        for v, sg in enumerate(groups):
            if v * sub > jp:
                beats = row >= sg
            elif (v + 1) * sub - 1 <= jp:
                beats = row > sg
            else:
                beats = (row > sg) | ((row == sg) & (j_loc > jp - v * sub))
            ranks[v] = ranks[v] + jnp.where(beats, 1.0, 0.0)
    rank = jnp.concatenate(ranks, axis=0)
    not_sel = jnp.where(rank < n_top, 0.0, 1.0)
    not_sel = jnp.concatenate([not_sel, jnp.zeros((LANES - n_sel, tq), F32)], axis=0).T.astype(BF16)

    q_aug = jnp.concatenate([q4, jnp.concatenate([not_sel] * hpg, axis=0)], axis=1)

    def sel_tile(kt, carry, bias):
        m, l, acc = carry
        k0 = pl.multiple_of(kt * tk, tk)
        sc = masked_scores(q_aug, ks_ref[pl.ds(k0, tk), :], bias)
        m_new = jnp.maximum(m, jnp.max(sc, axis=-1, keepdims=True))
        alpha = jnp.exp2(m - m_new)
        pe = jnp.exp2(sc - m_new)
        l = alpha * l + jnp.sum(pe, axis=-1, keepdims=True)
        acc = stack(alpha) * acc + _dot(stack(pe).astype(BF16), vs_ref[pl.ds(k0, tk), :])
        return m_new, l, acc

    last = (qi * tq + tq - 1) // tk
    init = (jnp.full((hpg, tq, 1), NEG_INF, F32), jnp.zeros((hpg, tq, 1), F32), jnp.zeros((hpg * tq, d), F32))
    carry = lax.fori_loop(0, last, lambda kt, c: sel_tile(kt, c, None), init)
    key_pos = last * tk + lax.broadcasted_iota(jnp.int32, (1, tk), 1)
    _, l_sel, acc_sel = sel_tile(last, carry, jnp.where(key_pos <= pos, 0.0, NEG_INF))
    o_sel = acc_sel / stack(l_sel)

    gt = jax.nn.sigmoid(gate_ref[...])
    lane = lax.broadcasted_iota(jnp.int32, (1, LANES), 1)
    for h in range(hpg):
        rows = slice(h * tq, (h + 1) * tq)
        o_h = jnp.zeros((tq, d), F32)
        for br, o_br in enumerate((o_cmp, o_sel, o_win)):
            g_col = jnp.sum(jnp.where(lane == gi * (3 * hpg) + h * 3 + br, gt, 0.0), axis=-1, keepdims=True)
            o_h = o_h + g_col * o_br[rows]
        o_ref[:, h * d:(h + 1) * d] = o_h.astype(BF16)


def selection_overlap(n_rows, n_sel):
    n_cmp = n_rows - 1
    c0 = np.arange(n_rows) * NSA_CMP_STRIDE
    s0 = np.arange(n_sel) * NSA_SEL_BLOCK
    ov = np.minimum(c0[:, None] + NSA_CMP_BLOCK, s0[None, :] + NSA_SEL_BLOCK) - np.maximum(c0[:, None], s0[None, :])
    ov = np.clip(ov, 0, None) / NSA_CMP_BLOCK
    ov[n_cmp:] = 0.0
    return jnp.asarray(ov, dtype=F32)


def nsa_attention(proj, kv_cmp, gate, q_gain, k_gains, tabs, tq=256, tk=512):
    b, s, _ = proj.shape
    g, d, hpg = NSA_KV_GROUPS, NSA_HEAD_DIM, NSA_HPG
    n_rows = s // NSA_CMP_STRIDE
    n_sel = s // NSA_SEL_BLOCK
    seg_blk = lambda i: NSA_HEADS + i * g
    cmp_spec = lambda kv: pl.BlockSpec((None, None, None, n_rows, d), lambda bi, gi, qi: (kv, bi, gi, 0, 0))
    seq_spec = lambda i: pl.BlockSpec((None, s, d), lambda bi, gi, qi: (bi, 0, seg_blk(i) + gi))
    const = lambda shape: pl.BlockSpec(shape, lambda bi, gi, qi: (0, 0))
    return pl.pallas_call(
        functools.partial(_nsa_attn_kernel, tq=tq, tk=tk, seq=s),
        grid=(b, g, s // tq),
        in_specs=[pl.BlockSpec((None, tq, hpg * d), lambda bi, gi, qi: (bi, qi, gi)),
                  cmp_spec(0), cmp_spec(1),
                  seq_spec(2), seq_spec(3), seq_spec(4), seq_spec(5),
                  pl.BlockSpec((None, tq, LANES), lambda bi, gi, qi: (bi, qi, 0)),
                  const((n_sel, n_rows)), const((1, d)), const((2, d)),
                  const((s, d)), const((s, d)), const((s, d))],
        out_specs=pl.BlockSpec((None, tq, hpg * d), lambda bi, gi, qi: (bi, qi, gi)),
        out_shape=jax.ShapeDtypeStruct((b, s, NSA_HEADS * d), BF16),
        scratch_shapes=[pltpu.VMEM((s, 2 * d), BF16)] + [pltpu.VMEM((s, d), BF16)] * 3,
        compiler_params=_params(("parallel", "parallel", "arbitrary")),
        name="nsa_attention",
    )(proj, kv_cmp, kv_cmp, proj, proj, proj, proj, gate, selection_overlap(n_rows, n_sel).T,
      q_gain.reshape(1, d), k_gains, *tabs)


def nsa_mixer(x2, b, s, gain, w_in, w_out, layer, q_norm, k_norm, cmp_pe, cmp_w1, cmp_w2):
    d, g = NSA_HEAD_DIM, NSA_KV_GROUPS
    n_main = (NSA_HEADS + 6 * g) * d
    w_gate = jnp.pad(w_in[layer, :, n_main:], ((0, 0), (0, LANES - 3 * NSA_HEADS)))
    proj, gate = norm_matmul(x2, gain, w_in, layer, n_main, w_gate)
    proj = proj.reshape(b, s, n_main)
    gate = gate.reshape(b, s, LANES)
    tabs = rope_tables(jnp.arange(s), d)
    cmp_end = jnp.arange(s // NSA_CMP_STRIDE) * NSA_CMP_STRIDE + (NSA_CMP_BLOCK - 1)
    kv_cmp = nsa_compress(proj, cmp_pe, cmp_w1, cmp_w2, k_norm[0], rope_tables(cmp_end, d))
    o = nsa_attention(proj, kv_cmp, gate, q_norm, k_norm[1:3], tabs)
    return matmul_residual(o.reshape(b * s, NSA_HEADS * d), w_out, layer, x2)


def _log_sigmoid(z):
    return jnp.minimum(z, 0.0) - jnp.log1p(jnp.exp(-jnp.abs(z)))


def _gla_kernel(q_ref, k_ref, v_ref, r_ref, glr_ref, wg_ref, bg_ref, on_ref, o_ref, state_ref, *, rows, hps):
    c = GLA_CHUNK
    sub = GLA_SUB
    dk, dv = GLA_KEY_DIM, GLA_VAL_DIM

    @pl.when(pl.program_id(2) == 0)
    def _():
        state_ref[...] = jnp.zeros_like(state_ref)

    tri = jnp.where(lax.broadcasted_iota(jnp.int32, (c, c), 0) >= lax.broadcasted_iota(jnp.int32, (c, c), 1), 1.0, 0.0)
    sub_row = lax.broadcasted_iota(jnp.int32, (sub, 1), 0)

    def head_chunk(hh, r0):
        kcols = slice(hh * dk, (hh + 1) * dk)
        vcols = slice(hh * dv, (hh + 1) * dv)
        qc = q_ref[pl.ds(r0, c), kcols] * (dk ** -0.5)
        kc = k_ref[pl.ds(r0, c), kcols]
        vc = v_ref[pl.ds(r0, c), vcols]
        z = _dot(glr_ref[pl.ds(r0, c), :], wg_ref[:, kcols], precision=lax.Precision.HIGHEST) + bg_ref[:, kcols]
        log_a = _log_sigmoid(z) / GLA_TAU
        cum = _dot(tri, log_a, precision=lax.Precision.HIGHEST)
        vb = vc.astype(BF16)

        state = state_ref[hh]
        o_inter = _dot((qc * jnp.exp(cum)).astype(BF16), state.astype(BF16), NT_DIMS)

        parts = []
        for bi in range(c // sub):
            lo = bi * sub
            q_i, k_i, v_i, c_i = qc[lo:lo + sub], kc[lo:lo + sub], vc[lo:lo + sub], cum[lo:lo + sub]
            o_i = o_inter[lo:lo + sub]
            if bi > 0:
                c0 = c_i[0:1]
                qs = (q_i * jnp.exp(c_i - c0)).astype(BF16)
                kp = (kc[:lo] * jnp.exp(c0 - cum[:lo])).astype(BF16)
                att = _dot(qs, kp, NT_DIMS)
                o_i = o_i + _dot(att.astype(BF16), vb[:lo])
            for jj in range(sub):
                e = jnp.exp(jnp.where(sub_row >= jj, c_i - c_i[jj:jj + 1], NEG_INF))
                col = jnp.sum(q_i * k_i[jj:jj + 1] * e, axis=-1, keepdims=True)
                o_i = o_i + col * v_i[jj:jj + 1]
            parts.append(o_i)
        o = jnp.concatenate(parts, axis=0)

        last = cum[c - 1:c]
        kd = (kc * jnp.exp(last - cum)).astype(BF16)
        state_ref[hh] = state * jnp.exp(last) + _dot(vb, kd, TN_DIMS)

        ms = jnp.mean(o * o, axis=-1, keepdims=True)
        on = o * lax.rsqrt(ms + NORM_EPS) * on_ref[...]
        r = r_ref[pl.ds(r0, c), vcols]
        o_ref[pl.ds(r0, c), vcols] = (on * (r * jax.nn.sigmoid(r))).astype(BF16)

    def chunk(ci, carry):
        r0 = pl.multiple_of(ci * c, c)
        for hh in range(hps):
            head_chunk(hh, r0)
        return carry

    lax.fori_loop(0, rows // c, chunk, 0)


def gla_attention(proj, glr, w_gate_up, b_gate, o_norm, rows=512, hps=2):
    b, s, _ = proj.shape
    nh, dk, dv = GLA_HEADS, GLA_KEY_DIM, GLA_VAL_DIM
    wg = jnp.pad(w_gate_up, ((0, LANES - GLA_GATE_RANK), (0, 0)))
    kw, vw = hps * dk, hps * dv
    k_blk = nh * dk // kw
    v_blk = 2 * nh * dk // vw
    r_blk = (2 * nh * dk + nh * dv) // vw
    return pl.pallas_call(
        functools.partial(_gla_kernel, rows=rows, hps=hps),
        grid=(b, nh // hps, s // rows),
        in_specs=[pl.BlockSpec((None, rows, kw), lambda bi, h, ci: (bi, ci, h)),
                  pl.BlockSpec((None, rows, kw), lambda bi, h, ci: (bi, ci, k_blk + h)),
                  pl.BlockSpec((None, rows, vw), lambda bi, h, ci: (bi, ci, v_blk + h)),
                  pl.BlockSpec((None, rows, vw), lambda bi, h, ci: (bi, ci, r_blk + h)),
                  pl.BlockSpec((None, rows, LANES), lambda bi, h, ci: (bi, ci, 0)),
                  pl.BlockSpec((LANES, kw), lambda bi, h, ci: (0, h)),
                  pl.BlockSpec((1, kw), lambda bi, h, ci: (0, h)),
                  pl.BlockSpec((1, dv), lambda bi, h, ci: (0, 0))],
        out_specs=pl.BlockSpec((None, rows, vw), lambda bi, h, ci: (bi, ci, h)),
        out_shape=jax.ShapeDtypeStruct((b, s, nh * dv), BF16),
        scratch_shapes=[pltpu.VMEM((hps, dv, dk), F32)],
        compiler_params=_params(("parallel", "parallel", "arbitrary")),
        name="gla_attention",
    )(proj, proj, proj, proj, glr, wg, b_gate.reshape(1, nh * dk), o_norm.reshape(1, dv))


def gla_mixer(x2, b, s, gain, w_in, w_gate_up, b_gate, o_norm, w_out, layer):
    nh, dk, dv = GLA_HEADS, GLA_KEY_DIM, GLA_VAL_DIM
    n_qkv = 2 * nh * dk + nh * dv
    w_r = w_in[layer, :, n_qkv + GLA_GATE_RANK:]
    w_glr = jnp.pad(w_in[layer, :, n_qkv:n_qkv + GLA_GATE_RANK], ((0, 0), (0, LANES - GLA_GATE_RANK)))
    proj, glr = norm_matmul(x2, gain, w_in, layer, n_qkv, w_glr, w_tail=w_r)
    o = gla_attention(proj.reshape(b, s, -1), glr.reshape(b, s, LANES), w_gate_up, b_gate, o_norm)
    return matmul_residual(o.reshape(b * s, nh * dv), w_out, layer, x2)


def _swa_heads(x, gain, tabs, seg, out_scale):
    blk = x.shape[0]
    d = SWA_HEAD_DIM
    n = x.shape[1] // LANES
    tiles = jnp.concatenate([x[:, t * LANES:(t + 1) * LANES] for t in range(n)], axis=0)
    sq = tiles * tiles
    hi = sq.astype(BF16)
    lo = (sq - hi.astype(F32)).astype(BF16)
    ms = (_dot(hi, seg) + _dot(lo, seg)) * (1.0 / d)
    heads = []
    for t in range(n):
        rows = slice(t * blk, (t + 1) * blk)
        y = _norm_rope(tiles[rows], gain, *tabs, d // ROPE_FRACTION // 2, ms[rows]) * out_scale
        heads += [y[:, u * d:(u + 1) * d] for u in range(LANES // d)]
    return heads


def _swa_kernel(q_ref, kp_ref, kc_ref, vp_ref, vc_ref, sink_ref, qg_ref, kg_ref, seg_ref,
                cc_ref, s1c_ref, s2c_ref, cp_ref, s1p_ref, s2p_ref, o_ref, *, blk):
    qi = pl.program_id(1)
    hpg, d, g = SWA_HPG, SWA_HEAD_DIM, SWA_KV_HEADS
    seg = seg_ref[...]
    tabs_c = (cc_ref[...], s1c_ref[...], s2c_ref[...])
    tabs_p = (cp_ref[...], s1p_ref[...], s2p_ref[...])
    q_heads = _swa_heads(q_ref[...], qg_ref[...], tabs_c, seg, d ** -0.5)
    kp_heads = _swa_heads(kp_ref[...], kg_ref[...], tabs_p, seg, 1.0)
    kc_heads = _swa_heads(kc_ref[...], kg_ref[...], tabs_c, seg, 1.0)
    vp, vc = vp_ref[...], vc_ref[...]

    qp = qi * blk + lax.broadcasted_iota(jnp.int32, (blk, 1), 0)
    kpos = (qi - 1) * blk + lax.broadcasted_iota(jnp.int32, (1, 2 * blk), 1)
    bias = jnp.where((kpos <= qp) & (kpos > qp - SWA_WINDOW) & (kpos >= 0), 0.0, NEG_INF)

    for gi in range(g):
        q = jnp.concatenate(q_heads[gi * hpg:(gi + 1) * hpg], axis=0).astype(BF16)
        k = jnp.concatenate([kp_heads[gi], kc_heads[gi]], axis=0).astype(BF16)
        v = jnp.concatenate([vp[:, gi * d:(gi + 1) * d], vc[:, gi * d:(gi + 1) * d]], axis=0).astype(BF16)
        s = _dot(q, k, NT_DIMS).reshape(hpg, blk, 2 * blk) + bias[None]
        sink = sink_ref[gi]
        m = jnp.maximum(jnp.max(s, axis=-1, keepdims=True), sink)
        p = jnp.exp(s - m)
        denom = jnp.sum(p, axis=-1, keepdims=True) + jnp.exp(sink - m)
        o = _dot(p.reshape(hpg * blk, 2 * blk).astype(BF16), v).reshape(hpg, blk, d) / denom
        o_ref[:, gi * hpg * d:(gi + 1) * hpg * d] = jnp.concatenate([o[h] for h in range(hpg)], axis=1).astype(BF16)


def swa_attention(proj, sinks, q_gain, k_gain, tabs, blk=128):
    b, s, _ = proj.shape
    g, hpg, d = SWA_KV_HEADS, SWA_HPG, SWA_HEAD_DIM
    nq, nkv = SWA_HEADS * d, g * d
    k_blk = nq // nkv
    prev = lambda qi: jnp.maximum(qi - 1, 0)
    kv_spec = lambda col, row: pl.BlockSpec((None, blk, nkv), lambda bi, qi: (bi, row(qi), col))
    const = lambda shape: pl.BlockSpec(shape, lambda bi, qi: (0,) * len(shape))
    tab_spec = lambda row: pl.BlockSpec((blk, LANES), lambda bi, qi: (row(qi), 0))
    cur = lambda qi: qi
    seg_id = np.arange(LANES) // d
    seg = jnp.asarray(seg_id[:, None] == seg_id[None, :], dtype=BF16)
    tile_gain = lambda gn: jnp.tile(gn.reshape(1, d), (1, LANES // d))
    return pl.pallas_call(
        functools.partial(_swa_kernel, blk=blk),
        grid=(b, s // blk),
        in_specs=[pl.BlockSpec((None, blk, nq), lambda bi, qi: (bi, qi, 0)),
                  kv_spec(k_blk, prev), kv_spec(k_blk, cur), kv_spec(k_blk + 1, prev), kv_spec(k_blk + 1, cur),
                  const((g, hpg, 1, 1)), const((1, LANES)), const((1, LANES)), const((LANES, LANES)),
                  tab_spec(cur), tab_spec(cur), tab_spec(cur), tab_spec(prev), tab_spec(prev), tab_spec(prev)],
        out_specs=pl.BlockSpec((None, blk, nq), lambda bi, qi: (bi, qi, 0)),
        out_shape=jax.ShapeDtypeStruct((b, s, nq), BF16),
        compiler_params=_params(("parallel", "parallel")),
        name="swa_attention",
    )(proj, proj, proj, proj, proj, sinks.astype(F32).reshape(g, hpg, 1, 1), tile_gain(q_gain), tile_gain(k_gain),
      seg, *tabs, *tabs)


def swa_mixer(x2, b, s, gain, w_in, w_out, layer, q_norm, k_norm, sinks):
    d, g = SWA_HEAD_DIM, SWA_KV_HEADS
    n_in = (SWA_HEADS + 2 * g) * d
    (proj,) = norm_matmul(x2, gain, w_in, layer, n_in)
    o = swa_attention(proj.reshape(b, s, n_in), sinks, q_norm, k_norm, rope_tables(jnp.arange(s), d))
    return matmul_residual(o.reshape(b * s, SWA_HEADS * d), w_out, layer, x2)


def kernel(x, norm_mix, norm_mlp, mlp_w_up, mlp_w_down, nsa_w_in, nsa_w_out, nsa_q_norm, nsa_k_norm, nsa_cmp_pe, nsa_cmp_w1, nsa_cmp_w2, gla_w_in, gla_w_gate_up, gla_b_gate, gla_o_norm, gla_w_out, swa_w_in, swa_w_out, swa_q_norm, swa_k_norm, swa_sinks):
    b, s, d = x.shape
    x2 = x.reshape(b * s, d)
    ia = ib = ic = 0
    for i in range(norm_mix.shape[0]):
        kind = i % N_MIXERS
        if kind == 0:
            x2 = nsa_mixer(x2, b, s, norm_mix[i], nsa_w_in, nsa_w_out, ia, nsa_q_norm[ia], nsa_k_norm[ia],
                           nsa_cmp_pe[ia], nsa_cmp_w1[ia], nsa_cmp_w2[ia])
            ia += 1
        elif kind == 1:
            x2 = gla_mixer(x2, b, s, norm_mix[i], gla_w_in, gla_w_gate_up[ib], gla_b_gate[ib],
                           gla_o_norm[ib], gla_w_out, ib)
            ib += 1
        else:
            x2 = swa_mixer(x2, b, s, norm_mix[i], swa_w_in, swa_w_out, ic, swa_q_norm[ic], swa_k_norm[ic],
                           swa_sinks[ic])
            ic += 1
        x2 = mlp_block(x2, norm_mlp[i], mlp_w_up, mlp_w_down, i)
    return x2.reshape(b, s, d)
```

```python
import functools

import numpy as np
import jax
import jax.numpy as jnp
from jax import lax
from jax.experimental import pallas as pl
from jax.experimental.pallas import tpu as pltpu

F32 = jnp.float32
BF16 = jnp.bfloat16

NORM_EPS = 1e-6
ROPE_THETA = 500000.0
ROPE_FRACTION = 4
NEG_INF = -1e30
N_MIXERS = 3

NSA_HEAD_DIM = 128
NSA_HEADS = 16
NSA_KV_GROUPS = 4
NSA_HPG = NSA_HEADS // NSA_KV_GROUPS
NSA_CMP_BLOCK = 32
NSA_CMP_STRIDE = 16
NSA_SEL_BLOCK = 64
NSA_TOP_N = 16
NSA_WINDOW = 512
NSA_FORCE_BONUS = 1e4
NSA_SCALE2 = float(NSA_HEAD_DIM ** -0.5 * np.log2(np.e))
NSA_MASK_NEG = -2.0 ** 100

GLA_HEADS = 4
GLA_KEY_DIM = 256
GLA_VAL_DIM = 512
GLA_GATE_RANK = 16
GLA_TAU = 16.0
GLA_CHUNK = 64
GLA_SUB = 16

SWA_HEAD_DIM = 64
SWA_HEADS = 32
SWA_KV_HEADS = 4
SWA_HPG = SWA_HEADS // SWA_KV_HEADS
SWA_WINDOW = 128

LANES = 128
VMEM_LIMIT = 56 * 1024 * 1024

NT_DIMS = (((1,), (1,)), ((), ()))
TN_DIMS = (((0,), (0,)), ((), ()))


def _params(sem):
    return pltpu.CompilerParams(dimension_semantics=sem, vmem_limit_bytes=VMEM_LIMIT)


def _dot(a, b, dims=None, precision=None):
    if dims is None:
        return jnp.dot(a, b, preferred_element_type=F32, precision=precision)
    return lax.dot_general(a, b, dims, preferred_element_type=F32, precision=precision)


def _rms_rows_to(h_scr, x_ref, g_ref, rows):
    n = x_ref.shape[0] // rows

    def body(i, c):
        r0 = pl.multiple_of(i * rows, rows)
        x = x_ref[pl.ds(r0, rows), :]
        ms = jnp.mean(x * x, axis=-1, keepdims=True)
        h_scr[pl.ds(r0, rows), :] = (x * lax.rsqrt(ms + NORM_EPS) * g_ref[...]).astype(BF16)
        return c

    lax.fori_loop(0, n, body, 0)


def _norm_matmul_kernel(x_ref, g_ref, w_ref, *rest, has_extra, n_head_tiles):
    rest = list(rest)
    wt_ref = rest.pop(0) if n_head_tiles is not None else None
    if has_extra:
        wx_ref, o_ref, ox_ref, h_scr = rest
    else:
        o_ref, h_scr = rest
    j = pl.program_id(1)

    @pl.when(j == 0)
    def _():
        _rms_rows_to(h_scr, x_ref, g_ref, 128)
        if has_extra:
            ox_ref[...] = _dot(h_scr[...], wx_ref[...].astype(BF16))

    if wt_ref is None:
        o_ref[...] = _dot(h_scr[...], w_ref[...].astype(BF16))
    else:
        @pl.when(j < n_head_tiles)
        def _():
            o_ref[...] = _dot(h_scr[...], w_ref[...].astype(BF16))

        @pl.when(j >= n_head_tiles)
        def _():
            o_ref[...] = _dot(h_scr[...], wt_ref[...].astype(BF16))


def norm_matmul(x, gain, w, layer, n_cols, w_extra=None, w_tail=None, tm=2048, tn=512):
    m, d = x.shape
    tm = min(tm, m)
    has_extra = w_extra is not None
    n_head = n_cols // tn
    n_tail = 0 if w_tail is None else w_tail.shape[1] // tn
    in_specs = [pl.BlockSpec((tm, d), lambda i, j: (i, 0), pipeline_mode=pl.Buffered(1)),
                pl.BlockSpec((1, d), lambda i, j: (0, 0)),
                pl.BlockSpec((None, d, tn), lambda i, j: (layer, 0, jnp.minimum(j, n_head - 1)))]
    out_shape = [jax.ShapeDtypeStruct((m, (n_head + n_tail) * tn), F32)]
    out_specs = [pl.BlockSpec((tm, tn), lambda i, j: (i, j))]
    args = [x, gain.reshape(1, d), w]
    if w_tail is not None:
        in_specs.append(pl.BlockSpec((d, tn), lambda i, j: (0, jnp.maximum(j - n_head, 0))))
        args.append(w_tail)
    if has_extra:
        in_specs.append(pl.BlockSpec((d, LANES), lambda i, j: (0, 0)))
        out_shape.append(jax.ShapeDtypeStruct((m, LANES), F32))
        out_specs.append(pl.BlockSpec((tm, LANES), lambda i, j: (i, 0)))
        args.append(w_extra)
    return pl.pallas_call(
        functools.partial(_norm_matmul_kernel, has_extra=has_extra,
                          n_head_tiles=None if w_tail is None else n_head),
        grid=(m // tm, n_head + n_tail),
        in_specs=in_specs, out_specs=out_specs, out_shape=out_shape,
        scratch_shapes=[pltpu.VMEM((tm, d), BF16)],
        compiler_params=_params(("parallel", "arbitrary")),
        name="norm_matmul",
    )(*args)


def _matmul_residual_kernel(a_ref, w_ref, r_ref, o_ref):
    o_ref[...] = r_ref[...] + _dot(a_ref[...], w_ref[...].astype(BF16))


def matmul_residual(a, w, layer, res, tm=2048, tn=512):
    m, k = a.shape
    tm = min(tm, m)
    n = w.shape[2]
    return pl.pallas_call(
        _matmul_residual_kernel,
        grid=(m // tm, n // tn),
        in_specs=[pl.BlockSpec((tm, k), lambda i, j: (i, 0)),
                  pl.BlockSpec((None, k, tn), lambda i, j: (layer, 0, j)),
                  pl.BlockSpec((tm, tn), lambda i, j: (i, j))],
        out_specs=pl.BlockSpec((tm, tn), lambda i, j: (i, j)),
        out_shape=jax.ShapeDtypeStruct((m, n), F32),
        compiler_params=_params(("parallel", "parallel")),
        name="matmul_residual",
    )(a, w, res)


def _mlp_kernel(x_ref, g_ref, wu_ref, xt_ref, wd_ref, o_ref, h_scr, u_scr, *, nf, tf, splits):
    j = pl.program_id(1)
    d = h_scr.shape[1]
    per = nf // splits

    @pl.when(j == 0)
    def _():
        _rms_rows_to(h_scr, x_ref, g_ref, 128)

    @pl.when(j < nf)
    def _():
        u = _dot(h_scr[:, 0:tf], wu_ref[0:tf, :].astype(BF16))
        for c in range(1, d // tf):
            u = u + _dot(h_scr[:, c * tf:(c + 1) * tf], wu_ref[c * tf:(c + 1) * tf, :].astype(BF16))
        u = jnp.maximum(u, 0.0)
        u_scr[j] = (u * u).astype(BF16)

    for part in range(splits):
        @pl.when((j >= nf) & ((j - nf) % splits == part))
        def _(part=part):
            acc = xt_ref[...] if part == 0 else o_ref[...]
            for f in range(per):
                acc = acc + _dot(u_scr[part * per + f], wd_ref[f * tf:(f + 1) * tf, :].astype(BF16))
            o_ref[...] = acc


def mlp_block(x, gain, w_up, w_down, layer, tm=1024, tf=512, tn=256, splits=2):
    m, d = x.shape
    ff = w_up.shape[2]
    nf, nn = ff // tf, d // tn
    down = lambda j: jnp.maximum(j - nf, 0)
    return pl.pallas_call(
        functools.partial(_mlp_kernel, nf=nf, tf=tf, splits=splits),
        grid=(m // tm, nf + nn * splits),
        in_specs=[pl.BlockSpec((tm, d), lambda i, j: (i, 0), pipeline_mode=pl.Buffered(1)),
                  pl.BlockSpec((1, d), lambda i, j: (0, 0)),
                  pl.BlockSpec((None, d, tf), lambda i, j: (layer, 0, jnp.minimum(j, nf - 1))),
                  pl.BlockSpec((tm, tn), lambda i, j: (i, down(j) // splits)),
                  pl.BlockSpec((None, ff // splits, tn), lambda i, j: (layer, down(j) % splits, down(j) // splits))],
        out_specs=pl.BlockSpec((tm, tn), lambda i, j: (i, down(j) // splits)),
        out_shape=jax.ShapeDtypeStruct((m, d), F32),
        scratch_shapes=[pltpu.VMEM((tm, d), BF16), pltpu.VMEM((nf, tm, tf), BF16)],
        compiler_params=_params(("parallel", "arbitrary")),
        name="mlp_block",
    )(x, gain.reshape(1, d), w_up, x, w_down)


def rope_tables(pos, head_dim):
    rot = head_dim // ROPE_FRACTION
    half = rot // 2
    inv_freq = jnp.power(jnp.float32(ROPE_THETA), -jnp.arange(half, dtype=F32) / half)
    ang = pos.astype(F32)[:, None] * inv_freq[None, :]
    cos, sin = jnp.cos(ang), jnp.sin(ang)
    n = pos.shape[0]
    zeros = jnp.zeros((n, head_dim - rot), F32)
    c = jnp.concatenate([cos, cos, jnp.ones((n, head_dim - rot), F32)], axis=-1)
    s1 = jnp.concatenate([-sin, jnp.zeros((n, half), F32), zeros], axis=-1)
    s2 = jnp.concatenate([jnp.zeros((n, half), F32), sin, zeros], axis=-1)
    reps = LANES // head_dim
    return tuple(jnp.tile(t, (1, reps)) for t in (c, s1, s2))


def _norm_rope(x, gain, c, s1, s2, half, ms=None):
    if ms is None:
        ms = jnp.mean(x * x, axis=-1, keepdims=True)
    y = x * lax.rsqrt(ms + NORM_EPS) * gain
    return y * c + pltpu.roll(y, LANES - half, 1) * s1 + pltpu.roll(y, half, 1) * s2


def _gelu_tanh(x):
    return 0.5 * x * (1.0 + jnp.tanh(np.sqrt(2.0 / np.pi) * (x + 0.044715 * (x * x * x))))


def _nsa_compress_kernel(x_ref, pe_ref, w1_ref, w2_ref, g_ref, c_ref, s1_ref, s2_ref, o_ref):
    n_rows = x_ref.shape[0] // NSA_CMP_STRIDE
    d = NSA_HEAD_DIM
    y0 = jnp.zeros((n_rows, d), F32)
    y1 = jnp.zeros((n_rows, d), F32)
    for r in range(NSA_CMP_STRIDE):
        xr = x_ref[pl.ds(r, n_rows, stride=NSA_CMP_STRIDE), :]
        a0 = (xr + pe_ref[r:r + 1, :]).astype(BF16)
        a1 = (xr + pe_ref[NSA_CMP_STRIDE + r:NSA_CMP_STRIDE + r + 1, :]).astype(BF16)
        y0 = y0 + _dot(a0, w1_ref[r * d:(r + 1) * d, :].astype(BF16))
        y1 = y1 + _dot(a1, w1_ref[(NSA_CMP_STRIDE + r) * d:(NSA_CMP_STRIDE + r + 1) * d, :].astype(BF16))
    pre = y0 + pltpu.roll(y1, n_rows - 1, 0)
    out = _dot(_gelu_tanh(pre).astype(BF16), w2_ref[...].astype(BF16))
    roped = _norm_rope(out, g_ref[...], c_ref[...], s1_ref[...], s2_ref[...],
                       NSA_HEAD_DIM // ROPE_FRACTION // 2, None)
    o_ref[...] = jnp.where(pl.program_id(0) == 0, roped * NSA_SCALE2, out).astype(BF16)


def nsa_compress(proj, pe, w1, w2, k_gain, tables):
    b, s, _ = proj.shape
    g, d = NSA_KV_GROUPS, NSA_HEAD_DIM
    n_rows = s // NSA_CMP_STRIDE
    col_blk = NSA_HEADS
    tab_spec = pl.BlockSpec((n_rows, d), lambda kv, bi, gi: (0, 0))
    return pl.pallas_call(
        _nsa_compress_kernel,
        grid=(2, b, g),
        in_specs=[pl.BlockSpec((None, s, d), lambda kv, bi, gi: (bi, 0, col_blk + kv * g + gi)),
                  pl.BlockSpec((None, NSA_CMP_BLOCK, d), lambda kv, bi, gi: (kv, 0, 0)),
                  pl.BlockSpec((None, NSA_CMP_BLOCK * d, d), lambda kv, bi, gi: (kv, 0, 0)),
                  pl.BlockSpec((None, d, d), lambda kv, bi, gi: (kv, 0, 0)),
                  pl.BlockSpec((1, d), lambda kv, bi, gi: (0, 0)),
                  tab_spec, tab_spec, tab_spec],
        out_specs=pl.BlockSpec((None, None, None, n_rows, d), lambda kv, bi, gi: (kv, bi, gi, 0, 0)),
        out_shape=jax.ShapeDtypeStruct((2, b, g, n_rows, d), BF16),
        compiler_params=_params(("parallel", "parallel", "parallel")),
        name="nsa_compress",
    )(proj, pe, w1, w2, k_gain.reshape(1, d), *tables)


def _nsa_attn_kernel(q_ref, kc_ref, vc_ref, ksr_ref, vsr_ref, kwr_ref, vwr_ref, gate_ref, ovt_ref,
                     qg_ref, kg_ref, c_ref, s1_ref, s2_ref, o_ref, ks_ref, vs_ref, kw_ref, vw_ref, q_scr, score_ref, rank_ref,
                     *, tq, tk, seq):
    gi = pl.program_id(1)
    qi = pl.program_id(2)
    d = NSA_HEAD_DIM
    hpg = NSA_HPG
    n_sel = seq // NSA_SEL_BLOCK
    n_top = min(NSA_TOP_N, n_sel)
    half = d // ROPE_FRACTION // 2

    def tables(rows):
        return c_ref[rows, :], s1_ref[rows, :], s2_ref[rows, :]

    scale2 = NSA_SCALE2

    @pl.when(qi == 0)
    def _():
        chunk = 512
        lane = lax.broadcasted_iota(jnp.int32, (chunk, d), 1)
        row = lax.broadcasted_iota(jnp.int32, (chunk, d), 0)

        def body(i, carry):
            r0 = pl.multiple_of(i * chunk, chunk)
            rows = pl.ds(r0, chunk)
            ks = _norm_rope(ksr_ref[rows, :], kg_ref[0:1, :], *tables(rows), half, None) * scale2
            kw = _norm_rope(kwr_ref[rows, :], kg_ref[1:2, :], *tables(rows), half, None) * scale2
            ks_ref[rows, 0:d] = ks.astype(BF16)
            ks_ref[rows, d:2 * d] = jnp.where((r0 + row) // NSA_SEL_BLOCK == lane, NSA_MASK_NEG, 0.0).astype(BF16)
            kw_ref[rows, :] = kw.astype(BF16)
            vs_ref[rows, :] = vsr_ref[rows, :].astype(BF16)
            vw_ref[rows, :] = vwr_ref[rows, :].astype(BF16)
            return carry

        lax.fori_loop(0, seq // chunk, body, 0)

    q_rows = 128

    def q_body(i, carry):
        r0 = pl.multiple_of(i * q_rows, q_rows)
        tabs = tables(pl.ds(pl.multiple_of(qi * tq, tq) + r0, q_rows))
        for h in range(hpg):
            y = _norm_rope(q_ref[pl.ds(r0, q_rows), h * d:(h + 1) * d], qg_ref[...], *tabs, half, None)
            q_scr[pl.ds(h * tq + r0, q_rows), :] = y.astype(BF16)
        return carry

    lax.fori_loop(0, tq // q_rows, q_body, 0)
    q4 = q_scr[...]
    pos = qi * tq + lax.broadcasted_iota(jnp.int32, (tq, 1), 0)
    pos_l = qi * tq + lax.broadcasted_iota(jnp.int32, (1, tq), 1)

    def masked_scores(qs, k, bias=None):
        sc = _dot(qs, k, NT_DIMS)
        sc = sc.reshape(-1, tq, sc.shape[-1])
        return sc if bias is None else sc + bias[None]

    def stack(t):
        return t.reshape(t.shape[0] * tq, -1)

    span = NSA_WINDOW + tq
    w0 = pl.multiple_of(jnp.clip(qi * tq - NSA_WINDOW, 0, seq - span), tq)
    vw = vw_ref[pl.ds(w0, span), :]
    kp = w0 + lax.broadcasted_iota(jnp.int32, (1, span), 1)
    sw = masked_scores(q4, kw_ref[pl.ds(w0, span), :],
                       jnp.where((kp <= pos) & (kp > pos - NSA_WINDOW), 0.0, NEG_INF))
    pw = jnp.exp2(sw - jnp.max(sw, axis=-1, keepdims=True))
    o_win = _dot(stack(pw).astype(BF16), vw) / stack(jnp.sum(pw, axis=-1, keepdims=True))

    n_c = kc_ref.shape[0]
    cend = lax.broadcasted_iota(jnp.int32, (1, n_c), 1) * NSA_CMP_STRIDE + (NSA_CMP_BLOCK - 1)
    s = masked_scores(q4, kc_ref[...], jnp.where(cend <= pos, 0.0, NEG_INF))
    p = jnp.exp2(s - jnp.max(s, axis=-1, keepdims=True))
    p = p / jnp.sum(p, axis=-1, keepdims=True)
    p = jnp.where((pos >= NSA_CMP_BLOCK - 1)[None], p, 0.0)
    o_cmp = _dot(stack(p).astype(BF16), vc_ref[...])
    p_sum = jnp.sum(p, axis=0)
    imp_t = _dot(ovt_ref[...], p_sum, NT_DIMS, precision=lax.Precision.HIGHEST)

    j = lax.broadcasted_iota(jnp.int32, (n_sel, 1), 0)
    bq = pos_l // NSA_SEL_BLOCK
    forced = (j == 0) | (j == bq) | (j == bq - 1)
    score = jnp.where(j <= bq, imp_t + jnp.where(forced, NSA_FORCE_BONUS, 0.0), NEG_INF)
    sub = 8
    n_grp = n_sel // sub
    score_ref[...] = score
    rank_ref[...] = jnp.zeros_like(rank_ref)
    j_loc = lax.broadcasted_iota(jnp.int32, (sub, tq), 0)
    last_blk = (qi * tq + tq - 1) // NSA_SEL_BLOCK
    for gp in range(n_grp):
        @pl.when(gp * sub <= last_blk)
        def _(gp=gp):
            groups = [score_ref[v * sub:(v + 1) * sub, :] for v in range(n_grp)]
            ranks = [rank_ref[v * sub:(v + 1) * sub, :] for v in range(n_grp)]
            for jp in range(gp * sub, (gp + 1) * sub):
                row = score_ref[jp:jp + 1, :]
                for v, sg in enumerate(groups):
                    if v * sub > jp:
                        beats = row >= sg
                    elif (v + 1) * sub - 1 <= jp:
                        beats = row > sg
                    else:
                        beats = (row > sg) | ((row == sg) & (j_loc > jp - v * sub))
                    ranks[v] = ranks[v] + jnp.where(beats, 1.0, 0.0)
            for v in range(n_grp):
                rank_ref[v * sub:(v + 1) * sub, :] = ranks[v]
    not_sel = jnp.where(rank_ref[...] < n_top, 0.0, 1.0)
    not_sel = jnp.concatenate([not_sel, jnp.zeros((LANES - n_sel, tq), F32)], axis=0).T.astype(BF16)

    q_aug = jnp.concatenate([q4, jnp.concatenate([not_sel] * hpg, axis=0)], axis=1)

    def sel_tile(kt, carry, bias):
        m, l, acc = carry
        k0 = pl.multiple_of(kt * tk, tk)
        sc = masked_scores(q_aug, ks_ref[pl.ds(k0, tk), :], bias)
        m_new = jnp.maximum(m, jnp.max(sc, axis=-1, keepdims=True))
        alpha = jnp.exp2(m - m_new)
        pe = jnp.exp2(sc - m_new)
        l = alpha * l + jnp.sum(pe, axis=-1, keepdims=True)
        acc = stack(alpha) * acc + _dot(stack(pe).astype(BF16), vs_ref[pl.ds(k0, tk), :])
        return m_new, l, acc

    last = (qi * tq + tq - 1) // tk
    init = (jnp.full((hpg, tq, 1), NEG_INF, F32), jnp.zeros((hpg, tq, 1), F32), jnp.zeros((hpg * tq, d), F32))
    carry = lax.fori_loop(0, last, lambda kt, c: sel_tile(kt, c, None), init)
    key_pos = last * tk + lax.broadcasted_iota(jnp.int32, (1, tk), 1)
    _, l_sel, acc_sel = sel_tile(last, carry, jnp.where(key_pos <= pos, 0.0, NEG_INF))
    o_sel = acc_sel / stack(l_sel)

    gt = jax.nn.sigmoid(gate_ref[...])
    lane = lax.broadcasted_iota(jnp.int32, (1, LANES), 1)
    for h in range(hpg):
        rows = slice(h * tq, (h + 1) * tq)
        o_h = jnp.zeros((tq, d), F32)
        for br, o_br in enumerate((o_cmp, o_sel, o_win)):
            g_col = jnp.sum(jnp.where(lane == gi * (3 * hpg) + h * 3 + br, gt, 0.0), axis=-1, keepdims=True)
            o_h = o_h + g_col * o_br[rows]
        o_ref[:, h * d:(h + 1) * d] = o_h.astype(BF16)


def selection_overlap(n_rows, n_sel):
    n_cmp = n_rows - 1
    c0 = np.arange(n_rows) * NSA_CMP_STRIDE
    s0 = np.arange(n_sel) * NSA_SEL_BLOCK
    ov = np.minimum(c0[:, None] + NSA_CMP_BLOCK, s0[None, :] + NSA_SEL_BLOCK) - np.maximum(c0[:, None], s0[None, :])
    ov = np.clip(ov, 0, None) / NSA_CMP_BLOCK
    ov[n_cmp:] = 0.0
    return jnp.asarray(ov, dtype=F32)


def nsa_attention(proj, kv_cmp, gate, q_gain, k_gains, tabs, tq=256, tk=512):
    b, s, _ = proj.shape
    g, d, hpg = NSA_KV_GROUPS, NSA_HEAD_DIM, NSA_HPG
    n_rows = s // NSA_CMP_STRIDE
    n_sel = s // NSA_SEL_BLOCK
    seg_blk = lambda i: NSA_HEADS + i * g
    cmp_spec = lambda kv: pl.BlockSpec((None, None, None, n_rows, d), lambda bi, gi, qi: (kv, bi, gi, 0, 0))
    seq_spec = lambda i: pl.BlockSpec((None, s, d), lambda bi, gi, qi: (bi, 0, seg_blk(i) + gi))
    const = lambda shape: pl.BlockSpec(shape, lambda bi, gi, qi: (0, 0))
    return pl.pallas_call(
        functools.partial(_nsa_attn_kernel, tq=tq, tk=tk, seq=s),
        grid=(b, g, s // tq),
        in_specs=[pl.BlockSpec((None, tq, hpg * d), lambda bi, gi, qi: (bi, qi, gi)),
                  cmp_spec(0), cmp_spec(1),
                  seq_spec(2), seq_spec(3), seq_spec(4), seq_spec(5),
                  pl.BlockSpec((None, tq, LANES), lambda bi, gi, qi: (bi, qi, 0)),
                  const((n_sel, n_rows)), const((1, d)), const((2, d)),
                  const((s, d)), const((s, d)), const((s, d))],
        out_specs=pl.BlockSpec((None, tq, hpg * d), lambda bi, gi, qi: (bi, qi, gi)),
        out_shape=jax.ShapeDtypeStruct((b, s, NSA_HEADS * d), BF16),
        scratch_shapes=[pltpu.VMEM((s, 2 * d), BF16)] + [pltpu.VMEM((s, d), BF16)] * 3
                       + [pltpu.VMEM((hpg * tq, d), BF16)] + [pltpu.VMEM((n_sel, tq), F32)] * 2,
        compiler_params=_params(("parallel", "parallel", "arbitrary")),
        name="nsa_attention",
    )(proj, kv_cmp, kv_cmp, proj, proj, proj, proj, gate, selection_overlap(n_rows, n_sel).T,
      q_gain.reshape(1, d), k_gains, *tabs)


def nsa_mixer(x2, b, s, gain, w_in, w_out, layer, q_norm, k_norm, cmp_pe, cmp_w1, cmp_w2):
    d, g = NSA_HEAD_DIM, NSA_KV_GROUPS
    n_main = (NSA_HEADS + 6 * g) * d
    w_gate = jnp.pad(w_in[layer, :, n_main:], ((0, 0), (0, LANES - 3 * NSA_HEADS)))
    proj, gate = norm_matmul(x2, gain, w_in, layer, n_main, w_gate)
    proj = proj.reshape(b, s, n_main)
    gate = gate.reshape(b, s, LANES)
    tabs = rope_tables(jnp.arange(s), d)
    cmp_end = jnp.arange(s // NSA_CMP_STRIDE) * NSA_CMP_STRIDE + (NSA_CMP_BLOCK - 1)
    kv_cmp = nsa_compress(proj, cmp_pe, cmp_w1, cmp_w2, k_norm[0], rope_tables(cmp_end, d))
    o = nsa_attention(proj, kv_cmp, gate, q_norm, k_norm[1:3], tabs)
    return matmul_residual(o.reshape(b * s, NSA_HEADS * d), w_out, layer, x2)


def _log_sigmoid(z):
    return jnp.minimum(z, 0.0) - jnp.log1p(jnp.exp(-jnp.abs(z)))


def _gla_kernel(q_ref, k_ref, v_ref, r_ref, glr_ref, wg_ref, bg_ref, on_ref, o_ref, state_ref, *, rows, hps):
    c = GLA_CHUNK
    sub = GLA_SUB
    dk, dv = GLA_KEY_DIM, GLA_VAL_DIM

    @pl.when(pl.program_id(2) == 0)
    def _():
        state_ref[...] = jnp.zeros_like(state_ref)

    tri = jnp.where(lax.broadcasted_iota(jnp.int32, (c, c), 0) >= lax.broadcasted_iota(jnp.int32, (c, c), 1), 1.0, 0.0)
    sub_row = lax.broadcasted_iota(jnp.int32, (sub, 1), 0)

    def head_chunk(hh, r0):
        kcols = slice(hh * dk, (hh + 1) * dk)
        vcols = slice(hh * dv, (hh + 1) * dv)
        qc = q_ref[pl.ds(r0, c), kcols] * (dk ** -0.5)
        kc = k_ref[pl.ds(r0, c), kcols]
        vc = v_ref[pl.ds(r0, c), vcols]
        z = _dot(glr_ref[pl.ds(r0, c), :], wg_ref[:, kcols], precision=lax.Precision.HIGHEST) + bg_ref[:, kcols]
        log_a = _log_sigmoid(z) / GLA_TAU
        cum = _dot(tri, log_a, precision=lax.Precision.HIGHEST)
        vb = vc.astype(BF16)

        state = state_ref[hh]
        o_inter = _dot((qc * jnp.exp(cum)).astype(BF16), state.astype(BF16), NT_DIMS)

        parts = []
        for bi in range(c // sub):
            lo = bi * sub
            q_i, k_i, v_i, c_i = qc[lo:lo + sub], kc[lo:lo + sub], vc[lo:lo + sub], cum[lo:lo + sub]
            o_i = o_inter[lo:lo + sub]
            if bi > 0:
                c0 = c_i[0:1]
                qs = (q_i * jnp.exp(c_i - c0)).astype(BF16)
                kp = (kc[:lo] * jnp.exp(c0 - cum[:lo])).astype(BF16)
                att = _dot(qs, kp, NT_DIMS)
                o_i = o_i + _dot(att.astype(BF16), vb[:lo])
            for jj in range(sub):
                e = jnp.exp(jnp.where(sub_row >= jj, c_i - c_i[jj:jj + 1], NEG_INF))
                col = jnp.sum(q_i * k_i[jj:jj + 1] * e, axis=-1, keepdims=True)
                o_i = o_i + col * v_i[jj:jj + 1]
            parts.append(o_i)
        o = jnp.concatenate(parts, axis=0)

        last = cum[c - 1:c]
        kd = (kc * jnp.exp(last - cum)).astype(BF16)
        state_ref[hh] = state * jnp.exp(last) + _dot(vb, kd, TN_DIMS)

        ms = jnp.mean(o * o, axis=-1, keepdims=True)
        on = o * lax.rsqrt(ms + NORM_EPS) * on_ref[...]
        r = r_ref[pl.ds(r0, c), vcols]
        o_ref[pl.ds(r0, c), vcols] = (on * (r * jax.nn.sigmoid(r))).astype(BF16)

    def chunk(ci, carry):
        r0 = pl.multiple_of(ci * c, c)
        for hh in range(hps):
            head_chunk(hh, r0)
        return carry

    lax.fori_loop(0, rows // c, chunk, 0)


def gla_attention(proj, glr, w_gate_up, b_gate, o_norm, rows=512, hps=4):
    b, s, _ = proj.shape
    nh, dk, dv = GLA_HEADS, GLA_KEY_DIM, GLA_VAL_DIM
    wg = jnp.pad(w_gate_up, ((0, LANES - GLA_GATE_RANK), (0, 0)))
    kw, vw = hps * dk, hps * dv
    k_blk = nh * dk // kw
    v_blk = 2 * nh * dk // vw
    r_blk = (2 * nh * dk + nh * dv) // vw
    return pl.pallas_call(
        functools.partial(_gla_kernel, rows=rows, hps=hps),
        grid=(b, nh // hps, s // rows),
        in_specs=[pl.BlockSpec((None, rows, kw), lambda bi, h, ci: (bi, ci, h)),
                  pl.BlockSpec((None, rows, kw), lambda bi, h, ci: (bi, ci, k_blk + h)),
                  pl.BlockSpec((None, rows, vw), lambda bi, h, ci: (bi, ci, v_blk + h)),
                  pl.BlockSpec((None, rows, vw), lambda bi, h, ci: (bi, ci, r_blk + h)),
                  pl.BlockSpec((None, rows, LANES), lambda bi, h, ci: (bi, ci, 0)),
                  pl.BlockSpec((LANES, kw), lambda bi, h, ci: (0, h)),
                  pl.BlockSpec((1, kw), lambda bi, h, ci: (0, h)),
                  pl.BlockSpec((1, dv), lambda bi, h, ci: (0, 0))],
        out_specs=pl.BlockSpec((None, rows, vw), lambda bi, h, ci: (bi, ci, h)),
        out_shape=jax.ShapeDtypeStruct((b, s, nh * dv), BF16),
        scratch_shapes=[pltpu.VMEM((hps, dv, dk), F32)],
        compiler_params=_params(("parallel", "parallel", "arbitrary")),
        name="gla_attention",
    )(proj, proj, proj, proj, glr, wg, b_gate.reshape(1, nh * dk), o_norm.reshape(1, dv))


def gla_mixer(x2, b, s, gain, w_in, w_gate_up, b_gate, o_norm, w_out, layer):
    nh, dk, dv = GLA_HEADS, GLA_KEY_DIM, GLA_VAL_DIM
    n_qkv = 2 * nh * dk + nh * dv
    w_r = w_in[layer, :, n_qkv + GLA_GATE_RANK:]
    w_glr = jnp.pad(w_in[layer, :, n_qkv:n_qkv + GLA_GATE_RANK], ((0, 0), (0, LANES - GLA_GATE_RANK)))
    proj, glr = norm_matmul(x2, gain, w_in, layer, n_qkv, w_glr, w_tail=w_r)
    o = gla_attention(proj.reshape(b, s, -1), glr.reshape(b, s, LANES), w_gate_up, b_gate, o_norm)
    return matmul_residual(o.reshape(b * s, nh * dv), w_out, layer, x2)


def _swa_heads(x, gain, tabs, seg, out_scale):
    blk = x.shape[0]
    d = SWA_HEAD_DIM
    n = x.shape[1] // LANES
    tiles = jnp.concatenate([x[:, t * LANES:(t + 1) * LANES] for t in range(n)], axis=0)
    sq = tiles * tiles
    hi = sq.astype(BF16)
    lo = (sq - hi.astype(F32)).astype(BF16)
    ms = (_dot(hi, seg) + _dot(lo, seg)) * (1.0 / d)
    heads = []
    for t in range(n):
        rows = slice(t * blk, (t + 1) * blk)
        y = _norm_rope(tiles[rows], gain, *tabs, d // ROPE_FRACTION // 2, ms[rows]) * out_scale
        heads += [y[:, u * d:(u + 1) * d] for u in range(LANES // d)]
    return heads


def _swa_kernel(q_ref, kp_ref, kc_ref, vp_ref, vc_ref, sink_ref, qg_ref, kg_ref, seg_ref,
                cc_ref, s1c_ref, s2c_ref, cp_ref, s1p_ref, s2p_ref, o_ref, *, blk):
    qi = pl.program_id(1)
    hpg, d, g = SWA_HPG, SWA_HEAD_DIM, SWA_KV_HEADS
    seg = seg_ref[...]
    tabs_c = (cc_ref[...], s1c_ref[...], s2c_ref[...])
    tabs_p = (cp_ref[...], s1p_ref[...], s2p_ref[...])
    q_heads = _swa_heads(q_ref[...], qg_ref[...], tabs_c, seg, d ** -0.5)
    kp_heads = _swa_heads(kp_ref[...], kg_ref[...], tabs_p, seg, 1.0)
    kc_heads = _swa_heads(kc_ref[...], kg_ref[...], tabs_c, seg, 1.0)
    vp, vc = vp_ref[...], vc_ref[...]

    qp = qi * blk + lax.broadcasted_iota(jnp.int32, (blk, 1), 0)
    kpos = (qi - 1) * blk + lax.broadcasted_iota(jnp.int32, (1, 2 * blk), 1)
    bias = jnp.where((kpos <= qp) & (kpos > qp - SWA_WINDOW) & (kpos >= 0), 0.0, NEG_INF)

    for gi in range(g):
        q = jnp.concatenate(q_heads[gi * hpg:(gi + 1) * hpg], axis=0).astype(BF16)
        k = jnp.concatenate([kp_heads[gi], kc_heads[gi]], axis=0).astype(BF16)
        v = jnp.concatenate([vp[:, gi * d:(gi + 1) * d], vc[:, gi * d:(gi + 1) * d]], axis=0).astype(BF16)
        s = _dot(q, k, NT_DIMS).reshape(hpg, blk, 2 * blk) + bias[None]
        sink = sink_ref[gi]
        m = jnp.maximum(jnp.max(s, axis=-1, keepdims=True), sink)
        p = jnp.exp(s - m)
        denom = jnp.sum(p, axis=-1, keepdims=True) + jnp.exp(sink - m)
        o = _dot(p.reshape(hpg * blk, 2 * blk).astype(BF16), v).reshape(hpg, blk, d) / denom
        o_ref[:, gi * hpg * d:(gi + 1) * hpg * d] = jnp.concatenate([o[h] for h in range(hpg)], axis=1).astype(BF16)


def swa_attention(proj, sinks, q_gain, k_gain, tabs, blk=128):
    b, s, _ = proj.shape
    g, hpg, d = SWA_KV_HEADS, SWA_HPG, SWA_HEAD_DIM
    nq, nkv = SWA_HEADS * d, g * d
    k_blk = nq // nkv
    prev = lambda qi: jnp.maximum(qi - 1, 0)
    kv_spec = lambda col, row: pl.BlockSpec((None, blk, nkv), lambda bi, qi: (bi, row(qi), col))
    const = lambda shape: pl.BlockSpec(shape, lambda bi, qi: (0,) * len(shape))
    tab_spec = lambda row: pl.BlockSpec((blk, LANES), lambda bi, qi: (row(qi), 0))
    cur = lambda qi: qi
    seg_id = np.arange(LANES) // d
    seg = jnp.asarray(seg_id[:, None] == seg_id[None, :], dtype=BF16)
    tile_gain = lambda gn: jnp.tile(gn.reshape(1, d), (1, LANES // d))
    return pl.pallas_call(
        functools.partial(_swa_kernel, blk=blk),
        grid=(b, s // blk),
        in_specs=[pl.BlockSpec((None, blk, nq), lambda bi, qi: (bi, qi, 0)),
                  kv_spec(k_blk, prev), kv_spec(k_blk, cur), kv_spec(k_blk + 1, prev), kv_spec(k_blk + 1, cur),
                  const((g, hpg, 1, 1)), const((1, LANES)), const((1, LANES)), const((LANES, LANES)),
                  tab_spec(cur), tab_spec(cur), tab_spec(cur), tab_spec(prev), tab_spec(prev), tab_spec(prev)],
        out_specs=pl.BlockSpec((None, blk, nq), lambda bi, qi: (bi, qi, 0)),
        out_shape=jax.ShapeDtypeStruct((b, s, nq), BF16),
        compiler_params=_params(("parallel", "parallel")),
        name="swa_attention",
    )(proj, proj, proj, proj, proj, sinks.astype(F32).reshape(g, hpg, 1, 1), tile_gain(q_gain), tile_gain(k_gain),
      seg, *tabs, *tabs)


def swa_mixer(x2, b, s, gain, w_in, w_out, layer, q_norm, k_norm, sinks):
    d, g = SWA_HEAD_DIM, SWA_KV_HEADS
    n_in = (SWA_HEADS + 2 * g) * d
    (proj,) = norm_matmul(x2, gain, w_in, layer, n_in)
    o = swa_attention(proj.reshape(b, s, n_in), sinks, q_norm, k_norm, rope_tables(jnp.arange(s), d))
    return matmul_residual(o.reshape(b * s, SWA_HEADS * d), w_out, layer, x2)


def kernel(x, norm_mix, norm_mlp, mlp_w_up, mlp_w_down, nsa_w_in, nsa_w_out, nsa_q_norm, nsa_k_norm, nsa_cmp_pe, nsa_cmp_w1, nsa_cmp_w2, gla_w_in, gla_w_gate_up, gla_b_gate, gla_o_norm, gla_w_out, swa_w_in, swa_w_out, swa_q_norm, swa_k_norm, swa_sinks):
    b, s, d = x.shape
    x2 = x.reshape(b * s, d)
    ia = ib = ic = 0
    for i in range(norm_mix.shape[0]):
        kind = i % N_MIXERS
        if kind == 0:
            x2 = nsa_mixer(x2, b, s, norm_mix[i], nsa_w_in, nsa_w_out, ia, nsa_q_norm[ia], nsa_k_norm[ia],
                           nsa_cmp_pe[ia], nsa_cmp_w1[ia], nsa_cmp_w2[ia])
            ia += 1
        elif kind == 1:
            x2 = gla_mixer(x2, b, s, norm_mix[i], gla_w_in, gla_w_gate_up[ib], gla_b_gate[ib],
                           gla_o_norm[ib], gla_w_out, ib)
            ib += 1
        else:
            x2 = swa_mixer(x2, b, s, norm_mix[i], swa_w_in, swa_w_out, ic, swa_q_norm[ic], swa_k_norm[ic],
                           swa_sinks[ic])
            ic += 1
        x2 = mlp_block(x2, norm_mlp[i], mlp_w_up, mlp_w_down, i)
    return x2.reshape(b, s, d)
```

```python
import functools

import numpy as np
import jax
import jax.numpy as jnp
from jax import lax
from jax.experimental import pallas as pl
from jax.experimental.pallas import tpu as pltpu

F32 = jnp.float32
BF16 = jnp.bfloat16

NORM_EPS = 1e-6
ROPE_THETA = 500000.0
ROPE_FRACTION = 4
NEG_INF = -1e30
N_MIXERS = 3

NSA_HEAD_DIM = 128
NSA_HEADS = 16
NSA_KV_GROUPS = 4
NSA_HPG = NSA_HEADS // NSA_KV_GROUPS
NSA_CMP_BLOCK = 32
NSA_CMP_STRIDE = 16
NSA_SEL_BLOCK = 64
NSA_TOP_N = 16
NSA_WINDOW = 512
NSA_FORCE_BONUS = 1e4
NSA_SCALE2 = float(NSA_HEAD_DIM ** -0.5 * np.log2(np.e))
NSA_MASK_NEG = -2.0 ** 100

GLA_HEADS = 4
GLA_KEY_DIM = 256
GLA_VAL_DIM = 512
GLA_GATE_RANK = 16
GLA_TAU = 16.0
GLA_CHUNK = 64
GLA_SUB = 16

SWA_HEAD_DIM = 64
SWA_HEADS = 32
SWA_KV_HEADS = 4
SWA_HPG = SWA_HEADS // SWA_KV_HEADS
SWA_WINDOW = 128

LANES = 128
VMEM_LIMIT = 56 * 1024 * 1024

NT_DIMS = (((1,), (1,)), ((), ()))
TN_DIMS = (((0,), (0,)), ((), ()))


def _params(sem):
    return pltpu.CompilerParams(dimension_semantics=sem, vmem_limit_bytes=VMEM_LIMIT)


def _dot(a, b, dims=None, precision=None):
    if dims is None:
        return jnp.dot(a, b, preferred_element_type=F32, precision=precision)
    return lax.dot_general(a, b, dims, preferred_element_type=F32, precision=precision)


def _rms_rows_to(h_scr, x_ref, g_ref, rows):
    n = x_ref.shape[0] // rows

    def body(i, c):
        r0 = pl.multiple_of(i * rows, rows)
        x = x_ref[pl.ds(r0, rows), :]
        ms = jnp.mean(x * x, axis=-1, keepdims=True)
        h_scr[pl.ds(r0, rows), :] = (x * lax.rsqrt(ms + NORM_EPS) * g_ref[...]).astype(BF16)
        return c

    lax.fori_loop(0, n, body, 0)


def _row_tile_copy(x_hbm, x_buf, sem, i):
    tm = x_buf.shape[0]
    return pltpu.make_async_copy(x_hbm.at[pl.ds(pl.multiple_of(i * tm, tm), tm), :], x_buf, sem)


def _fetch_and_norm(x_hbm, x_buf, sem, g_ref, h_scr, prefetch_step):
    i, j = pl.program_id(0), pl.program_id(1)

    @pl.when(j == 0)
    def _():
        @pl.when(i == 0)
        def _():
            _row_tile_copy(x_hbm, x_buf, sem, 0).start()

        _row_tile_copy(x_hbm, x_buf, sem, i).wait()
        _rms_rows_to(h_scr, x_buf, g_ref, 128)

    @pl.when((j == prefetch_step) & (i + 1 < pl.num_programs(0)))
    def _():
        _row_tile_copy(x_hbm, x_buf, sem, i + 1).start()


def _norm_matmul_kernel(x_hbm, g_ref, w_ref, *rest, has_extra, n_head_tiles):
    rest = list(rest)
    wt_ref = rest.pop(0) if n_head_tiles is not None else None
    if has_extra:
        wx_ref, o_ref, ox_ref, h_scr, x_buf, sem = rest
    else:
        o_ref, h_scr, x_buf, sem = rest
    j = pl.program_id(1)
    _fetch_and_norm(x_hbm, x_buf, sem, g_ref, h_scr, prefetch_step=1)

    if has_extra:
        @pl.when(j == 0)
        def _():
            ox_ref[...] = _dot(h_scr[...], wx_ref[...].astype(BF16))

    if wt_ref is None:
        o_ref[...] = _dot(h_scr[...], w_ref[...].astype(BF16))
    else:
        @pl.when(j < n_head_tiles)
        def _():
            o_ref[...] = _dot(h_scr[...], w_ref[...].astype(BF16))

        @pl.when(j >= n_head_tiles)
        def _():
            o_ref[...] = _dot(h_scr[...], wt_ref[...].astype(BF16))


def norm_matmul(x, gain, w, layer, n_cols, w_extra=None, w_tail=None, tm=2048, tn=512):
    m, d = x.shape
    tm = min(tm, m)
    has_extra = w_extra is not None
    n_head = n_cols // tn
    n_tail = 0 if w_tail is None else w_tail.shape[1] // tn
    in_specs = [pl.BlockSpec(memory_space=pl.ANY),
                pl.BlockSpec((1, d), lambda i, j: (0, 0)),
                pl.BlockSpec((None, d, tn), lambda i, j: (layer, 0, jnp.minimum(j, n_head - 1)))]
    out_shape = [jax.ShapeDtypeStruct((m, (n_head + n_tail) * tn), F32)]
    out_specs = [pl.BlockSpec((tm, tn), lambda i, j: (i, j))]
    args = [x, gain.reshape(1, d), w]
    if w_tail is not None:
        in_specs.append(pl.BlockSpec((d, tn), lambda i, j: (0, jnp.maximum(j - n_head, 0))))
        args.append(w_tail)
    if has_extra:
        in_specs.append(pl.BlockSpec((d, LANES), lambda i, j: (0, 0)))
        out_shape.append(jax.ShapeDtypeStruct((m, LANES), F32))
        out_specs.append(pl.BlockSpec((tm, LANES), lambda i, j: (i, 0)))
        args.append(w_extra)
    return pl.pallas_call(
        functools.partial(_norm_matmul_kernel, has_extra=has_extra,
                          n_head_tiles=None if w_tail is None else n_head),
        grid=(m // tm, n_head + n_tail),
        in_specs=in_specs, out_specs=out_specs, out_shape=out_shape,
        scratch_shapes=[pltpu.VMEM((tm, d), BF16), pltpu.VMEM((tm, d), F32), pltpu.SemaphoreType.DMA(())],
        compiler_params=_params(("arbitrary", "arbitrary")),
        name="norm_matmul",
    )(*args)


def _matmul_residual_kernel(a_ref, w_ref, r_ref, o_ref):
    o_ref[...] = r_ref[...] + _dot(a_ref[...], w_ref[...].astype(BF16))


def matmul_residual(a, w, layer, res, tm=2048, tn=512):
    m, k = a.shape
    tm = min(tm, m)
    n = w.shape[2]
    return pl.pallas_call(
        _matmul_residual_kernel,
        grid=(m // tm, n // tn),
        in_specs=[pl.BlockSpec((tm, k), lambda i, j: (i, 0)),
                  pl.BlockSpec((None, k, tn), lambda i, j: (layer, 0, j)),
                  pl.BlockSpec((tm, tn), lambda i, j: (i, j))],
        out_specs=pl.BlockSpec((tm, tn), lambda i, j: (i, j)),
        out_shape=jax.ShapeDtypeStruct((m, n), F32),
        compiler_params=_params(("parallel", "parallel")),
        name="matmul_residual",
    )(a, w, res)


def _mlp_kernel(x_hbm, g_ref, wu_ref, xt_ref, wd_ref, o_ref, h_scr, u_scr, x_buf, sem, *, nf, tf, splits):
    j = pl.program_id(1)
    d = h_scr.shape[1]
    per = nf // splits
    _fetch_and_norm(x_hbm, x_buf, sem, g_ref, h_scr, prefetch_step=nf)

    @pl.when(j < nf)
    def _():
        u = _dot(h_scr[:, 0:tf], wu_ref[0:tf, :].astype(BF16))
        for c in range(1, d // tf):
            u = u + _dot(h_scr[:, c * tf:(c + 1) * tf], wu_ref[c * tf:(c + 1) * tf, :].astype(BF16))
        u = jnp.maximum(u, 0.0)
        u_scr[j] = (u * u).astype(BF16)

    for part in range(splits):
        @pl.when((j >= nf) & ((j - nf) % splits == part))
        def _(part=part):
            acc = xt_ref[...] if part == 0 else o_ref[...]
            for f in range(per):
                acc = acc + _dot(u_scr[part * per + f], wd_ref[f * tf:(f + 1) * tf, :].astype(BF16))
            o_ref[...] = acc


def mlp_block(x, gain, w_up, w_down, layer, tm=1024, tf=512, tn=512, splits=4):
    m, d = x.shape
    ff = w_up.shape[2]
    nf, nn = ff // tf, d // tn
    down = lambda j: jnp.maximum(j - nf, 0)
    return pl.pallas_call(
        functools.partial(_mlp_kernel, nf=nf, tf=tf, splits=splits),
        grid=(m // tm, nf + nn * splits),
        in_specs=[pl.BlockSpec(memory_space=pl.ANY),
                  pl.BlockSpec((1, d), lambda i, j: (0, 0)),
                  pl.BlockSpec((None, d, tf), lambda i, j: (layer, 0, jnp.minimum(j, nf - 1))),
                  pl.BlockSpec((tm, tn), lambda i, j: (i, down(j) // splits)),
                  pl.BlockSpec((None, ff // splits, tn), lambda i, j: (layer, down(j) % splits, down(j) // splits))],
        out_specs=pl.BlockSpec((tm, tn), lambda i, j: (i, down(j) // splits)),
        out_shape=jax.ShapeDtypeStruct((m, d), F32),
        scratch_shapes=[pltpu.VMEM((tm, d), BF16), pltpu.VMEM((nf, tm, tf), BF16), pltpu.VMEM((tm, d), F32),
                        pltpu.SemaphoreType.DMA(())],
        compiler_params=_params(("arbitrary", "arbitrary")),
        name="mlp_block",
    )(x, gain.reshape(1, d), w_up, x, w_down)


def rope_tables(pos, head_dim):
    rot = head_dim // ROPE_FRACTION
    half = rot // 2
    inv_freq = jnp.power(jnp.float32(ROPE_THETA), -jnp.arange(half, dtype=F32) / half)
    ang = pos.astype(F32)[:, None] * inv_freq[None, :]
    cos, sin = jnp.cos(ang), jnp.sin(ang)
    n = pos.shape[0]
    zeros = jnp.zeros((n, head_dim - rot), F32)
    c = jnp.concatenate([cos, cos, jnp.ones((n, head_dim - rot), F32)], axis=-1)
    s1 = jnp.concatenate([-sin, jnp.zeros((n, half), F32), zeros], axis=-1)
    s2 = jnp.concatenate([jnp.zeros((n, half), F32), sin, zeros], axis=-1)
    reps = LANES // head_dim
    return tuple(jnp.tile(t, (1, reps)) for t in (c, s1, s2))


def _norm_rope(x, gain, c, s1, s2, half, ms=None):
    if ms is None:
        ms = jnp.mean(x * x, axis=-1, keepdims=True)
    y = x * lax.rsqrt(ms + NORM_EPS) * gain
    return y * c + pltpu.roll(y, LANES - half, 1) * s1 + pltpu.roll(y, half, 1) * s2


def _gelu_tanh(x):
    return 0.5 * x * (1.0 + jnp.tanh(np.sqrt(2.0 / np.pi) * (x + 0.044715 * (x * x * x))))


def _nsa_compress_kernel(x_ref, pe_ref, w1_ref, w2_ref, g_ref, c_ref, s1_ref, s2_ref, o_ref):
    n_rows = x_ref.shape[0] // NSA_CMP_STRIDE
    d = NSA_HEAD_DIM
    y0 = jnp.zeros((n_rows, d), F32)
    y1 = jnp.zeros((n_rows, d), F32)
    for r in range(NSA_CMP_STRIDE):
        xr = x_ref[pl.ds(r, n_rows, stride=NSA_CMP_STRIDE), :]
        a0 = (xr + pe_ref[r:r + 1, :]).astype(BF16)
        a1 = (xr + pe_ref[NSA_CMP_STRIDE + r:NSA_CMP_STRIDE + r + 1, :]).astype(BF16)
        y0 = y0 + _dot(a0, w1_ref[r * d:(r + 1) * d, :].astype(BF16))
        y1 = y1 + _dot(a1, w1_ref[(NSA_CMP_STRIDE + r) * d:(NSA_CMP_STRIDE + r + 1) * d, :].astype(BF16))
    pre = y0 + pltpu.roll(y1, n_rows - 1, 0)
    out = _dot(_gelu_tanh(pre).astype(BF16), w2_ref[...].astype(BF16))
    roped = _norm_rope(out, g_ref[...], c_ref[...], s1_ref[...], s2_ref[...],
                       NSA_HEAD_DIM // ROPE_FRACTION // 2, None)
    o_ref[...] = jnp.where(pl.program_id(0) == 0, roped * NSA_SCALE2, out).astype(BF16)


def nsa_compress(proj, pe, w1, w2, k_gain, tables):
    b, s, _ = proj.shape
    g, d = NSA_KV_GROUPS, NSA_HEAD_DIM
    n_rows = s // NSA_CMP_STRIDE
    col_blk = NSA_HEADS
    tab_spec = pl.BlockSpec((n_rows, d), lambda kv, bi, gi: (0, 0))
    return pl.pallas_call(
        _nsa_compress_kernel,
        grid=(2, b, g),
        in_specs=[pl.BlockSpec((None, s, d), lambda kv, bi, gi: (bi, 0, col_blk + kv * g + gi)),
                  pl.BlockSpec((None, NSA_CMP_BLOCK, d), lambda kv, bi, gi: (kv, 0, 0)),
                  pl.BlockSpec((None, NSA_CMP_BLOCK * d, d), lambda kv, bi, gi: (kv, 0, 0)),
                  pl.BlockSpec((None, d, d), lambda kv, bi, gi: (kv, 0, 0)),
                  pl.BlockSpec((1, d), lambda kv, bi, gi: (0, 0)),
                  tab_spec, tab_spec, tab_spec],
        out_specs=pl.BlockSpec((None, None, None, n_rows, d), lambda kv, bi, gi: (kv, bi, gi, 0, 0)),
        out_shape=jax.ShapeDtypeStruct((2, b, g, n_rows, d), BF16),
        compiler_params=_params(("parallel", "parallel", "parallel")),
        name="nsa_compress",
    )(proj, pe, w1, w2, k_gain.reshape(1, d), *tables)


def _nsa_attn_kernel(q_ref, kc_ref, vc_ref, ksr_ref, vsr_ref, kwr_ref, vwr_ref, gate_ref, ovt_ref,
                     qg_ref, kg_ref, c_ref, s1_ref, s2_ref, o_ref, ks_ref, vs_ref, kw_ref, vw_ref, q_scr, score_ref, rank_ref,
                     *, tq, tk, seq):
    gi = pl.program_id(1)
    qi = pl.program_id(2)
    d = NSA_HEAD_DIM
    hpg = NSA_HPG
    n_sel = seq // NSA_SEL_BLOCK
    n_top = min(NSA_TOP_N, n_sel)
    half = d // ROPE_FRACTION // 2

    def tables(rows):
        return c_ref[rows, :], s1_ref[rows, :], s2_ref[rows, :]

    scale2 = NSA_SCALE2

    @pl.when(qi == 0)
    def _():
        chunk = 512
        lane = lax.broadcasted_iota(jnp.int32, (chunk, d), 1)
        row = lax.broadcasted_iota(jnp.int32, (chunk, d), 0)

        def body(i, carry):
            r0 = pl.multiple_of(i * chunk, chunk)
            rows = pl.ds(r0, chunk)
            ks = _norm_rope(ksr_ref[rows, :], kg_ref[0:1, :], *tables(rows), half, None) * scale2
            kw = _norm_rope(kwr_ref[rows, :], kg_ref[1:2, :], *tables(rows), half, None) * scale2
            ks_ref[rows, 0:d] = ks.astype(BF16)
            ks_ref[rows, d:2 * d] = jnp.where((r0 + row) // NSA_SEL_BLOCK == lane, NSA_MASK_NEG, 0.0).astype(BF16)
            kw_ref[rows, :] = kw.astype(BF16)
            vs_ref[rows, :] = vsr_ref[rows, :].astype(BF16)
            vw_ref[rows, :] = vwr_ref[rows, :].astype(BF16)
            return carry

        lax.fori_loop(0, seq // chunk, body, 0)

    q_rows = 128

    def q_body(i, carry):
        r0 = pl.multiple_of(i * q_rows, q_rows)
        tabs = tables(pl.ds(pl.multiple_of(qi * tq, tq) + r0, q_rows))
        for h in range(hpg):
            y = _norm_rope(q_ref[pl.ds(r0, q_rows), h * d:(h + 1) * d], qg_ref[...], *tabs, half, None)
            q_scr[pl.ds(h * tq + r0, q_rows), :] = y.astype(BF16)
        return carry

    lax.fori_loop(0, tq // q_rows, q_body, 0)
    q4 = q_scr[...]
    pos = qi * tq + lax.broadcasted_iota(jnp.int32, (tq, 1), 0)
    pos_l = qi * tq + lax.broadcasted_iota(jnp.int32, (1, tq), 1)

    def masked_scores(qs, k, bias=None):
        sc = _dot(qs, k, NT_DIMS)
        sc = sc.reshape(-1, tq, sc.shape[-1])
        return sc if bias is None else sc + bias[None]

    def stack(t):
        return t.reshape(t.shape[0] * tq, -1)

    span = NSA_WINDOW + tq
    w0 = pl.multiple_of(jnp.clip(qi * tq - NSA_WINDOW, 0, seq - span), tq)
    vw = vw_ref[pl.ds(w0, span), :]
    kp = w0 + lax.broadcasted_iota(jnp.int32, (1, span), 1)
    sw = masked_scores(q4, kw_ref[pl.ds(w0, span), :],
                       jnp.where((kp <= pos) & (kp > pos - NSA_WINDOW), 0.0, NEG_INF))
    pw = jnp.exp2(sw - jnp.max(sw, axis=-1, keepdims=True))
    o_win = _dot(stack(pw).astype(BF16), vw) / stack(jnp.sum(pw, axis=-1, keepdims=True))

    n_c = kc_ref.shape[0]
    cend = lax.broadcasted_iota(jnp.int32, (1, n_c), 1) * NSA_CMP_STRIDE + (NSA_CMP_BLOCK - 1)
    s = masked_scores(q4, kc_ref[...], jnp.where(cend <= pos, 0.0, NEG_INF))
    p = jnp.exp2(s - jnp.max(s, axis=-1, keepdims=True))
    p = p / jnp.sum(p, axis=-1, keepdims=True)
    p = jnp.where((pos >= NSA_CMP_BLOCK - 1)[None], p, 0.0)
    o_cmp = _dot(stack(p).astype(BF16), vc_ref[...])
    p_sum = jnp.sum(p, axis=0)
    imp_t = _dot(ovt_ref[...], p_sum, NT_DIMS, precision=lax.Precision.HIGHEST)

    j = lax.broadcasted_iota(jnp.int32, (n_sel, 1), 0)
    bq = pos_l // NSA_SEL_BLOCK
    forced = (j == 0) | (j == bq) | (j == bq - 1)
    score = jnp.where(j <= bq, imp_t + jnp.where(forced, NSA_FORCE_BONUS, 0.0), NEG_INF)
    sub = 8
    n_grp = n_sel // sub
    score_ref[...] = score
    rank_ref[...] = jnp.zeros_like(rank_ref)
    j_loc = lax.broadcasted_iota(jnp.int32, (sub, tq), 0)
    last_blk = (qi * tq + tq - 1) // NSA_SEL_BLOCK
    for gp in range(n_grp):
        @pl.when(gp * sub <= last_blk)
        def _(gp=gp):
            groups = [score_ref[v * sub:(v + 1) * sub, :] for v in range(n_grp)]
            ranks = [rank_ref[v * sub:(v + 1) * sub, :] for v in range(n_grp)]
            for jp in range(gp * sub, (gp + 1) * sub):
                row = score_ref[jp:jp + 1, :]
                for v, sg in enumerate(groups):
                    if v * sub > jp:
                        beats = row >= sg
                    elif (v + 1) * sub - 1 <= jp:
                        beats = row > sg
                    else:
                        beats = (row > sg) | ((row == sg) & (j_loc > jp - v * sub))
                    ranks[v] = ranks[v] + jnp.where(beats, 1.0, 0.0)
            for v in range(n_grp):
                rank_ref[v * sub:(v + 1) * sub, :] = ranks[v]
    not_sel = jnp.where(rank_ref[...] < n_top, 0.0, 1.0)
    not_sel = jnp.concatenate([not_sel, jnp.zeros((LANES - n_sel, tq), F32)], axis=0).T.astype(BF16)

    q_aug = jnp.concatenate([q4, jnp.concatenate([not_sel] * hpg, axis=0)], axis=1)

    def sel_tile(kt, carry, bias):
        m, l, acc = carry
        k0 = pl.multiple_of(kt * tk, tk)
        sc = masked_scores(q_aug, ks_ref[pl.ds(k0, tk), :], bias)
        m_new = jnp.maximum(m, jnp.max(sc, axis=-1, keepdims=True))
        alpha = jnp.exp2(m - m_new)
        pe = jnp.exp2(sc - m_new)
        l = alpha * l + jnp.sum(pe, axis=-1, keepdims=True)
        acc = stack(alpha) * acc + _dot(stack(pe).astype(BF16), vs_ref[pl.ds(k0, tk), :])
        return m_new, l, acc

    last = (qi * tq + tq - 1) // tk
    init = (jnp.full((hpg, tq, 1), NEG_INF, F32), jnp.zeros((hpg, tq, 1), F32), jnp.zeros((hpg * tq, d), F32))
    carry = lax.fori_loop(0, last, lambda kt, c: sel_tile(kt, c, None), init)
    key_pos = last * tk + lax.broadcasted_iota(jnp.int32, (1, tk), 1)
    _, l_sel, acc_sel = sel_tile(last, carry, jnp.where(key_pos <= pos, 0.0, NEG_INF))
    o_sel = acc_sel / stack(l_sel)

    gt = jax.nn.sigmoid(gate_ref[...])
    lane = lax.broadcasted_iota(jnp.int32, (1, LANES), 1)
    for h in range(hpg):
        rows = slice(h * tq, (h + 1) * tq)
        o_h = jnp.zeros((tq, d), F32)
        for br, o_br in enumerate((o_cmp, o_sel, o_win)):
            g_col = jnp.sum(jnp.where(lane == gi * (3 * hpg) + h * 3 + br, gt, 0.0), axis=-1, keepdims=True)
            o_h = o_h + g_col * o_br[rows]
        o_ref[:, h * d:(h + 1) * d] = o_h.astype(BF16)


def selection_overlap(n_rows, n_sel):
    n_cmp = n_rows - 1
    c0 = np.arange(n_rows) * NSA_CMP_STRIDE
    s0 = np.arange(n_sel) * NSA_SEL_BLOCK
    ov = np.minimum(c0[:, None] + NSA_CMP_BLOCK, s0[None, :] + NSA_SEL_BLOCK) - np.maximum(c0[:, None], s0[None, :])
    ov = np.clip(ov, 0, None) / NSA_CMP_BLOCK
    ov[n_cmp:] = 0.0
    return jnp.asarray(ov, dtype=F32)


def nsa_attention(proj, kv_cmp, gate, q_gain, k_gains, tabs, tq=256, tk=512):
    b, s, _ = proj.shape
    g, d, hpg = NSA_KV_GROUPS, NSA_HEAD_DIM, NSA_HPG
    n_rows = s // NSA_CMP_STRIDE
    n_sel = s // NSA_SEL_BLOCK
    seg_blk = lambda i: NSA_HEADS + i * g
    cmp_spec = lambda kv: pl.BlockSpec((None, None, None, n_rows, d), lambda bi, gi, qi: (kv, bi, gi, 0, 0))
    seq_spec = lambda i: pl.BlockSpec((None, s, d), lambda bi, gi, qi: (bi, 0, seg_blk(i) + gi))
    const = lambda shape: pl.BlockSpec(shape, lambda bi, gi, qi: (0, 0))
    return pl.pallas_call(
        functools.partial(_nsa_attn_kernel, tq=tq, tk=tk, seq=s),
        grid=(b, g, s // tq),
        in_specs=[pl.BlockSpec((None, tq, hpg * d), lambda bi, gi, qi: (bi, qi, gi)),
                  cmp_spec(0), cmp_spec(1),
                  seq_spec(2), seq_spec(3), seq_spec(4), seq_spec(5),
                  pl.BlockSpec((None, tq, LANES), lambda bi, gi, qi: (bi, qi, 0)),
                  const((n_sel, n_rows)), const((1, d)), const((2, d)),
                  const((s, d)), const((s, d)), const((s, d))],
        out_specs=pl.BlockSpec((None, tq, hpg * d), lambda bi, gi, qi: (bi, qi, gi)),
        out_shape=jax.ShapeDtypeStruct((b, s, NSA_HEADS * d), BF16),
        scratch_shapes=[pltpu.VMEM((s, 2 * d), BF16)] + [pltpu.VMEM((s, d), BF16)] * 3
                       + [pltpu.VMEM((hpg * tq, d), BF16)] + [pltpu.VMEM((n_sel, tq), F32)] * 2,
        compiler_params=_params(("parallel", "parallel", "arbitrary")),
        name="nsa_attention",
    )(proj, kv_cmp, kv_cmp, proj, proj, proj, proj, gate, selection_overlap(n_rows, n_sel).T,
      q_gain.reshape(1, d), k_gains, *tabs)


def nsa_mixer(x2, b, s, gain, w_in, w_out, layer, q_norm, k_norm, cmp_pe, cmp_w1, cmp_w2):
    d, g = NSA_HEAD_DIM, NSA_KV_GROUPS
    n_main = (NSA_HEADS + 6 * g) * d
    w_gate = jnp.pad(w_in[layer, :, n_main:], ((0, 0), (0, LANES - 3 * NSA_HEADS)))
    proj, gate = norm_matmul(x2, gain, w_in, layer, n_main, w_gate)
    proj = proj.reshape(b, s, n_main)
    gate = gate.reshape(b, s, LANES)
    tabs = rope_tables(jnp.arange(s), d)
    cmp_end = jnp.arange(s // NSA_CMP_STRIDE) * NSA_CMP_STRIDE + (NSA_CMP_BLOCK - 1)
    kv_cmp = nsa_compress(proj, cmp_pe, cmp_w1, cmp_w2, k_norm[0], rope_tables(cmp_end, d))
    o = nsa_attention(proj, kv_cmp, gate, q_norm, k_norm[1:3], tabs)
    return matmul_residual(o.reshape(b * s, NSA_HEADS * d), w_out, layer, x2)


def _log_sigmoid(z):
    return jnp.minimum(z, 0.0) - jnp.log1p(jnp.exp(-jnp.abs(z)))


def _gla_kernel(q_ref, k_ref, v_ref, r_ref, glr_ref, wg_ref, bg_ref, on_ref, o_ref, state_ref, *, rows, hps):
    c = GLA_CHUNK
    sub = GLA_SUB
    dk, dv = GLA_KEY_DIM, GLA_VAL_DIM

    @pl.when(pl.program_id(2) == 0)
    def _():
        state_ref[...] = jnp.zeros_like(state_ref)

    tri = jnp.where(lax.broadcasted_iota(jnp.int32, (c, c), 0) >= lax.broadcasted_iota(jnp.int32, (c, c), 1), 1.0, 0.0)
    sub_row = lax.broadcasted_iota(jnp.int32, (sub, 1), 0)

    def head_chunk(hh, r0):
        kcols = slice(hh * dk, (hh + 1) * dk)
        vcols = slice(hh * dv, (hh + 1) * dv)
        qc = q_ref[pl.ds(r0, c), kcols] * (dk ** -0.5)
        kc = k_ref[pl.ds(r0, c), kcols]
        vc = v_ref[pl.ds(r0, c), vcols]
        z = _dot(glr_ref[pl.ds(r0, c), :], wg_ref[:, kcols], precision=lax.Precision.HIGHEST) + bg_ref[:, kcols]
        log_a = _log_sigmoid(z) / GLA_TAU
        cum = _dot(tri, log_a, precision=lax.Precision.HIGHEST)
        vb = vc.astype(BF16)

        state = state_ref[hh]
        o_inter = _dot((qc * jnp.exp(cum)).astype(BF16), state.astype(BF16), NT_DIMS)

        parts = []
        for bi in range(c // sub):
            lo = bi * sub
            q_i, k_i, v_i, c_i = qc[lo:lo + sub], kc[lo:lo + sub], vc[lo:lo + sub], cum[lo:lo + sub]
            o_i = o_inter[lo:lo + sub]
            if bi > 0:
                c0 = c_i[0:1]
                qs = (q_i * jnp.exp(c_i - c0)).astype(BF16)
                kp = (kc[:lo] * jnp.exp(c0 - cum[:lo])).astype(BF16)
                att = _dot(qs, kp, NT_DIMS)
                o_i = o_i + _dot(att.astype(BF16), vb[:lo])
            for jj in range(sub):
                e = jnp.exp(jnp.where(sub_row >= jj, c_i - c_i[jj:jj + 1], NEG_INF))
                col = jnp.sum(q_i * k_i[jj:jj + 1] * e, axis=-1, keepdims=True)
                o_i = o_i + col * v_i[jj:jj + 1]
            parts.append(o_i)
        o = jnp.concatenate(parts, axis=0)

        last = cum[c - 1:c]
        kd = (kc * jnp.exp(last - cum)).astype(BF16)
        state_ref[hh] = state * jnp.exp(last) + _dot(vb, kd, TN_DIMS)

        ms = jnp.mean(o * o, axis=-1, keepdims=True)
        on = o * lax.rsqrt(ms + NORM_EPS) * on_ref[...]
        r = r_ref[pl.ds(r0, c), vcols]
        o_ref[pl.ds(r0, c), vcols] = (on * (r * jax.nn.sigmoid(r))).astype(BF16)

    def chunk(ci, carry):
        r0 = pl.multiple_of(ci * c, c)
        for hh in range(hps):
            head_chunk(hh, r0)
        return carry

    lax.fori_loop(0, rows // c, chunk, 0)


def gla_attention(proj, glr, w_gate_up, b_gate, o_norm, rows=512, hps=4):
    b, s, _ = proj.shape
    nh, dk, dv = GLA_HEADS, GLA_KEY_DIM, GLA_VAL_DIM
    wg = jnp.pad(w_gate_up, ((0, LANES - GLA_GATE_RANK), (0, 0)))
    kw, vw = hps * dk, hps * dv
    k_blk = nh * dk // kw
    v_blk = 2 * nh * dk // vw
    r_blk = (2 * nh * dk + nh * dv) // vw
    return pl.pallas_call(
        functools.partial(_gla_kernel, rows=rows, hps=hps),
        grid=(b, nh // hps, s // rows),
        in_specs=[pl.BlockSpec((None, rows, kw), lambda bi, h, ci: (bi, ci, h)),
                  pl.BlockSpec((None, rows, kw), lambda bi, h, ci: (bi, ci, k_blk + h)),
                  pl.BlockSpec((None, rows, vw), lambda bi, h, ci: (bi, ci, v_blk + h)),
                  pl.BlockSpec((None, rows, vw), lambda bi, h, ci: (bi, ci, r_blk + h)),
                  pl.BlockSpec((None, rows, LANES), lambda bi, h, ci: (bi, ci, 0)),
                  pl.BlockSpec((LANES, kw), lambda bi, h, ci: (0, h)),
                  pl.BlockSpec((1, kw), lambda bi, h, ci: (0, h)),
                  pl.BlockSpec((1, dv), lambda bi, h, ci: (0, 0))],
        out_specs=pl.BlockSpec((None, rows, vw), lambda bi, h, ci: (bi, ci, h)),
        out_shape=jax.ShapeDtypeStruct((b, s, nh * dv), BF16),
        scratch_shapes=[pltpu.VMEM((hps, dv, dk), F32)],
        compiler_params=_params(("parallel", "parallel", "arbitrary")),
        name="gla_attention",
    )(proj, proj, proj, proj, glr, wg, b_gate.reshape(1, nh * dk), o_norm.reshape(1, dv))


def gla_mixer(x2, b, s, gain, w_in, w_gate_up, b_gate, o_norm, w_out, layer):
    nh, dk, dv = GLA_HEADS, GLA_KEY_DIM, GLA_VAL_DIM
    n_qkv = 2 * nh * dk + nh * dv
    w_r = w_in[layer, :, n_qkv + GLA_GATE_RANK:]
    w_glr = jnp.pad(w_in[layer, :, n_qkv:n_qkv + GLA_GATE_RANK], ((0, 0), (0, LANES - GLA_GATE_RANK)))
    proj, glr = norm_matmul(x2, gain, w_in, layer, n_qkv, w_glr, w_tail=w_r)
    o = gla_attention(proj.reshape(b, s, -1), glr.reshape(b, s, LANES), w_gate_up, b_gate, o_norm)
    return matmul_residual(o.reshape(b * s, nh * dv), w_out, layer, x2)


def _swa_heads(x, gain, tabs, seg, out_scale):
    blk = x.shape[0]
    d = SWA_HEAD_DIM
    n = x.shape[1] // LANES
    tiles = jnp.concatenate([x[:, t * LANES:(t + 1) * LANES] for t in range(n)], axis=0)
    sq = tiles * tiles
    hi = sq.astype(BF16)
    lo = (sq - hi.astype(F32)).astype(BF16)
    ms = (_dot(hi, seg) + _dot(lo, seg)) * (1.0 / d)
    heads = []
    for t in range(n):
        rows = slice(t * blk, (t + 1) * blk)
        y = _norm_rope(tiles[rows], gain, *tabs, d // ROPE_FRACTION // 2, ms[rows]) * out_scale
        heads += [y[:, u * d:(u + 1) * d] for u in range(LANES // d)]
    return heads


def _swa_kernel(q_ref, kp_ref, kc_ref, vp_ref, vc_ref, sink_ref, qg_ref, kg_ref, seg_ref,
                cc_ref, s1c_ref, s2c_ref, cp_ref, s1p_ref, s2p_ref, o_ref, *, blk):
    qi = pl.program_id(1)
    hpg, d, g = SWA_HPG, SWA_HEAD_DIM, SWA_KV_HEADS
    seg = seg_ref[...]
    tabs_c = (cc_ref[...], s1c_ref[...], s2c_ref[...])
    tabs_p = (cp_ref[...], s1p_ref[...], s2p_ref[...])
    q_heads = _swa_heads(q_ref[...], qg_ref[...], tabs_c, seg, d ** -0.5)
    kp_heads = _swa_heads(kp_ref[...], kg_ref[...], tabs_p, seg, 1.0)
    kc_heads = _swa_heads(kc_ref[...], kg_ref[...], tabs_c, seg, 1.0)
    vp, vc = vp_ref[...], vc_ref[...]

    qp = qi * blk + lax.broadcasted_iota(jnp.int32, (blk, 1), 0)
    kpos = (qi - 1) * blk + lax.broadcasted_iota(jnp.int32, (1, 2 * blk), 1)
    bias = jnp.where((kpos <= qp) & (kpos > qp - SWA_WINDOW) & (kpos >= 0), 0.0, NEG_INF)

    for gi in range(g):
        q = jnp.concatenate(q_heads[gi * hpg:(gi + 1) * hpg], axis=0).astype(BF16)
        k = jnp.concatenate([kp_heads[gi], kc_heads[gi]], axis=0).astype(BF16)
        v = jnp.concatenate([vp[:, gi * d:(gi + 1) * d], vc[:, gi * d:(gi + 1) * d]], axis=0).astype(BF16)
        s = _dot(q, k, NT_DIMS).reshape(hpg, blk, 2 * blk) + bias[None]
        sink = sink_ref[gi]
        m = jnp.maximum(jnp.max(s, axis=-1, keepdims=True), sink)
        p = jnp.exp(s - m)
        denom = jnp.sum(p, axis=-1, keepdims=True) + jnp.exp(sink - m)
        o = _dot(p.reshape(hpg * blk, 2 * blk).astype(BF16), v).reshape(hpg, blk, d) / denom
        o_ref[:, gi * hpg * d:(gi + 1) * hpg * d] = jnp.concatenate([o[h] for h in range(hpg)], axis=1).astype(BF16)


def swa_attention(proj, sinks, q_gain, k_gain, tabs, blk=128):
    b, s, _ = proj.shape
    g, hpg, d = SWA_KV_HEADS, SWA_HPG, SWA_HEAD_DIM
    nq, nkv = SWA_HEADS * d, g * d
    k_blk = nq // nkv
    prev = lambda qi: jnp.maximum(qi - 1, 0)
    kv_spec = lambda col, row: pl.BlockSpec((None, blk, nkv), lambda bi, qi: (bi, row(qi), col))
    const = lambda shape: pl.BlockSpec(shape, lambda bi, qi: (0,) * len(shape))
    tab_spec = lambda row: pl.BlockSpec((blk, LANES), lambda bi, qi: (row(qi), 0))
    cur = lambda qi: qi
    seg_id = np.arange(LANES) // d
    seg = jnp.asarray(seg_id[:, None] == seg_id[None, :], dtype=BF16)
    tile_gain = lambda gn: jnp.tile(gn.reshape(1, d), (1, LANES // d))
    return pl.pallas_call(
        functools.partial(_swa_kernel, blk=blk),
        grid=(b, s // blk),
        in_specs=[pl.BlockSpec((None, blk, nq), lambda bi, qi: (bi, qi, 0)),
                  kv_spec(k_blk, prev), kv_spec(k_blk, cur), kv_spec(k_blk + 1, prev), kv_spec(k_blk + 1, cur),
                  const((g, hpg, 1, 1)), const((1, LANES)), const((1, LANES)), const((LANES, LANES)),
                  tab_spec(cur), tab_spec(cur), tab_spec(cur), tab_spec(prev), tab_spec(prev), tab_spec(prev)],
        out_specs=pl.BlockSpec((None, blk, nq), lambda bi, qi: (bi, qi, 0)),
        out_shape=jax.ShapeDtypeStruct((b, s, nq), BF16),
        compiler_params=_params(("parallel", "parallel")),
        name="swa_attention",
    )(proj, proj, proj, proj, proj, sinks.astype(F32).reshape(g, hpg, 1, 1), tile_gain(q_gain), tile_gain(k_gain),
      seg, *tabs, *tabs)


def swa_mixer(x2, b, s, gain, w_in, w_out, layer, q_norm, k_norm, sinks):
    d, g = SWA_HEAD_DIM, SWA_KV_HEADS
    n_in = (SWA_HEADS + 2 * g) * d
    (proj,) = norm_matmul(x2, gain, w_in, layer, n_in)
    o = swa_attention(proj.reshape(b, s, n_in), sinks, q_norm, k_norm, rope_tables(jnp.arange(s), d))
    return matmul_residual(o.reshape(b * s, SWA_HEADS * d), w_out, layer, x2)


def kernel(x, norm_mix, norm_mlp, mlp_w_up, mlp_w_down, nsa_w_in, nsa_w_out, nsa_q_norm, nsa_k_norm, nsa_cmp_pe, nsa_cmp_w1, nsa_cmp_w2, gla_w_in, gla_w_gate_up, gla_b_gate, gla_o_norm, gla_w_out, swa_w_in, swa_w_out, swa_q_norm, swa_k_norm, swa_sinks):
    b, s, d = x.shape
    x2 = x.reshape(b * s, d)
    ia = ib = ic = 0
    for i in range(norm_mix.shape[0]):
        kind = i % N_MIXERS
        if kind == 0:
            x2 = nsa_mixer(x2, b, s, norm_mix[i], nsa_w_in, nsa_w_out, ia, nsa_q_norm[ia], nsa_k_norm[ia],
                           nsa_cmp_pe[ia], nsa_cmp_w1[ia], nsa_cmp_w2[ia])
            ia += 1
        elif kind == 1:
            x2 = gla_mixer(x2, b, s, norm_mix[i], gla_w_in, gla_w_gate_up[ib], gla_b_gate[ib],
                           gla_o_norm[ib], gla_w_out, ib)
            ib += 1
        else:
            x2 = swa_mixer(x2, b, s, norm_mix[i], swa_w_in, swa_w_out, ic, swa_q_norm[ic], swa_k_norm[ic],
                           swa_sinks[ic])
            ic += 1
        x2 = mlp_block(x2, norm_mlp[i], mlp_w_up, mlp_w_down, i)
    return x2.reshape(b, s, d)
```

```python
import functools

import numpy as np
import jax
import jax.numpy as jnp
from jax import lax
from jax.experimental import pallas as pl
from jax.experimental.pallas import tpu as pltpu

F32 = jnp.float32
BF16 = jnp.bfloat16

NORM_EPS = 1e-6
ROPE_THETA = 500000.0
ROPE_FRACTION = 4
NEG_INF = -1e30
N_MIXERS = 3

NSA_HEAD_DIM = 128
NSA_HEADS = 16
NSA_KV_GROUPS = 4
NSA_HPG = NSA_HEADS // NSA_KV_GROUPS
NSA_CMP_BLOCK = 32
NSA_CMP_STRIDE = 16
NSA_SEL_BLOCK = 64
NSA_TOP_N = 16
NSA_WINDOW = 512
NSA_FORCE_BONUS = 1e4
NSA_SCALE2 = float(NSA_HEAD_DIM ** -0.5 * np.log2(np.e))
NSA_MASK_NEG = -2.0 ** 100

GLA_HEADS = 4
GLA_KEY_DIM = 256
GLA_VAL_DIM = 512
GLA_GATE_RANK = 16
GLA_TAU = 16.0
GLA_CHUNK = 64
GLA_SUB = 16

SWA_HEAD_DIM = 64
SWA_HEADS = 32
SWA_KV_HEADS = 4
SWA_HPG = SWA_HEADS // SWA_KV_HEADS
SWA_WINDOW = 128

LANES = 128
VMEM_LIMIT = 56 * 1024 * 1024
MLP_VMEM_LIMIT = 62 * 1024 * 1024

NT_DIMS = (((1,), (1,)), ((), ()))
TN_DIMS = (((0,), (0,)), ((), ()))


def _params(sem, vmem_limit=VMEM_LIMIT):
    return pltpu.CompilerParams(dimension_semantics=sem, vmem_limit_bytes=vmem_limit)


def _dot(a, b, dims=None, precision=None):
    if dims is None:
        return jnp.dot(a, b, preferred_element_type=F32, precision=precision)
    return lax.dot_general(a, b, dims, preferred_element_type=F32, precision=precision)


def _rms_rows_to(h_scr, x_ref, g_ref, rows):
    n = x_ref.shape[0] // rows

    def body(i, c):
        r0 = pl.multiple_of(i * rows, rows)
        x = x_ref[pl.ds(r0, rows), :]
        ms = jnp.mean(x * x, axis=-1, keepdims=True)
        h_scr[pl.ds(r0, rows), :] = (x * lax.rsqrt(ms + NORM_EPS) * g_ref[...]).astype(BF16)
        return c

    lax.fori_loop(0, n, body, 0)


def _row_tile_copy(x_hbm, x_buf, sem, i):
    tm = x_buf.shape[0]
    return pltpu.make_async_copy(x_hbm.at[pl.ds(pl.multiple_of(i * tm, tm), tm), :], x_buf, sem)


def _fetch_and_norm(x_hbm, x_buf, sem, g_ref, h_scr, prefetch_step):
    i, j = pl.program_id(0), pl.program_id(1)

    @pl.when(j == 0)
    def _():
        @pl.when(i == 0)
        def _():
            _row_tile_copy(x_hbm, x_buf, sem, 0).start()

        _row_tile_copy(x_hbm, x_buf, sem, i).wait()
        _rms_rows_to(h_scr, x_buf, g_ref, 128)

    @pl.when((j == prefetch_step) & (i + 1 < pl.num_programs(0)))
    def _():
        _row_tile_copy(x_hbm, x_buf, sem, i + 1).start()


def _norm_matmul_kernel(x_hbm, g_ref, w_ref, *rest, has_extra, n_head_tiles):
    rest = list(rest)
    wt_ref = rest.pop(0) if n_head_tiles is not None else None
    if has_extra:
        wx_ref, o_ref, ox_ref, h_scr, x_buf, sem = rest
    else:
        o_ref, h_scr, x_buf, sem = rest
    j = pl.program_id(1)
    _fetch_and_norm(x_hbm, x_buf, sem, g_ref, h_scr, prefetch_step=1)

    if has_extra:
        @pl.when(j == 0)
        def _():
            ox_ref[...] = _dot(h_scr[...], wx_ref[...].astype(BF16))

    if wt_ref is None:
        o_ref[...] = _dot(h_scr[...], w_ref[...].astype(BF16))
    else:
        @pl.when(j < n_head_tiles)
        def _():
            o_ref[...] = _dot(h_scr[...], w_ref[...].astype(BF16))

        @pl.when(j >= n_head_tiles)
        def _():
            o_ref[...] = _dot(h_scr[...], wt_ref[...].astype(BF16))


def norm_matmul(x, gain, w, layer, n_cols, w_extra=None, w_tail=None, tm=2048, tn=512):
    m, d = x.shape
    tm = min(tm, m)
    has_extra = w_extra is not None
    n_head = n_cols // tn
    n_tail = 0 if w_tail is None else w_tail.shape[1] // tn
    in_specs = [pl.BlockSpec(memory_space=pl.ANY),
                pl.BlockSpec((1, d), lambda i, j: (0, 0)),
                pl.BlockSpec((None, d, tn), lambda i, j: (layer, 0, jnp.minimum(j, n_head - 1)))]
    out_shape = [jax.ShapeDtypeStruct((m, (n_head + n_tail) * tn), F32)]
    out_specs = [pl.BlockSpec((tm, tn), lambda i, j: (i, j))]
    args = [x, gain.reshape(1, d), w]
    if w_tail is not None:
        in_specs.append(pl.BlockSpec((d, tn), lambda i, j: (0, jnp.maximum(j - n_head, 0))))
        args.append(w_tail)
    if has_extra:
        in_specs.append(pl.BlockSpec((d, LANES), lambda i, j: (0, 0)))
        out_shape.append(jax.ShapeDtypeStruct((m, LANES), F32))
        out_specs.append(pl.BlockSpec((tm, LANES), lambda i, j: (i, 0)))
        args.append(w_extra)
    return pl.pallas_call(
        functools.partial(_norm_matmul_kernel, has_extra=has_extra,
                          n_head_tiles=None if w_tail is None else n_head),
        grid=(m // tm, n_head + n_tail),
        in_specs=in_specs, out_specs=out_specs, out_shape=out_shape,
        scratch_shapes=[pltpu.VMEM((tm, d), BF16), pltpu.VMEM((tm, d), F32), pltpu.SemaphoreType.DMA(())],
        compiler_params=_params(("arbitrary", "arbitrary")),
        name="norm_matmul",
    )(*args)


def _matmul_residual_kernel(a_ref, w_ref, r_ref, o_ref):
    o_ref[...] = r_ref[...] + _dot(a_ref[...], w_ref[...].astype(BF16))


def matmul_residual(a, w, layer, res, tm=2048, tn=512):
    m, k = a.shape
    tm = min(tm, m)
    n = w.shape[2]
    return pl.pallas_call(
        _matmul_residual_kernel,
        grid=(m // tm, n // tn),
        in_specs=[pl.BlockSpec((tm, k), lambda i, j: (i, 0)),
                  pl.BlockSpec((None, k, tn), lambda i, j: (layer, 0, j)),
                  pl.BlockSpec((tm, tn), lambda i, j: (i, j))],
        out_specs=pl.BlockSpec((tm, tn), lambda i, j: (i, j)),
        out_shape=jax.ShapeDtypeStruct((m, n), F32),
        compiler_params=_params(("parallel", "parallel")),
        name="matmul_residual",
    )(a, w, res)


def _mlp_kernel(x_hbm, g_ref, wu_ref, xt_ref, wd_ref, o_ref, h_scr, u_scr, x_buf, sem, *, nf, tf, splits):
    j = pl.program_id(1)
    d = h_scr.shape[1]
    per = nf // splits
    _fetch_and_norm(x_hbm, x_buf, sem, g_ref, h_scr, prefetch_step=nf)

    @pl.when(j < nf)
    def _():
        u = _dot(h_scr[:, 0:tf], wu_ref[0:tf, :].astype(BF16))
        for c in range(1, d // tf):
            u = u + _dot(h_scr[:, c * tf:(c + 1) * tf], wu_ref[c * tf:(c + 1) * tf, :].astype(BF16))
        u = jnp.maximum(u, 0.0)
        u_scr[j] = (u * u).astype(BF16)

    for part in range(splits):
        @pl.when((j >= nf) & ((j - nf) % splits == part))
        def _(part=part):
            acc = xt_ref[...] if part == 0 else o_ref[...]
            for f in range(per):
                acc = acc + _dot(u_scr[part * per + f], wd_ref[f * tf:(f + 1) * tf, :].astype(BF16))
            o_ref[...] = acc


def mlp_block(x, gain, w_up, w_down, layer, tm=1024, tf=512, tn=512, splits=2):
    m, d = x.shape
    ff = w_up.shape[2]
    nf, nn = ff // tf, d // tn
    down = lambda j: jnp.maximum(j - nf, 0)
    return pl.pallas_call(
        functools.partial(_mlp_kernel, nf=nf, tf=tf, splits=splits),
        grid=(m // tm, nf + nn * splits),
        in_specs=[pl.BlockSpec(memory_space=pl.ANY),
                  pl.BlockSpec((1, d), lambda i, j: (0, 0)),
                  pl.BlockSpec((None, d, tf), lambda i, j: (layer, 0, jnp.minimum(j, nf - 1))),
                  pl.BlockSpec((tm, tn), lambda i, j: (i, down(j) // splits)),
                  pl.BlockSpec((None, ff // splits, tn), lambda i, j: (layer, down(j) % splits, down(j) // splits))],
        out_specs=pl.BlockSpec((tm, tn), lambda i, j: (i, down(j) // splits)),
        out_shape=jax.ShapeDtypeStruct((m, d), F32),
        scratch_shapes=[pltpu.VMEM((tm, d), BF16), pltpu.VMEM((nf, tm, tf), BF16), pltpu.VMEM((tm, d), F32),
                        pltpu.SemaphoreType.DMA(())],
        compiler_params=_params(("arbitrary", "arbitrary"), MLP_VMEM_LIMIT),
        name="mlp_block",
    )(x, gain.reshape(1, d), w_up, x, w_down)


def rope_tables(pos, head_dim):
    rot = head_dim // ROPE_FRACTION
    half = rot // 2
    inv_freq = jnp.power(jnp.float32(ROPE_THETA), -jnp.arange(half, dtype=F32) / half)
    ang = pos.astype(F32)[:, None] * inv_freq[None, :]
    cos, sin = jnp.cos(ang), jnp.sin(ang)
    n = pos.shape[0]
    zeros = jnp.zeros((n, head_dim - rot), F32)
    c = jnp.concatenate([cos, cos, jnp.ones((n, head_dim - rot), F32)], axis=-1)
    s1 = jnp.concatenate([-sin, jnp.zeros((n, half), F32), zeros], axis=-1)
    s2 = jnp.concatenate([jnp.zeros((n, half), F32), sin, zeros], axis=-1)
    reps = LANES // head_dim
    return tuple(jnp.tile(t, (1, reps)) for t in (c, s1, s2))


def _norm_rope(x, gain, c, s1, s2, half, ms=None):
    if ms is None:
        ms = jnp.mean(x * x, axis=-1, keepdims=True)
    y = x * lax.rsqrt(ms + NORM_EPS) * gain
    return y * c + pltpu.roll(y, LANES - half, 1) * s1 + pltpu.roll(y, half, 1) * s2


def _gelu_tanh(x):
    return 0.5 * x * (1.0 + jnp.tanh(np.sqrt(2.0 / np.pi) * (x + 0.044715 * (x * x * x))))


def _nsa_compress_kernel(x_ref, pe_ref, w1_ref, w2_ref, g_ref, c_ref, s1_ref, s2_ref, o_ref):
    n_rows = x_ref.shape[0] // NSA_CMP_STRIDE
    d = NSA_HEAD_DIM
    y0 = jnp.zeros((n_rows, d), F32)
    y1 = jnp.zeros((n_rows, d), F32)
    for r in range(NSA_CMP_STRIDE):
        xr = x_ref[pl.ds(r, n_rows, stride=NSA_CMP_STRIDE), :]
        a0 = (xr + pe_ref[r:r + 1, :]).astype(BF16)
        a1 = (xr + pe_ref[NSA_CMP_STRIDE + r:NSA_CMP_STRIDE + r + 1, :]).astype(BF16)
        y0 = y0 + _dot(a0, w1_ref[r * d:(r + 1) * d, :].astype(BF16))
        y1 = y1 + _dot(a1, w1_ref[(NSA_CMP_STRIDE + r) * d:(NSA_CMP_STRIDE + r + 1) * d, :].astype(BF16))
    pre = y0 + pltpu.roll(y1, n_rows - 1, 0)
    out = _dot(_gelu_tanh(pre).astype(BF16), w2_ref[...].astype(BF16))
    roped = _norm_rope(out, g_ref[...], c_ref[...], s1_ref[...], s2_ref[...],
                       NSA_HEAD_DIM // ROPE_FRACTION // 2, None)
    o_ref[...] = jnp.where(pl.program_id(0) == 0, roped * NSA_SCALE2, out).astype(BF16)


def nsa_compress(proj, pe, w1, w2, k_gain, tables):
    b, s, _ = proj.shape
    g, d = NSA_KV_GROUPS, NSA_HEAD_DIM
    n_rows = s // NSA_CMP_STRIDE
    col_blk = NSA_HEADS
    tab_spec = pl.BlockSpec((n_rows, d), lambda kv, bi, gi: (0, 0))
    return pl.pallas_call(
        _nsa_compress_kernel,
        grid=(2, b, g),
        in_specs=[pl.BlockSpec((None, s, d), lambda kv, bi, gi: (bi, 0, col_blk + kv * g + gi)),
                  pl.BlockSpec((None, NSA_CMP_BLOCK, d), lambda kv, bi, gi: (kv, 0, 0)),
                  pl.BlockSpec((None, NSA_CMP_BLOCK * d, d), lambda kv, bi, gi: (kv, 0, 0)),
                  pl.BlockSpec((None, d, d), lambda kv, bi, gi: (kv, 0, 0)),
                  pl.BlockSpec((1, d), lambda kv, bi, gi: (0, 0)),
                  tab_spec, tab_spec, tab_spec],
        out_specs=pl.BlockSpec((None, None, None, n_rows, d), lambda kv, bi, gi: (kv, bi, gi, 0, 0)),
        out_shape=jax.ShapeDtypeStruct((2, b, g, n_rows, d), BF16),
        compiler_params=_params(("parallel", "parallel", "parallel")),
        name="nsa_compress",
    )(proj, pe, w1, w2, k_gain.reshape(1, d), *tables)


def _nsa_attn_kernel(q_ref, kc_ref, vc_ref, ksr_ref, vsr_ref, kwr_ref, vwr_ref, gate_ref, ovt_ref,
                     qg_ref, kg_ref, c_ref, s1_ref, s2_ref, o_ref, ks_ref, vs_ref, kw_ref, vw_ref, q_scr, score_ref, rank_ref,
                     *, tq, tk, seq):
    gi = pl.program_id(1)
    qi = pl.program_id(2)
    d = NSA_HEAD_DIM
    hpg = NSA_HPG
    n_sel = seq // NSA_SEL_BLOCK
    n_top = min(NSA_TOP_N, n_sel)
    half = d // ROPE_FRACTION // 2

    def tables(rows):
        return c_ref[rows, :], s1_ref[rows, :], s2_ref[rows, :]

    scale2 = NSA_SCALE2

    @pl.when(qi == 0)
    def _():
        chunk = 512
        lane = lax.broadcasted_iota(jnp.int32, (chunk, d), 1)
        row = lax.broadcasted_iota(jnp.int32, (chunk, d), 0)

        def body(i, carry):
            r0 = pl.multiple_of(i * chunk, chunk)
            rows = pl.ds(r0, chunk)
            ks = _norm_rope(ksr_ref[rows, :], kg_ref[0:1, :], *tables(rows), half, None) * scale2
            kw = _norm_rope(kwr_ref[rows, :], kg_ref[1:2, :], *tables(rows), half, None) * scale2
            ks_ref[rows, 0:d] = ks.astype(BF16)
            ks_ref[rows, d:2 * d] = jnp.where((r0 + row) // NSA_SEL_BLOCK == lane, NSA_MASK_NEG, 0.0).astype(BF16)
            kw_ref[rows, :] = kw.astype(BF16)
            vs_ref[rows, :] = vsr_ref[rows, :].astype(BF16)
            vw_ref[rows, :] = vwr_ref[rows, :].astype(BF16)
            return carry

        lax.fori_loop(0, seq // chunk, body, 0)

    q_rows = 128

    def q_body(i, carry):
        r0 = pl.multiple_of(i * q_rows, q_rows)
        tabs = tables(pl.ds(pl.multiple_of(qi * tq, tq) + r0, q_rows))
        for h in range(hpg):
            y = _norm_rope(q_ref[pl.ds(r0, q_rows), h * d:(h + 1) * d], qg_ref[...], *tabs, half, None)
            q_scr[pl.ds(h * tq + r0, q_rows), :] = y.astype(BF16)
        return carry

    lax.fori_loop(0, tq // q_rows, q_body, 0)
    q4 = q_scr[...]
    pos = qi * tq + lax.broadcasted_iota(jnp.int32, (tq, 1), 0)
    pos_l = qi * tq + lax.broadcasted_iota(jnp.int32, (1, tq), 1)

    def masked_scores(qs, k, bias=None):
        sc = _dot(qs, k, NT_DIMS)
        sc = sc.reshape(-1, tq, sc.shape[-1])
        return sc if bias is None else sc + bias[None]

    def stack(t):
        return t.reshape(t.shape[0] * tq, -1)

    span = NSA_WINDOW + tq
    w0 = pl.multiple_of(jnp.clip(qi * tq - NSA_WINDOW, 0, seq - span), tq)
    vw = vw_ref[pl.ds(w0, span), :]
    kp = w0 + lax.broadcasted_iota(jnp.int32, (1, span), 1)
    sw = masked_scores(q4, kw_ref[pl.ds(w0, span), :],
                       jnp.where((kp <= pos) & (kp > pos - NSA_WINDOW), 0.0, NEG_INF))
    pw = jnp.exp2(sw - jnp.max(sw, axis=-1, keepdims=True))
    o_win = _dot(stack(pw).astype(BF16), vw) / stack(jnp.sum(pw, axis=-1, keepdims=True))

    n_c = kc_ref.shape[0]
    cend = lax.broadcasted_iota(jnp.int32, (1, n_c), 1) * NSA_CMP_STRIDE + (NSA_CMP_BLOCK - 1)
    s = masked_scores(q4, kc_ref[...], jnp.where(cend <= pos, 0.0, NEG_INF))
    p = jnp.exp2(s - jnp.max(s, axis=-1, keepdims=True))
    p = p / jnp.sum(p, axis=-1, keepdims=True)
    p = jnp.where((pos >= NSA_CMP_BLOCK - 1)[None], p, 0.0)
    o_cmp = _dot(stack(p).astype(BF16), vc_ref[...])
    p_sum = jnp.sum(p, axis=0)
    imp_t = _dot(ovt_ref[...], p_sum, NT_DIMS, precision=lax.Precision.HIGHEST)

    j = lax.broadcasted_iota(jnp.int32, (n_sel, 1), 0)
    bq = pos_l // NSA_SEL_BLOCK
    forced = (j == 0) | (j == bq) | (j == bq - 1)
    score = jnp.where(j <= bq, imp_t + jnp.where(forced, NSA_FORCE_BONUS, 0.0), NEG_INF)
    sub = 8
    n_grp = n_sel // sub
    score_ref[...] = score
    rank_ref[...] = jnp.zeros_like(rank_ref)
    j_loc = lax.broadcasted_iota(jnp.int32, (sub, tq), 0)
    last_blk = (qi * tq + tq - 1) // NSA_SEL_BLOCK
    for gp in range(n_grp):
        @pl.when(gp * sub <= last_blk)
        def _(gp=gp):
            groups = [score_ref[v * sub:(v + 1) * sub, :] for v in range(n_grp)]
            ranks = [rank_ref[v * sub:(v + 1) * sub, :] for v in range(n_grp)]
            for jp in range(gp * sub, (gp + 1) * sub):
                row = score_ref[jp:jp + 1, :]
                for v, sg in enumerate(groups):
                    if v * sub > jp:
                        beats = row >= sg
                    elif (v + 1) * sub - 1 <= jp:
                        beats = row > sg
                    else:
                        beats = (row > sg) | ((row == sg) & (j_loc > jp - v * sub))
                    ranks[v] = ranks[v] + jnp.where(beats, 1.0, 0.0)
            for v in range(n_grp):
                rank_ref[v * sub:(v + 1) * sub, :] = ranks[v]
    not_sel = jnp.where(rank_ref[...] < n_top, 0.0, 1.0)
    not_sel = jnp.concatenate([not_sel, jnp.zeros((LANES - n_sel, tq), F32)], axis=0).T.astype(BF16)

    q_aug = jnp.concatenate([q4, jnp.concatenate([not_sel] * hpg, axis=0)], axis=1)

    def sel_tile(kt, carry, bias):
        m, l, acc = carry
        k0 = pl.multiple_of(kt * tk, tk)
        sc = masked_scores(q_aug, ks_ref[pl.ds(k0, tk), :], bias)
        m_new = jnp.maximum(m, jnp.max(sc, axis=-1, keepdims=True))
        alpha = jnp.exp2(m - m_new)
        pe = jnp.exp2(sc - m_new)
        l = alpha * l + jnp.sum(pe, axis=-1, keepdims=True)
        acc = stack(alpha) * acc + _dot(stack(pe).astype(BF16), vs_ref[pl.ds(k0, tk), :])
        return m_new, l, acc

    last = (qi * tq + tq - 1) // tk
    init = (jnp.full((hpg, tq, 1), NEG_INF, F32), jnp.zeros((hpg, tq, 1), F32), jnp.zeros((hpg * tq, d), F32))
    carry = lax.fori_loop(0, last, lambda kt, c: sel_tile(kt, c, None), init)
    key_pos = last * tk + lax.broadcasted_iota(jnp.int32, (1, tk), 1)
    _, l_sel, acc_sel = sel_tile(last, carry, jnp.where(key_pos <= pos, 0.0, NEG_INF))
    o_sel = acc_sel / stack(l_sel)

    gt = jax.nn.sigmoid(gate_ref[...])
    lane = lax.broadcasted_iota(jnp.int32, (1, LANES), 1)
    for h in range(hpg):
        rows = slice(h * tq, (h + 1) * tq)
        o_h = jnp.zeros((tq, d), F32)
        for br, o_br in enumerate((o_cmp, o_sel, o_win)):
            g_col = jnp.sum(jnp.where(lane == gi * (3 * hpg) + h * 3 + br, gt, 0.0), axis=-1, keepdims=True)
            o_h = o_h + g_col * o_br[rows]
        o_ref[:, h * d:(h + 1) * d] = o_h.astype(BF16)


def selection_overlap(n_rows, n_sel):
    n_cmp = n_rows - 1
    c0 = np.arange(n_rows) * NSA_CMP_STRIDE
    s0 = np.arange(n_sel) * NSA_SEL_BLOCK
    ov = np.minimum(c0[:, None] + NSA_CMP_BLOCK, s0[None, :] + NSA_SEL_BLOCK) - np.maximum(c0[:, None], s0[None, :])
    ov = np.clip(ov, 0, None) / NSA_CMP_BLOCK
    ov[n_cmp:] = 0.0
    return jnp.asarray(ov, dtype=F32)


def nsa_attention(proj, kv_cmp, gate, q_gain, k_gains, tabs, tq=256, tk=512):
    b, s, _ = proj.shape
    g, d, hpg = NSA_KV_GROUPS, NSA_HEAD_DIM, NSA_HPG
    n_rows = s // NSA_CMP_STRIDE
    n_sel = s // NSA_SEL_BLOCK
    seg_blk = lambda i: NSA_HEADS + i * g
    cmp_spec = lambda kv: pl.BlockSpec((None, None, None, n_rows, d), lambda bi, gi, qi: (kv, bi, gi, 0, 0))
    seq_spec = lambda i: pl.BlockSpec((None, s, d), lambda bi, gi, qi: (bi, 0, seg_blk(i) + gi))
    const = lambda shape: pl.BlockSpec(shape, lambda bi, gi, qi: (0, 0))
    return pl.pallas_call(
        functools.partial(_nsa_attn_kernel, tq=tq, tk=tk, seq=s),
        grid=(b, g, s // tq),
        in_specs=[pl.BlockSpec((None, tq, hpg * d), lambda bi, gi, qi: (bi, qi, gi)),
                  cmp_spec(0), cmp_spec(1),
                  seq_spec(2), seq_spec(3), seq_spec(4), seq_spec(5),
                  pl.BlockSpec((None, tq, LANES), lambda bi, gi, qi: (bi, qi, 0)),
                  const((n_sel, n_rows)), const((1, d)), const((2, d)),
                  const((s, d)), const((s, d)), const((s, d))],
        out_specs=pl.BlockSpec((None, tq, hpg * d), lambda bi, gi, qi: (bi, qi, gi)),
        out_shape=jax.ShapeDtypeStruct((b, s, NSA_HEADS * d), BF16),
        scratch_shapes=[pltpu.VMEM((s, 2 * d), BF16)] + [pltpu.VMEM((s, d), BF16)] * 3
                       + [pltpu.VMEM((hpg * tq, d), BF16)] + [pltpu.VMEM((n_sel, tq), F32)] * 2,
        compiler_params=_params(("parallel", "parallel", "arbitrary")),
        name="nsa_attention",
    )(proj, kv_cmp, kv_cmp, proj, proj, proj, proj, gate, selection_overlap(n_rows, n_sel).T,
      q_gain.reshape(1, d), k_gains, *tabs)


def nsa_mixer(x2, b, s, gain, w_in, w_out, layer, q_norm, k_norm, cmp_pe, cmp_w1, cmp_w2):
    d, g = NSA_HEAD_DIM, NSA_KV_GROUPS
    n_main = (NSA_HEADS + 6 * g) * d
    w_gate = jnp.pad(w_in[layer, :, n_main:], ((0, 0), (0, LANES - 3 * NSA_HEADS)))
    proj, gate = norm_matmul(x2, gain, w_in, layer, n_main, w_gate)
    proj = proj.reshape(b, s, n_main)
    gate = gate.reshape(b, s, LANES)
    tabs = rope_tables(jnp.arange(s), d)
    cmp_end = jnp.arange(s // NSA_CMP_STRIDE) * NSA_CMP_STRIDE + (NSA_CMP_BLOCK - 1)
    kv_cmp = nsa_compress(proj, cmp_pe, cmp_w1, cmp_w2, k_norm[0], rope_tables(cmp_end, d))
    o = nsa_attention(proj, kv_cmp, gate, q_norm, k_norm[1:3], tabs)
    return matmul_residual(o.reshape(b * s, NSA_HEADS * d), w_out, layer, x2)


def _log_sigmoid(z):
    return jnp.minimum(z, 0.0) - jnp.log1p(jnp.exp(-jnp.abs(z)))


def _gla_kernel(q_ref, k_ref, v_ref, r_ref, glr_ref, wg_ref, bg_ref, on_ref, o_ref, state_ref, *, rows, hps):
    c = GLA_CHUNK
    sub = GLA_SUB
    dk, dv = GLA_KEY_DIM, GLA_VAL_DIM

    @pl.when(pl.program_id(2) == 0)
    def _():
        state_ref[...] = jnp.zeros_like(state_ref)

    tri = jnp.where(lax.broadcasted_iota(jnp.int32, (c, c), 0) >= lax.broadcasted_iota(jnp.int32, (c, c), 1), 1.0, 0.0)
    sub_row = lax.broadcasted_iota(jnp.int32, (sub, 1), 0)

    def head_chunk(hh, r0):
        kcols = slice(hh * dk, (hh + 1) * dk)
        vcols = slice(hh * dv, (hh + 1) * dv)
        qc = q_ref[pl.ds(r0, c), kcols] * (dk ** -0.5)
        kc = k_ref[pl.ds(r0, c), kcols]
        vc = v_ref[pl.ds(r0, c), vcols]
        z = _dot(glr_ref[pl.ds(r0, c), :], wg_ref[:, kcols], precision=lax.Precision.HIGHEST) + bg_ref[:, kcols]
        log_a = _log_sigmoid(z) / GLA_TAU
        cum = _dot(tri, log_a, precision=lax.Precision.HIGHEST)
        vb = vc.astype(BF16)

        state = state_ref[hh]
        o_inter = _dot((qc * jnp.exp(cum)).astype(BF16), state.astype(BF16), NT_DIMS)

        parts = []
        for bi in range(c // sub):
            lo = bi * sub
            q_i, k_i, v_i, c_i = qc[lo:lo + sub], kc[lo:lo + sub], vc[lo:lo + sub], cum[lo:lo + sub]
            o_i = o_inter[lo:lo + sub]
            if bi > 0:
                c0 = c_i[0:1]
                qs = (q_i * jnp.exp(c_i - c0)).astype(BF16)
                kp = (kc[:lo] * jnp.exp(c0 - cum[:lo])).astype(BF16)
                att = _dot(qs, kp, NT_DIMS)
                o_i = o_i + _dot(att.astype(BF16), vb[:lo])
            for jj in range(sub):
                e = jnp.exp(jnp.where(sub_row >= jj, c_i - c_i[jj:jj + 1], NEG_INF))
                col = jnp.sum(q_i * k_i[jj:jj + 1] * e, axis=-1, keepdims=True)
                o_i = o_i + col * v_i[jj:jj + 1]
            parts.append(o_i)
        o = jnp.concatenate(parts, axis=0)

        last = cum[c - 1:c]
        kd = (kc * jnp.exp(last - cum)).astype(BF16)
        state_ref[hh] = state * jnp.exp(last) + _dot(vb, kd, TN_DIMS)

        ms = jnp.mean(o * o, axis=-1, keepdims=True)
        on = o * lax.rsqrt(ms + NORM_EPS) * on_ref[...]
        r = r_ref[pl.ds(r0, c), vcols]
        o_ref[pl.ds(r0, c), vcols] = (on * (r * jax.nn.sigmoid(r))).astype(BF16)

    def chunk(ci, carry):
        r0 = pl.multiple_of(ci * c, c)
        for hh in range(hps):
            head_chunk(hh, r0)
        return carry

    lax.fori_loop(0, rows // c, chunk, 0)


def gla_attention(proj, glr, w_gate_up, b_gate, o_norm, rows=512, hps=4):
    b, s, _ = proj.shape
    nh, dk, dv = GLA_HEADS, GLA_KEY_DIM, GLA_VAL_DIM
    wg = jnp.pad(w_gate_up, ((0, LANES - GLA_GATE_RANK), (0, 0)))
    kw, vw = hps * dk, hps * dv
    k_blk = nh * dk // kw
    v_blk = 2 * nh * dk // vw
    r_blk = (2 * nh * dk + nh * dv) // vw
    return pl.pallas_call(
        functools.partial(_gla_kernel, rows=rows, hps=hps),
        grid=(b, nh // hps, s // rows),
        in_specs=[pl.BlockSpec((None, rows, kw), lambda bi, h, ci: (bi, ci, h)),
                  pl.BlockSpec((None, rows, kw), lambda bi, h, ci: (bi, ci, k_blk + h)),
                  pl.BlockSpec((None, rows, vw), lambda bi, h, ci: (bi, ci, v_blk + h)),
                  pl.BlockSpec((None, rows, vw), lambda bi, h, ci: (bi, ci, r_blk + h)),
                  pl.BlockSpec((None, rows, LANES), lambda bi, h, ci: (bi, ci, 0)),
                  pl.BlockSpec((LANES, kw), lambda bi, h, ci: (0, h)),
                  pl.BlockSpec((1, kw), lambda bi, h, ci: (0, h)),
                  pl.BlockSpec((1, dv), lambda bi, h, ci: (0, 0))],
        out_specs=pl.BlockSpec((None, rows, vw), lambda bi, h, ci: (bi, ci, h)),
        out_shape=jax.ShapeDtypeStruct((b, s, nh * dv), BF16),
        scratch_shapes=[pltpu.VMEM((hps, dv, dk), F32)],
        compiler_params=_params(("parallel", "parallel", "arbitrary")),
        name="gla_attention",
    )(proj, proj, proj, proj, glr, wg, b_gate.reshape(1, nh * dk), o_norm.reshape(1, dv))


def gla_mixer(x2, b, s, gain, w_in, w_gate_up, b_gate, o_norm, w_out, layer):
    nh, dk, dv = GLA_HEADS, GLA_KEY_DIM, GLA_VAL_DIM
    n_qkv = 2 * nh * dk + nh * dv
    w_r = w_in[layer, :, n_qkv + GLA_GATE_RANK:]
    w_glr = jnp.pad(w_in[layer, :, n_qkv:n_qkv + GLA_GATE_RANK], ((0, 0), (0, LANES - GLA_GATE_RANK)))
    proj, glr = norm_matmul(x2, gain, w_in, layer, n_qkv, w_glr, w_tail=w_r)
    o = gla_attention(proj.reshape(b, s, -1), glr.reshape(b, s, LANES), w_gate_up, b_gate, o_norm)
    return matmul_residual(o.reshape(b * s, nh * dv), w_out, layer, x2)


def _swa_heads(x, gain, tabs, seg, out_scale):
    blk = x.shape[0]
    d = SWA_HEAD_DIM
    n = x.shape[1] // LANES
    tiles = jnp.concatenate([x[:, t * LANES:(t + 1) * LANES] for t in range(n)], axis=0)
    sq = tiles * tiles
    hi = sq.astype(BF16)
    lo = (sq - hi.astype(F32)).astype(BF16)
    ms = (_dot(hi, seg) + _dot(lo, seg)) * (1.0 / d)
    heads = []
    for t in range(n):
        rows = slice(t * blk, (t + 1) * blk)
        y = _norm_rope(tiles[rows], gain, *tabs, d // ROPE_FRACTION // 2, ms[rows]) * out_scale
        heads += [y[:, u * d:(u + 1) * d] for u in range(LANES // d)]
    return heads


def _swa_kernel(q_ref, kp_ref, kc_ref, vp_ref, vc_ref, sink_ref, qg_ref, kg_ref, seg_ref,
                cc_ref, s1c_ref, s2c_ref, cp_ref, s1p_ref, s2p_ref, o_ref, *, blk):
    qi = pl.program_id(1)
    hpg, d, g = SWA_HPG, SWA_HEAD_DIM, SWA_KV_HEADS
    seg = seg_ref[...]
    tabs_c = (cc_ref[...], s1c_ref[...], s2c_ref[...])
    tabs_p = (cp_ref[...], s1p_ref[...], s2p_ref[...])
    q_heads = _swa_heads(q_ref[...], qg_ref[...], tabs_c, seg, d ** -0.5)
    kp_heads = _swa_heads(kp_ref[...], kg_ref[...], tabs_p, seg, 1.0)
    kc_heads = _swa_heads(kc_ref[...], kg_ref[...], tabs_c, seg, 1.0)
    vp, vc = vp_ref[...], vc_ref[...]

    qp = qi * blk + lax.broadcasted_iota(jnp.int32, (blk, 1), 0)
    kpos = (qi - 1) * blk + lax.broadcasted_iota(jnp.int32, (1, 2 * blk), 1)
    bias = jnp.where((kpos <= qp) & (kpos > qp - SWA_WINDOW) & (kpos >= 0), 0.0, NEG_INF)

    for gi in range(g):
        q = jnp.concatenate(q_heads[gi * hpg:(gi + 1) * hpg], axis=0).astype(BF16)
        k = jnp.concatenate([kp_heads[gi], kc_heads[gi]], axis=0).astype(BF16)
        v = jnp.concatenate([vp[:, gi * d:(gi + 1) * d], vc[:, gi * d:(gi + 1) * d]], axis=0).astype(BF16)
        s = _dot(q, k, NT_DIMS).reshape(hpg, blk, 2 * blk) + bias[None]
        sink = sink_ref[gi]
        m = jnp.maximum(jnp.max(s, axis=-1, keepdims=True), sink)
        p = jnp.exp(s - m)
        denom = jnp.sum(p, axis=-1, keepdims=True) + jnp.exp(sink - m)
        o = _dot(p.reshape(hpg * blk, 2 * blk).astype(BF16), v).reshape(hpg, blk, d) / denom
        o_ref[:, gi * hpg * d:(gi + 1) * hpg * d] = jnp.concatenate([o[h] for h in range(hpg)], axis=1).astype(BF16)


def swa_attention(proj, sinks, q_gain, k_gain, tabs, blk=128):
    b, s, _ = proj.shape
    g, hpg, d = SWA_KV_HEADS, SWA_HPG, SWA_HEAD_DIM
    nq, nkv = SWA_HEADS * d, g * d
    k_blk = nq // nkv
    prev = lambda qi: jnp.maximum(qi - 1, 0)
    kv_spec = lambda col, row: pl.BlockSpec((None, blk, nkv), lambda bi, qi: (bi, row(qi), col))
    const = lambda shape: pl.BlockSpec(shape, lambda bi, qi: (0,) * len(shape))
    tab_spec = lambda row: pl.BlockSpec((blk, LANES), lambda bi, qi: (row(qi), 0))
    cur = lambda qi: qi
    seg_id = np.arange(LANES) // d
    seg = jnp.asarray(seg_id[:, None] == seg_id[None, :], dtype=BF16)
    tile_gain = lambda gn: jnp.tile(gn.reshape(1, d), (1, LANES // d))
    return pl.pallas_call(
        functools.partial(_swa_kernel, blk=blk),
        grid=(b, s // blk),
        in_specs=[pl.BlockSpec((None, blk, nq), lambda bi, qi: (bi, qi, 0)),
                  kv_spec(k_blk, prev), kv_spec(k_blk, cur), kv_spec(k_blk + 1, prev), kv_spec(k_blk + 1, cur),
                  const((g, hpg, 1, 1)), const((1, LANES)), const((1, LANES)), const((LANES, LANES)),
                  tab_spec(cur), tab_spec(cur), tab_spec(cur), tab_spec(prev), tab_spec(prev), tab_spec(prev)],
        out_specs=pl.BlockSpec((None, blk, nq), lambda bi, qi: (bi, qi, 0)),
        out_shape=jax.ShapeDtypeStruct((b, s, nq), BF16),
        compiler_params=_params(("parallel", "parallel")),
        name="swa_attention",
    )(proj, proj, proj, proj, proj, sinks.astype(F32).reshape(g, hpg, 1, 1), tile_gain(q_gain), tile_gain(k_gain),
      seg, *tabs, *tabs)


def swa_mixer(x2, b, s, gain, w_in, w_out, layer, q_norm, k_norm, sinks):
    d, g = SWA_HEAD_DIM, SWA_KV_HEADS
    n_in = (SWA_HEADS + 2 * g) * d
    (proj,) = norm_matmul(x2, gain, w_in, layer, n_in)
    o = swa_attention(proj.reshape(b, s, n_in), sinks, q_norm, k_norm, rope_tables(jnp.arange(s), d))
    return matmul_residual(o.reshape(b * s, SWA_HEADS * d), w_out, layer, x2)


def kernel(x, norm_mix, norm_mlp, mlp_w_up, mlp_w_down, nsa_w_in, nsa_w_out, nsa_q_norm, nsa_k_norm, nsa_cmp_pe, nsa_cmp_w1, nsa_cmp_w2, gla_w_in, gla_w_gate_up, gla_b_gate, gla_o_norm, gla_w_out, swa_w_in, swa_w_out, swa_q_norm, swa_k_norm, swa_sinks):
    b, s, d = x.shape
    x2 = x.reshape(b * s, d)
    ia = ib = ic = 0
    for i in range(norm_mix.shape[0]):
        kind = i % N_MIXERS
        if kind == 0:
            x2 = nsa_mixer(x2, b, s, norm_mix[i], nsa_w_in, nsa_w_out, ia, nsa_q_norm[ia], nsa_k_norm[ia],
                           nsa_cmp_pe[ia], nsa_cmp_w1[ia], nsa_cmp_w2[ia])
            ia += 1
        elif kind == 1:
            x2 = gla_mixer(x2, b, s, norm_mix[i], gla_w_in, gla_w_gate_up[ib], gla_b_gate[ib],
                           gla_o_norm[ib], gla_w_out, ib)
            ib += 1
        else:
            x2 = swa_mixer(x2, b, s, norm_mix[i], swa_w_in, swa_w_out, ic, swa_q_norm[ic], swa_k_norm[ic],
                           swa_sinks[ic])
            ic += 1
        x2 = mlp_block(x2, norm_mlp[i], mlp_w_up, mlp_w_down, i)
    return x2.reshape(b, s, d)
```

```python
import functools

import numpy as np
import jax
import jax.numpy as jnp
from jax import lax
from jax.experimental import pallas as pl
from jax.experimental.pallas import tpu as pltpu

F32 = jnp.float32
BF16 = jnp.bfloat16

NORM_EPS = 1e-6
ROPE_THETA = 500000.0
ROPE_FRACTION = 4
NEG_INF = -1e30
N_MIXERS = 3

NSA_HEAD_DIM = 128
NSA_HEADS = 16
NSA_KV_GROUPS = 4
NSA_HPG = NSA_HEADS // NSA_KV_GROUPS
NSA_CMP_BLOCK = 32
NSA_CMP_STRIDE = 16
NSA_SEL_BLOCK = 64
NSA_TOP_N = 16
NSA_WINDOW = 512
NSA_FORCE_BONUS = 1e4
NSA_SCALE2 = float(NSA_HEAD_DIM ** -0.5 * np.log2(np.e))
NSA_MASK_NEG = -2.0 ** 100

GLA_HEADS = 4
GLA_KEY_DIM = 256
GLA_VAL_DIM = 512
GLA_GATE_RANK = 16
GLA_TAU = 16.0
GLA_CHUNK = 64
GLA_SUB = 16

SWA_HEAD_DIM = 64
SWA_HEADS = 32
SWA_KV_HEADS = 4
SWA_HPG = SWA_HEADS // SWA_KV_HEADS
SWA_WINDOW = 128

LANES = 128
VMEM_LIMIT = 56 * 1024 * 1024
MLP_VMEM_LIMIT = 127 * 512 * 1024

NT_DIMS = (((1,), (1,)), ((), ()))
TN_DIMS = (((0,), (0,)), ((), ()))


def _params(sem, vmem_limit=VMEM_LIMIT):
    return pltpu.CompilerParams(dimension_semantics=sem, vmem_limit_bytes=vmem_limit)


def _dot(a, b, dims=None, precision=None):
    if dims is None:
        return jnp.dot(a, b, preferred_element_type=F32, precision=precision)
    return lax.dot_general(a, b, dims, preferred_element_type=F32, precision=precision)


def _rms_rows_to(h_scr, x_ref, g_ref, rows):
    n = x_ref.shape[0] // rows

    def body(i, c):
        r0 = pl.multiple_of(i * rows, rows)
        x = x_ref[pl.ds(r0, rows), :]
        ms = jnp.mean(x * x, axis=-1, keepdims=True)
        h_scr[pl.ds(r0, rows), :] = (x * lax.rsqrt(ms + NORM_EPS) * g_ref[...]).astype(BF16)
        return c

    lax.fori_loop(0, n, body, 0)


def _row_tile_copy(x_hbm, x_buf, sem, i):
    tm = x_buf.shape[0]
    return pltpu.make_async_copy(x_hbm.at[pl.ds(pl.multiple_of(i * tm, tm), tm), :], x_buf, sem)


def _fetch_and_norm(x_hbm, x_buf, sem, g_ref, h_scr, prefetch_step):
    i, j = pl.program_id(0), pl.program_id(1)

    @pl.when(j == 0)
    def _():
        @pl.when(i == 0)
        def _():
            _row_tile_copy(x_hbm, x_buf, sem, 0).start()

        _row_tile_copy(x_hbm, x_buf, sem, i).wait()
        _rms_rows_to(h_scr, x_buf, g_ref, 128)

    @pl.when((j == prefetch_step) & (i + 1 < pl.num_programs(0)))
    def _():
        _row_tile_copy(x_hbm, x_buf, sem, i + 1).start()


def _norm_matmul_kernel(x_hbm, g_ref, w_ref, *rest, has_extra, n_head_tiles):
    rest = list(rest)
    wt_ref = rest.pop(0) if n_head_tiles is not None else None
    if has_extra:
        wx_ref, o_ref, ox_ref, h_scr, x_buf, sem = rest
    else:
        o_ref, h_scr, x_buf, sem = rest
    j = pl.program_id(1)
    _fetch_and_norm(x_hbm, x_buf, sem, g_ref, h_scr, prefetch_step=1)

    if has_extra:
        @pl.when(j == 0)
        def _():
            ox_ref[...] = _dot(h_scr[...], wx_ref[...].astype(BF16))

    if wt_ref is None:
        o_ref[...] = _dot(h_scr[...], w_ref[...].astype(BF16))
    else:
        @pl.when(j < n_head_tiles)
        def _():
            o_ref[...] = _dot(h_scr[...], w_ref[...].astype(BF16))

        @pl.when(j >= n_head_tiles)
        def _():
            o_ref[...] = _dot(h_scr[...], wt_ref[...].astype(BF16))


def norm_matmul(x, gain, w, layer, n_cols, w_extra=None, w_tail=None, tm=2048, tn=512):
    m, d = x.shape
    tm = min(tm, m)
    has_extra = w_extra is not None
    n_head = n_cols // tn
    n_tail = 0 if w_tail is None else w_tail.shape[1] // tn
    in_specs = [pl.BlockSpec(memory_space=pl.ANY),
                pl.BlockSpec((1, d), lambda i, j: (0, 0)),
                pl.BlockSpec((None, d, tn), lambda i, j: (layer, 0, jnp.minimum(j, n_head - 1)))]
    out_shape = [jax.ShapeDtypeStruct((m, (n_head + n_tail) * tn), F32)]
    out_specs = [pl.BlockSpec((tm, tn), lambda i, j: (i, j))]
    args = [x, gain.reshape(1, d), w]
    if w_tail is not None:
        in_specs.append(pl.BlockSpec((d, tn), lambda i, j: (0, jnp.maximum(j - n_head, 0))))
        args.append(w_tail)
    if has_extra:
        in_specs.append(pl.BlockSpec((d, LANES), lambda i, j: (0, 0)))
        out_shape.append(jax.ShapeDtypeStruct((m, LANES), F32))
        out_specs.append(pl.BlockSpec((tm, LANES), lambda i, j: (i, 0)))
        args.append(w_extra)
    return pl.pallas_call(
        functools.partial(_norm_matmul_kernel, has_extra=has_extra,
                          n_head_tiles=None if w_tail is None else n_head),
        grid=(m // tm, n_head + n_tail),
        in_specs=in_specs, out_specs=out_specs, out_shape=out_shape,
        scratch_shapes=[pltpu.VMEM((tm, d), BF16), pltpu.VMEM((tm, d), F32), pltpu.SemaphoreType.DMA(())],
        compiler_params=_params(("arbitrary", "arbitrary")),
        name="norm_matmul",
    )(*args)


def _matmul_residual_kernel(a_ref, w_ref, r_ref, o_ref):
    o_ref[...] = r_ref[...] + _dot(a_ref[...], w_ref[...].astype(BF16))


def matmul_residual(a, w, layer, res, tm=2048, tn=512):
    m, k = a.shape
    tm = min(tm, m)
    n = w.shape[2]
    return pl.pallas_call(
        _matmul_residual_kernel,
        grid=(m // tm, n // tn),
        in_specs=[pl.BlockSpec((tm, k), lambda i, j: (i, 0)),
                  pl.BlockSpec((None, k, tn), lambda i, j: (layer, 0, j)),
                  pl.BlockSpec((tm, tn), lambda i, j: (i, j))],
        out_specs=pl.BlockSpec((tm, tn), lambda i, j: (i, j)),
        out_shape=jax.ShapeDtypeStruct((m, n), F32),
        compiler_params=_params(("parallel", "parallel")),
        name="matmul_residual",
    )(a, w, res)


def _mlp_kernel(x_hbm, g_ref, wu_ref, xt_ref, wd_ref, o_ref, h_scr, u_scr, x_buf, sem, *, nf, tf, kc):
    i, j = pl.program_id(0), pl.program_id(1)
    tm, d = h_scr.shape
    n_chunks = pl.num_programs(1) - nf
    rows = x_buf.shape[1]
    nxt = jnp.minimum(i + 1, pl.num_programs(0) - 1)

    def chunk_copy(tile, s):
        r0 = pl.multiple_of(tile * tm + s * rows, rows)
        return pltpu.make_async_copy(x_hbm.at[pl.ds(r0, rows), :], x_buf.at[s % 2], sem.at[s % 2])

    def norm_chunk(s):
        x = x_buf[s % 2]
        ms = jnp.mean(x * x, axis=-1, keepdims=True)
        h_scr[pl.ds(pl.multiple_of(s * rows, rows), rows), :] = (x * lax.rsqrt(ms + NORM_EPS) * g_ref[...]).astype(BF16)

    @pl.when((j == 0) & (i == 0))
    def _():
        def first_tile(s, carry):
            chunk_copy(0, s).start()
            chunk_copy(0, s).wait()
            norm_chunk(s)
            return carry

        lax.fori_loop(0, n_chunks, first_tile, 0)

    @pl.when(j == nf - 1)
    def _():
        chunk_copy(nxt, 0).start()

    @pl.when(j < nf)
    def _():
        u = _dot(h_scr[:, 0:kc], wu_ref[0:kc, :].astype(BF16))
        for c in range(1, d // kc):
            u = u + _dot(h_scr[:, c * kc:(c + 1) * kc], wu_ref[c * kc:(c + 1) * kc, :].astype(BF16))
        u = jnp.maximum(u, 0.0)
        u_scr[j] = (u * u).astype(BF16)

    @pl.when(j >= nf)
    def _():
        s = j - nf
        chunk_copy(nxt, s).wait()

        @pl.when(s + 1 < n_chunks)
        def _():
            chunk_copy(nxt, s + 1).start()

        norm_chunk(s)
        acc = xt_ref[...]
        for f in range(nf):
            for c in range(tf // kc):
                piece = slice(f * tf + c * kc, f * tf + (c + 1) * kc)
                acc = acc + _dot(u_scr[f, :, c * kc:(c + 1) * kc], wd_ref[piece, :].astype(BF16))
        o_ref[...] = acc


def mlp_block(x, gain, w_up, w_down, layer, tm=1024, tf=1024, tn=256, kc=512):
    m, d = x.shape
    ff = w_up.shape[2]
    nf, nn = ff // tf, d // tn
    assert tm % nn == 0
    down = lambda j: jnp.maximum(j - nf, 0)
    return pl.pallas_call(
        functools.partial(_mlp_kernel, nf=nf, tf=tf, kc=kc),
        grid=(m // tm, nf + nn),
        in_specs=[pl.BlockSpec(memory_space=pl.ANY),
                  pl.BlockSpec((1, d), lambda i, j: (0, 0)),
                  pl.BlockSpec((None, d, tf), lambda i, j: (layer, 0, jnp.minimum(j, nf - 1))),
                  pl.BlockSpec((tm, tn), lambda i, j: (i, down(j))),
                  pl.BlockSpec((None, ff, tn), lambda i, j: (layer, 0, down(j)))],
        out_specs=pl.BlockSpec((tm, tn), lambda i, j: (i, down(j))),
        out_shape=jax.ShapeDtypeStruct((m, d), F32),
        scratch_shapes=[pltpu.VMEM((tm, d), BF16), pltpu.VMEM((nf, tm, tf), BF16), pltpu.VMEM((2, tm // nn, d), F32),
                        pltpu.SemaphoreType.DMA((2,))],
        compiler_params=_params(("arbitrary", "arbitrary"), MLP_VMEM_LIMIT),
        name="mlp_block",
    )(x, gain.reshape(1, d), w_up, x, w_down)


def rope_tables(pos, head_dim):
    rot = head_dim // ROPE_FRACTION
    half = rot // 2
    inv_freq = jnp.power(jnp.float32(ROPE_THETA), -jnp.arange(half, dtype=F32) / half)
    ang = pos.astype(F32)[:, None] * inv_freq[None, :]
    cos, sin = jnp.cos(ang), jnp.sin(ang)
    n = pos.shape[0]
    zeros = jnp.zeros((n, head_dim - rot), F32)
    c = jnp.concatenate([cos, cos, jnp.ones((n, head_dim - rot), F32)], axis=-1)
    s1 = jnp.concatenate([-sin, jnp.zeros((n, half), F32), zeros], axis=-1)
    s2 = jnp.concatenate([jnp.zeros((n, half), F32), sin, zeros], axis=-1)
    reps = LANES // head_dim
    return tuple(jnp.tile(t, (1, reps)) for t in (c, s1, s2))


def _norm_rope(x, gain, c, s1, s2, half, ms=None):
    if ms is None:
        ms = jnp.mean(x * x, axis=-1, keepdims=True)
    y = x * lax.rsqrt(ms + NORM_EPS) * gain
    return y * c + pltpu.roll(y, LANES - half, 1) * s1 + pltpu.roll(y, half, 1) * s2


def _gelu_tanh(x):
    return 0.5 * x * (1.0 + jnp.tanh(np.sqrt(2.0 / np.pi) * (x + 0.044715 * (x * x * x))))


def _nsa_compress_kernel(x_ref, pe_ref, w1_ref, w2_ref, g_ref, c_ref, s1_ref, s2_ref, o_ref):
    n_rows = x_ref.shape[0] // NSA_CMP_STRIDE
    d = NSA_HEAD_DIM
    y0 = jnp.zeros((n_rows, d), F32)
    y1 = jnp.zeros((n_rows, d), F32)
    for r in range(NSA_CMP_STRIDE):
        xr = x_ref[pl.ds(r, n_rows, stride=NSA_CMP_STRIDE), :]
        a0 = (xr + pe_ref[r:r + 1, :]).astype(BF16)
        a1 = (xr + pe_ref[NSA_CMP_STRIDE + r:NSA_CMP_STRIDE + r + 1, :]).astype(BF16)
        y0 = y0 + _dot(a0, w1_ref[r * d:(r + 1) * d, :].astype(BF16))
        y1 = y1 + _dot(a1, w1_ref[(NSA_CMP_STRIDE + r) * d:(NSA_CMP_STRIDE + r + 1) * d, :].astype(BF16))
    pre = y0 + pltpu.roll(y1, n_rows - 1, 0)
    out = _dot(_gelu_tanh(pre).astype(BF16), w2_ref[...].astype(BF16))
    roped = _norm_rope(out, g_ref[...], c_ref[...], s1_ref[...], s2_ref[...],
                       NSA_HEAD_DIM // ROPE_FRACTION // 2, None)
    o_ref[...] = jnp.where(pl.program_id(0) == 0, roped * NSA_SCALE2, out).astype(BF16)


def nsa_compress(proj, pe, w1, w2, k_gain, tables):
    b, s, _ = proj.shape
    g, d = NSA_KV_GROUPS, NSA_HEAD_DIM
    n_rows = s // NSA_CMP_STRIDE
    col_blk = NSA_HEADS
    tab_spec = pl.BlockSpec((n_rows, d), lambda kv, bi, gi: (0, 0))
    return pl.pallas_call(
        _nsa_compress_kernel,
        grid=(2, b, g),
        in_specs=[pl.BlockSpec((None, s, d), lambda kv, bi, gi: (bi, 0, col_blk + kv * g + gi)),
                  pl.BlockSpec((None, NSA_CMP_BLOCK, d), lambda kv, bi, gi: (kv, 0, 0)),
                  pl.BlockSpec((None, NSA_CMP_BLOCK * d, d), lambda kv, bi, gi: (kv, 0, 0)),
                  pl.BlockSpec((None, d, d), lambda kv, bi, gi: (kv, 0, 0)),
                  pl.BlockSpec((1, d), lambda kv, bi, gi: (0, 0)),
                  tab_spec, tab_spec, tab_spec],
        out_specs=pl.BlockSpec((None, None, None, n_rows, d), lambda kv, bi, gi: (kv, bi, gi, 0, 0)),
        out_shape=jax.ShapeDtypeStruct((2, b, g, n_rows, d), BF16),
        compiler_params=_params(("parallel", "parallel", "parallel")),
        name="nsa_compress",
    )(proj, pe, w1, w2, k_gain.reshape(1, d), *tables)


def _nsa_attn_kernel(q_ref, kc_ref, vc_ref, ksr_ref, vsr_ref, kwr_ref, vwr_ref, gate_ref, ovt_ref,
                     qg_ref, kg_ref, c_ref, s1_ref, s2_ref, o_ref, ks_ref, vs_ref, kw_ref, vw_ref, q_scr, score_ref, rank_ref,
                     *, tq, tk, seq):
    gi = pl.program_id(1)
    qi = pl.program_id(2)
    d = NSA_HEAD_DIM
    hpg = NSA_HPG
    n_sel = seq // NSA_SEL_BLOCK
    n_top = min(NSA_TOP_N, n_sel)
    half = d // ROPE_FRACTION // 2

    def tables(rows):
        return c_ref[rows, :], s1_ref[rows, :], s2_ref[rows, :]

    scale2 = NSA_SCALE2

    @pl.when(qi == 0)
    def _():
        chunk = 512
        lane = lax.broadcasted_iota(jnp.int32, (chunk, d), 1)
        row = lax.broadcasted_iota(jnp.int32, (chunk, d), 0)

        def body(i, carry):
            r0 = pl.multiple_of(i * chunk, chunk)
            rows = pl.ds(r0, chunk)
            ks = _norm_rope(ksr_ref[rows, :], kg_ref[0:1, :], *tables(rows), half, None) * scale2
            kw = _norm_rope(kwr_ref[rows, :], kg_ref[1:2, :], *tables(rows), half, None) * scale2
            ks_ref[rows, 0:d] = ks.astype(BF16)
            ks_ref[rows, d:2 * d] = jnp.where((r0 + row) // NSA_SEL_BLOCK == lane, NSA_MASK_NEG, 0.0).astype(BF16)
            kw_ref[rows, :] = kw.astype(BF16)
            vs_ref[rows, :] = vsr_ref[rows, :].astype(BF16)
            vw_ref[rows, :] = vwr_ref[rows, :].astype(BF16)
            return carry

        lax.fori_loop(0, seq // chunk, body, 0)

    q_rows = 128

    def q_body(i, carry):
        r0 = pl.multiple_of(i * q_rows, q_rows)
        tabs = tables(pl.ds(pl.multiple_of(qi * tq, tq) + r0, q_rows))
        for h in range(hpg):
            y = _norm_rope(q_ref[pl.ds(r0, q_rows), h * d:(h + 1) * d], qg_ref[...], *tabs, half, None)
            q_scr[pl.ds(h * tq + r0, q_rows), :] = y.astype(BF16)
        return carry

    lax.fori_loop(0, tq // q_rows, q_body, 0)
    q4 = q_scr[...]
    pos = qi * tq + lax.broadcasted_iota(jnp.int32, (tq, 1), 0)
    pos_l = qi * tq + lax.broadcasted_iota(jnp.int32, (1, tq), 1)

    def masked_scores(qs, k, bias=None):
        sc = _dot(qs, k, NT_DIMS)
        sc = sc.reshape(-1, tq, sc.shape[-1])
        return sc if bias is None else sc + bias[None]

    def stack(t):
        return t.reshape(t.shape[0] * tq, -1)

    span = NSA_WINDOW + tq
    w0 = pl.multiple_of(jnp.clip(qi * tq - NSA_WINDOW, 0, seq - span), tq)
    vw = vw_ref[pl.ds(w0, span), :]
    kp = w0 + lax.broadcasted_iota(jnp.int32, (1, span), 1)
    sw = masked_scores(q4, kw_ref[pl.ds(w0, span), :],
                       jnp.where((kp <= pos) & (kp > pos - NSA_WINDOW), 0.0, NEG_INF))
    pw = jnp.exp2(sw - jnp.max(sw, axis=-1, keepdims=True))
    o_win = _dot(stack(pw).astype(BF16), vw) / stack(jnp.sum(pw, axis=-1, keepdims=True))

    n_c = kc_ref.shape[0]
    cend = lax.broadcasted_iota(jnp.int32, (1, n_c), 1) * NSA_CMP_STRIDE + (NSA_CMP_BLOCK - 1)
    s = masked_scores(q4, kc_ref[...], jnp.where(cend <= pos, 0.0, NEG_INF))
    p = jnp.exp2(s - jnp.max(s, axis=-1, keepdims=True))
    p = p / jnp.sum(p, axis=-1, keepdims=True)
    p = jnp.where((pos >= NSA_CMP_BLOCK - 1)[None], p, 0.0)
    o_cmp = _dot(stack(p).astype(BF16), vc_ref[...])
    p_sum = jnp.sum(p, axis=0)
    imp_t = _dot(ovt_ref[...], p_sum, NT_DIMS, precision=lax.Precision.HIGHEST)

    j = lax.broadcasted_iota(jnp.int32, (n_sel, 1), 0)
    bq = pos_l // NSA_SEL_BLOCK
    forced = (j == 0) | (j == bq) | (j == bq - 1)
    score = jnp.where(j <= bq, imp_t + jnp.where(forced, NSA_FORCE_BONUS, 0.0), NEG_INF)
    sub = 8
    n_grp = n_sel // sub
    score_ref[...] = score
    rank_ref[...] = jnp.zeros_like(rank_ref)
    j_loc = lax.broadcasted_iota(jnp.int32, (sub, tq), 0)
    last_blk = (qi * tq + tq - 1) // NSA_SEL_BLOCK
    for gp in range(n_grp):
        @pl.when(gp * sub <= last_blk)
        def _(gp=gp):
            groups = [score_ref[v * sub:(v + 1) * sub, :] for v in range(n_grp)]
            ranks = [rank_ref[v * sub:(v + 1) * sub, :] for v in range(n_grp)]
            for jp in range(gp * sub, (gp + 1) * sub):
                row = score_ref[jp:jp + 1, :]
                for v, sg in enumerate(groups):
                    if v * sub > jp:
                        beats = row >= sg
                    elif (v + 1) * sub - 1 <= jp:
                        beats = row > sg
                    else:
                        beats = (row > sg) | ((row == sg) & (j_loc > jp - v * sub))
                    ranks[v] = ranks[v] + jnp.where(beats, 1.0, 0.0)
            for v in range(n_grp):
                rank_ref[v * sub:(v + 1) * sub, :] = ranks[v]
    not_sel = jnp.where(rank_ref[...] < n_top, 0.0, 1.0)
    not_sel = jnp.concatenate([not_sel, jnp.zeros((LANES - n_sel, tq), F32)], axis=0).T.astype(BF16)

    q_aug = jnp.concatenate([q4, jnp.concatenate([not_sel] * hpg, axis=0)], axis=1)

    def sel_tile(kt, carry, bias):
        m, l, acc = carry
        k0 = pl.multiple_of(kt * tk, tk)
        sc = masked_scores(q_aug, ks_ref[pl.ds(k0, tk), :], bias)
        m_new = jnp.maximum(m, jnp.max(sc, axis=-1, keepdims=True))
        alpha = jnp.exp2(m - m_new)
        pe = jnp.exp2(sc - m_new)
        l = alpha * l + jnp.sum(pe, axis=-1, keepdims=True)
        acc = stack(alpha) * acc + _dot(stack(pe).astype(BF16), vs_ref[pl.ds(k0, tk), :])
        return m_new, l, acc

    last = (qi * tq + tq - 1) // tk
    init = (jnp.full((hpg, tq, 1), NEG_INF, F32), jnp.zeros((hpg, tq, 1), F32), jnp.zeros((hpg * tq, d), F32))
    carry = lax.fori_loop(0, last, lambda kt, c: sel_tile(kt, c, None), init)
    key_pos = last * tk + lax.broadcasted_iota(jnp.int32, (1, tk), 1)
    _, l_sel, acc_sel = sel_tile(last, carry, jnp.where(key_pos <= pos, 0.0, NEG_INF))
    o_sel = acc_sel / stack(l_sel)

    gt = jax.nn.sigmoid(gate_ref[...])
    lane = lax.broadcasted_iota(jnp.int32, (1, LANES), 1)
    for h in range(hpg):
        rows = slice(h * tq, (h + 1) * tq)
        o_h = jnp.zeros((tq, d), F32)
        for br, o_br in enumerate((o_cmp, o_sel, o_win)):
            g_col = jnp.sum(jnp.where(lane == gi * (3 * hpg) + h * 3 + br, gt, 0.0), axis=-1, keepdims=True)
            o_h = o_h + g_col * o_br[rows]
        o_ref[:, h * d:(h + 1) * d] = o_h.astype(BF16)


def selection_overlap(n_rows, n_sel):
    n_cmp = n_rows - 1
    c0 = np.arange(n_rows) * NSA_CMP_STRIDE
    s0 = np.arange(n_sel) * NSA_SEL_BLOCK
    ov = np.minimum(c0[:, None] + NSA_CMP_BLOCK, s0[None, :] + NSA_SEL_BLOCK) - np.maximum(c0[:, None], s0[None, :])
    ov = np.clip(ov, 0, None) / NSA_CMP_BLOCK
    ov[n_cmp:] = 0.0
    return jnp.asarray(ov, dtype=F32)


def nsa_attention(proj, kv_cmp, gate, q_gain, k_gains, tabs, tq=256, tk=512):
    b, s, _ = proj.shape
    g, d, hpg = NSA_KV_GROUPS, NSA_HEAD_DIM, NSA_HPG
    n_rows = s // NSA_CMP_STRIDE
    n_sel = s // NSA_SEL_BLOCK
    seg_blk = lambda i: NSA_HEADS + i * g
    cmp_spec = lambda kv: pl.BlockSpec((None, None, None, n_rows, d), lambda bi, gi, qi: (kv, bi, gi, 0, 0))
    seq_spec = lambda i: pl.BlockSpec((None, s, d), lambda bi, gi, qi: (bi, 0, seg_blk(i) + gi))
    const = lambda shape: pl.BlockSpec(shape, lambda bi, gi, qi: (0, 0))
    return pl.pallas_call(
        functools.partial(_nsa_attn_kernel, tq=tq, tk=tk, seq=s),
        grid=(b, g, s // tq),
        in_specs=[pl.BlockSpec((None, tq, hpg * d), lambda bi, gi, qi: (bi, qi, gi)),
                  cmp_spec(0), cmp_spec(1),
                  seq_spec(2), seq_spec(3), seq_spec(4), seq_spec(5),
                  pl.BlockSpec((None, tq, LANES), lambda bi, gi, qi: (bi, qi, 0)),
                  const((n_sel, n_rows)), const((1, d)), const((2, d)),
                  const((s, d)), const((s, d)), const((s, d))],
        out_specs=pl.BlockSpec((None, tq, hpg * d), lambda bi, gi, qi: (bi, qi, gi)),
        out_shape=jax.ShapeDtypeStruct((b, s, NSA_HEADS * d), BF16),
        scratch_shapes=[pltpu.VMEM((s, 2 * d), BF16)] + [pltpu.VMEM((s, d), BF16)] * 3
                       + [pltpu.VMEM((hpg * tq, d), BF16)] + [pltpu.VMEM((n_sel, tq), F32)] * 2,
        compiler_params=_params(("parallel", "parallel", "arbitrary")),
        name="nsa_attention",
    )(proj, kv_cmp, kv_cmp, proj, proj, proj, proj, gate, selection_overlap(n_rows, n_sel).T,
      q_gain.reshape(1, d), k_gains, *tabs)


def nsa_mixer(x2, b, s, gain, w_in, w_out, layer, q_norm, k_norm, cmp_pe, cmp_w1, cmp_w2):
    d, g = NSA_HEAD_DIM, NSA_KV_GROUPS
    n_main = (NSA_HEADS + 6 * g) * d
    w_gate = jnp.pad(w_in[layer, :, n_main:], ((0, 0), (0, LANES - 3 * NSA_HEADS)))
    proj, gate = norm_matmul(x2, gain, w_in, layer, n_main, w_gate)
    proj = proj.reshape(b, s, n_main)
    gate = gate.reshape(b, s, LANES)
    tabs = rope_tables(jnp.arange(s), d)
    cmp_end = jnp.arange(s // NSA_CMP_STRIDE) * NSA_CMP_STRIDE + (NSA_CMP_BLOCK - 1)
    kv_cmp = nsa_compress(proj, cmp_pe, cmp_w1, cmp_w2, k_norm[0], rope_tables(cmp_end, d))
    o = nsa_attention(proj, kv_cmp, gate, q_norm, k_norm[1:3], tabs)
    return matmul_residual(o.reshape(b * s, NSA_HEADS * d), w_out, layer, x2)


def _log_sigmoid(z):
    return jnp.minimum(z, 0.0) - jnp.log1p(jnp.exp(-jnp.abs(z)))


def _gla_kernel(q_ref, k_ref, v_ref, r_ref, glr_ref, wg_ref, bg_ref, on_ref, o_ref, state_ref, *, rows, hps):
    c = GLA_CHUNK
    sub = GLA_SUB
    dk, dv = GLA_KEY_DIM, GLA_VAL_DIM

    @pl.when(pl.program_id(2) == 0)
    def _():
        state_ref[...] = jnp.zeros_like(state_ref)

    tri = jnp.where(lax.broadcasted_iota(jnp.int32, (c, c), 0) >= lax.broadcasted_iota(jnp.int32, (c, c), 1), 1.0, 0.0)
    sub_row = lax.broadcasted_iota(jnp.int32, (sub, 1), 0)

    def head_chunk(hh, r0):
        kcols = slice(hh * dk, (hh + 1) * dk)
        vcols = slice(hh * dv, (hh + 1) * dv)
        qc = q_ref[pl.ds(r0, c), kcols] * (dk ** -0.5)
        kc = k_ref[pl.ds(r0, c), kcols]
        vc = v_ref[pl.ds(r0, c), vcols]
        z = _dot(glr_ref[pl.ds(r0, c), :], wg_ref[:, kcols], precision=lax.Precision.HIGHEST) + bg_ref[:, kcols]
        log_a = _log_sigmoid(z) / GLA_TAU
        cum = _dot(tri, log_a, precision=lax.Precision.HIGHEST)
        vb = vc.astype(BF16)

        state = state_ref[hh]
        o_inter = _dot((qc * jnp.exp(cum)).astype(BF16), state.astype(BF16), NT_DIMS)

        parts = []
        for bi in range(c // sub):
            lo = bi * sub
            q_i, k_i, v_i, c_i = qc[lo:lo + sub], kc[lo:lo + sub], vc[lo:lo + sub], cum[lo:lo + sub]
            o_i = o_inter[lo:lo + sub]
            if bi > 0:
                c0 = c_i[0:1]
                qs = (q_i * jnp.exp(c_i - c0)).astype(BF16)
                kp = (kc[:lo] * jnp.exp(c0 - cum[:lo])).astype(BF16)
                att = _dot(qs, kp, NT_DIMS)
                o_i = o_i + _dot(att.astype(BF16), vb[:lo])
            for jj in range(sub):
                e = jnp.exp(jnp.where(sub_row >= jj, c_i - c_i[jj:jj + 1], NEG_INF))
                col = jnp.sum(q_i * k_i[jj:jj + 1] * e, axis=-1, keepdims=True)
                o_i = o_i + col * v_i[jj:jj + 1]
            parts.append(o_i)
        o = jnp.concatenate(parts, axis=0)

        last = cum[c - 1:c]
        kd = (kc * jnp.exp(last - cum)).astype(BF16)
        state_ref[hh] = state * jnp.exp(last) + _dot(vb, kd, TN_DIMS)

        ms = jnp.mean(o * o, axis=-1, keepdims=True)
        on = o * lax.rsqrt(ms + NORM_EPS) * on_ref[...]
        r = r_ref[pl.ds(r0, c), vcols]
        o_ref[pl.ds(r0, c), vcols] = (on * (r * jax.nn.sigmoid(r))).astype(BF16)

    def chunk(ci, carry):
        r0 = pl.multiple_of(ci * c, c)
        for hh in range(hps):
            head_chunk(hh, r0)
        return carry

    lax.fori_loop(0, rows // c, chunk, 0)


def gla_attention(proj, glr, w_gate_up, b_gate, o_norm, rows=512, hps=4):
    b, s, _ = proj.shape
    nh, dk, dv = GLA_HEADS, GLA_KEY_DIM, GLA_VAL_DIM
    wg = jnp.pad(w_gate_up, ((0, LANES - GLA_GATE_RANK), (0, 0)))
    kw, vw = hps * dk, hps * dv
    k_blk = nh * dk // kw
    v_blk = 2 * nh * dk // vw
    r_blk = (2 * nh * dk + nh * dv) // vw
    return pl.pallas_call(
        functools.partial(_gla_kernel, rows=rows, hps=hps),
        grid=(b, nh // hps, s // rows),
        in_specs=[pl.BlockSpec((None, rows, kw), lambda bi, h, ci: (bi, ci, h)),
                  pl.BlockSpec((None, rows, kw), lambda bi, h, ci: (bi, ci, k_blk + h)),
                  pl.BlockSpec((None, rows, vw), lambda bi, h, ci: (bi, ci, v_blk + h)),
                  pl.BlockSpec((None, rows, vw), lambda bi, h, ci: (bi, ci, r_blk + h)),
                  pl.BlockSpec((None, rows, LANES), lambda bi, h, ci: (bi, ci, 0)),
                  pl.BlockSpec((LANES, kw), lambda bi, h, ci: (0, h)),
                  pl.BlockSpec((1, kw), lambda bi, h, ci: (0, h)),
                  pl.BlockSpec((1, dv), lambda bi, h, ci: (0, 0))],
        out_specs=pl.BlockSpec((None, rows, vw), lambda bi, h, ci: (bi, ci, h)),
        out_shape=jax.ShapeDtypeStruct((b, s, nh * dv), BF16),
        scratch_shapes=[pltpu.VMEM((hps, dv, dk), F32)],
        compiler_params=_params(("parallel", "parallel", "arbitrary")),
        name="gla_attention",
    )(proj, proj, proj, proj, glr, wg, b_gate.reshape(1, nh * dk), o_norm.reshape(1, dv))


def gla_mixer(x2, b, s, gain, w_in, w_gate_up, b_gate, o_norm, w_out, layer):
    nh, dk, dv = GLA_HEADS, GLA_KEY_DIM, GLA_VAL_DIM
    n_qkv = 2 * nh * dk + nh * dv
    w_r = w_in[layer, :, n_qkv + GLA_GATE_RANK:]
    w_glr = jnp.pad(w_in[layer, :, n_qkv:n_qkv + GLA_GATE_RANK], ((0, 0), (0, LANES - GLA_GATE_RANK)))
    proj, glr = norm_matmul(x2, gain, w_in, layer, n_qkv, w_glr, w_tail=w_r)
    o = gla_attention(proj.reshape(b, s, -1), glr.reshape(b, s, LANES), w_gate_up, b_gate, o_norm)
    return matmul_residual(o.reshape(b * s, nh * dv), w_out, layer, x2)


def _swa_heads(x, gain, tabs, seg, out_scale):
    blk = x.shape[0]
    d = SWA_HEAD_DIM
    n = x.shape[1] // LANES
    tiles = jnp.concatenate([x[:, t * LANES:(t + 1) * LANES] for t in range(n)], axis=0)
    sq = tiles * tiles
    hi = sq.astype(BF16)
    lo = (sq - hi.astype(F32)).astype(BF16)
    ms = (_dot(hi, seg) + _dot(lo, seg)) * (1.0 / d)
    heads = []
    for t in range(n):
        rows = slice(t * blk, (t + 1) * blk)
        y = _norm_rope(tiles[rows], gain, *tabs, d // ROPE_FRACTION // 2, ms[rows]) * out_scale
        heads += [y[:, u * d:(u + 1) * d] for u in range(LANES // d)]
    return heads


def _swa_kernel(q_ref, kp_ref, kc_ref, vp_ref, vc_ref, sink_ref, qg_ref, kg_ref, seg_ref,
                cc_ref, s1c_ref, s2c_ref, cp_ref, s1p_ref, s2p_ref, o_ref, *, blk):
    qi = pl.program_id(1)
    hpg, d, g = SWA_HPG, SWA_HEAD_DIM, SWA_KV_HEADS
    seg = seg_ref[...]
    tabs_c = (cc_ref[...], s1c_ref[...], s2c_ref[...])
    tabs_p = (cp_ref[...], s1p_ref[...], s2p_ref[...])
    q_heads = _swa_heads(q_ref[...], qg_ref[...], tabs_c, seg, d ** -0.5)
    kp_heads = _swa_heads(kp_ref[...], kg_ref[...], tabs_p, seg, 1.0)
    kc_heads = _swa_heads(kc_ref[...], kg_ref[...], tabs_c, seg, 1.0)
    vp, vc = vp_ref[...], vc_ref[...]

    qp = qi * blk + lax.broadcasted_iota(jnp.int32, (blk, 1), 0)
    kpos = (qi - 1) * blk + lax.broadcasted_iota(jnp.int32, (1, 2 * blk), 1)
    bias = jnp.where((kpos <= qp) & (kpos > qp - SWA_WINDOW) & (kpos >= 0), 0.0, NEG_INF)

    for gi in range(g):
        q = jnp.concatenate(q_heads[gi * hpg:(gi + 1) * hpg], axis=0).astype(BF16)
        k = jnp.concatenate([kp_heads[gi], kc_heads[gi]], axis=0).astype(BF16)
        v = jnp.concatenate([vp[:, gi * d:(gi + 1) * d], vc[:, gi * d:(gi + 1) * d]], axis=0).astype(BF16)
        s = _dot(q, k, NT_DIMS).reshape(hpg, blk, 2 * blk) + bias[None]
        sink = sink_ref[gi]
        m = jnp.maximum(jnp.max(s, axis=-1, keepdims=True), sink)
        p = jnp.exp(s - m)
        denom = jnp.sum(p, axis=-1, keepdims=True) + jnp.exp(sink - m)
        o = _dot(p.reshape(hpg * blk, 2 * blk).astype(BF16), v).reshape(hpg, blk, d) / denom
        o_ref[:, gi * hpg * d:(gi + 1) * hpg * d] = jnp.concatenate([o[h] for h in range(hpg)], axis=1).astype(BF16)


def swa_attention(proj, sinks, q_gain, k_gain, tabs, blk=128):
    b, s, _ = proj.shape
    g, hpg, d = SWA_KV_HEADS, SWA_HPG, SWA_HEAD_DIM
    nq, nkv = SWA_HEADS * d, g * d
    k_blk = nq // nkv
    prev = lambda qi: jnp.maximum(qi - 1, 0)
    kv_spec = lambda col, row: pl.BlockSpec((None, blk, nkv), lambda bi, qi: (bi, row(qi), col))
    const = lambda shape: pl.BlockSpec(shape, lambda bi, qi: (0,) * len(shape))
    tab_spec = lambda row: pl.BlockSpec((blk, LANES), lambda bi, qi: (row(qi), 0))
    cur = lambda qi: qi
    seg_id = np.arange(LANES) // d
    seg = jnp.asarray(seg_id[:, None] == seg_id[None, :], dtype=BF16)
    tile_gain = lambda gn: jnp.tile(gn.reshape(1, d), (1, LANES // d))
    return pl.pallas_call(
        functools.partial(_swa_kernel, blk=blk),
        grid=(b, s // blk),
        in_specs=[pl.BlockSpec((None, blk, nq), lambda bi, qi: (bi, qi, 0)),
                  kv_spec(k_blk, prev), kv_spec(k_blk, cur), kv_spec(k_blk + 1, prev), kv_spec(k_blk + 1, cur),
                  const((g, hpg, 1, 1)), const((1, LANES)), const((1, LANES)), const((LANES, LANES)),
                  tab_spec(cur), tab_spec(cur), tab_spec(cur), tab_spec(prev), tab_spec(prev), tab_spec(prev)],
        out_specs=pl.BlockSpec((None, blk, nq), lambda bi, qi: (bi, qi, 0)),
        out_shape=jax.ShapeDtypeStruct((b, s, nq), BF16),
        compiler_params=_params(("parallel", "parallel")),
        name="swa_attention",
    )(proj, proj, proj, proj, proj, sinks.astype(F32).reshape(g, hpg, 1, 1), tile_gain(q_gain), tile_gain(k_gain),
      seg, *tabs, *tabs)


def swa_mixer(x2, b, s, gain, w_in, w_out, layer, q_norm, k_norm, sinks):
    d, g = SWA_HEAD_DIM, SWA_KV_HEADS
    n_in = (SWA_HEADS + 2 * g) * d
    (proj,) = norm_matmul(x2, gain, w_in, layer, n_in)
    o = swa_attention(proj.reshape(b, s, n_in), sinks, q_norm, k_norm, rope_tables(jnp.arange(s), d))
    return matmul_residual(o.reshape(b * s, SWA_HEADS * d), w_out, layer, x2)


def kernel(x, norm_mix, norm_mlp, mlp_w_up, mlp_w_down, nsa_w_in, nsa_w_out, nsa_q_norm, nsa_k_norm, nsa_cmp_pe, nsa_cmp_w1, nsa_cmp_w2, gla_w_in, gla_w_gate_up, gla_b_gate, gla_o_norm, gla_w_out, swa_w_in, swa_w_out, swa_q_norm, swa_k_norm, swa_sinks):
    b, s, d = x.shape
    x2 = x.reshape(b * s, d)
    ia = ib = ic = 0
    for i in range(norm_mix.shape[0]):
        kind = i % N_MIXERS
        if kind == 0:
            x2 = nsa_mixer(x2, b, s, norm_mix[i], nsa_w_in, nsa_w_out, ia, nsa_q_norm[ia], nsa_k_norm[ia],
                           nsa_cmp_pe[ia], nsa_cmp_w1[ia], nsa_cmp_w2[ia])
            ia += 1
        elif kind == 1:
            x2 = gla_mixer(x2, b, s, norm_mix[i], gla_w_in, gla_w_gate_up[ib], gla_b_gate[ib],
                           gla_o_norm[ib], gla_w_out, ib)
            ib += 1
        else:
            x2 = swa_mixer(x2, b, s, norm_mix[i], swa_w_in, swa_w_out, ic, swa_q_norm[ic], swa_k_norm[ic],
                           swa_sinks[ic])
            ic += 1
        x2 = mlp_block(x2, norm_mlp[i], mlp_w_up, mlp_w_down, i)
    return x2.reshape(b, s, d)
```

```python
import functools

import numpy as np
import jax
import jax.numpy as jnp
from jax import lax
from jax.experimental import pallas as pl
from jax.experimental.pallas import tpu as pltpu

F32 = jnp.float32
BF16 = jnp.bfloat16

NORM_EPS = 1e-6
ROPE_THETA = 500000.0
ROPE_FRACTION = 4
NEG_INF = -1e30
N_MIXERS = 3

NSA_HEAD_DIM = 128
NSA_HEADS = 16
NSA_KV_GROUPS = 4
NSA_HPG = NSA_HEADS // NSA_KV_GROUPS
NSA_CMP_BLOCK = 32
NSA_CMP_STRIDE = 16
NSA_SEL_BLOCK = 64
NSA_TOP_N = 16
NSA_WINDOW = 512
NSA_FORCE_BONUS = 1e4
NSA_SCALE2 = float(NSA_HEAD_DIM ** -0.5 * np.log2(np.e))
NSA_MASK_NEG = -2.0 ** 100

GLA_HEADS = 4
GLA_KEY_DIM = 256
GLA_VAL_DIM = 512
GLA_GATE_RANK = 16
GLA_TAU = 16.0
GLA_CHUNK = 64
GLA_SUB = 16

SWA_HEAD_DIM = 64
SWA_HEADS = 32
SWA_KV_HEADS = 4
SWA_HPG = SWA_HEADS // SWA_KV_HEADS
SWA_WINDOW = 128

LANES = 128
VMEM_LIMIT = 56 * 1024 * 1024
MLP_VMEM_LIMIT = 60 * 1024 * 1024

NT_DIMS = (((1,), (1,)), ((), ()))
TN_DIMS = (((0,), (0,)), ((), ()))


def _params(sem, vmem_limit=VMEM_LIMIT):
    return pltpu.CompilerParams(dimension_semantics=sem, vmem_limit_bytes=vmem_limit)


def _dot(a, b, dims=None, precision=None):
    if dims is None:
        return jnp.dot(a, b, preferred_element_type=F32, precision=precision)
    return lax.dot_general(a, b, dims, preferred_element_type=F32, precision=precision)


def _rms_rows_to(h_scr, x_ref, g_ref, rows):
    n = x_ref.shape[0] // rows

    def body(i, c):
        r0 = pl.multiple_of(i * rows, rows)
        x = x_ref[pl.ds(r0, rows), :]
        ms = jnp.mean(x * x, axis=-1, keepdims=True)
        h_scr[pl.ds(r0, rows), :] = (x * lax.rsqrt(ms + NORM_EPS) * g_ref[...]).astype(BF16)
        return c

    lax.fori_loop(0, n, body, 0)


def _row_tile_copy(x_hbm, x_buf, sem, i):
    tm = x_buf.shape[0]
    return pltpu.make_async_copy(x_hbm.at[pl.ds(pl.multiple_of(i * tm, tm), tm), :], x_buf, sem)


def _fetch_and_norm(x_hbm, x_buf, sem, g_ref, h_scr, prefetch_step):
    i, j = pl.program_id(0), pl.program_id(1)

    @pl.when(j == 0)
    def _():
        @pl.when(i == 0)
        def _():
            _row_tile_copy(x_hbm, x_buf, sem, 0).start()

        _row_tile_copy(x_hbm, x_buf, sem, i).wait()
        _rms_rows_to(h_scr, x_buf, g_ref, 128)

    @pl.when((j == prefetch_step) & (i + 1 < pl.num_programs(0)))
    def _():
        _row_tile_copy(x_hbm, x_buf, sem, i + 1).start()


def _norm_matmul_kernel(x_hbm, g_ref, w_ref, *rest, has_extra, n_head_tiles):
    rest = list(rest)
    wt_ref = rest.pop(0) if n_head_tiles is not None else None
    if has_extra:
        wx_ref, o_ref, ox_ref, h_scr, x_buf, sem = rest
    else:
        o_ref, h_scr, x_buf, sem = rest
    j = pl.program_id(1)
    _fetch_and_norm(x_hbm, x_buf, sem, g_ref, h_scr, prefetch_step=1)

    if has_extra:
        @pl.when(j == 0)
        def _():
            ox_ref[...] = _dot(h_scr[...], wx_ref[...].astype(BF16))

    if wt_ref is None:
        o_ref[...] = _dot(h_scr[...], w_ref[...].astype(BF16))
    else:
        @pl.when(j < n_head_tiles)
        def _():
            o_ref[...] = _dot(h_scr[...], w_ref[...].astype(BF16))

        @pl.when(j >= n_head_tiles)
        def _():
            o_ref[...] = _dot(h_scr[...], wt_ref[...].astype(BF16))


def norm_matmul(x, gain, w, layer, n_cols, w_extra=None, w_tail=None, tm=2048, tn=512):
    m, d = x.shape
    tm = min(tm, m)
    has_extra = w_extra is not None
    n_head = n_cols // tn
    n_tail = 0 if w_tail is None else w_tail.shape[1] // tn
    in_specs = [pl.BlockSpec(memory_space=pl.ANY),
                pl.BlockSpec((1, d), lambda i, j: (0, 0)),
                pl.BlockSpec((None, d, tn), lambda i, j: (layer, 0, jnp.minimum(j, n_head - 1)))]
    out_shape = [jax.ShapeDtypeStruct((m, (n_head + n_tail) * tn), F32)]
    out_specs = [pl.BlockSpec((tm, tn), lambda i, j: (i, j))]
    args = [x, gain.reshape(1, d), w]
    if w_tail is not None:
        in_specs.append(pl.BlockSpec((d, tn), lambda i, j: (0, jnp.maximum(j - n_head, 0))))
        args.append(w_tail)
    if has_extra:
        in_specs.append(pl.BlockSpec((d, LANES), lambda i, j: (0, 0)))
        out_shape.append(jax.ShapeDtypeStruct((m, LANES), F32))
        out_specs.append(pl.BlockSpec((tm, LANES), lambda i, j: (i, 0)))
        args.append(w_extra)
    return pl.pallas_call(
        functools.partial(_norm_matmul_kernel, has_extra=has_extra,
                          n_head_tiles=None if w_tail is None else n_head),
        grid=(m // tm, n_head + n_tail),
        in_specs=in_specs, out_specs=out_specs, out_shape=out_shape,
        scratch_shapes=[pltpu.VMEM((tm, d), BF16), pltpu.VMEM((tm, d), F32), pltpu.SemaphoreType.DMA(())],
        compiler_params=_params(("arbitrary", "arbitrary")),
        name="norm_matmul",
    )(*args)


def _matmul_residual_kernel(a_ref, w_ref, r_ref, o_ref):
    o_ref[...] = r_ref[...] + _dot(a_ref[...], w_ref[...].astype(BF16))


def matmul_residual(a, w, layer, res, tm=2048, tn=512):
    m, k = a.shape
    tm = min(tm, m)
    n = w.shape[2]
    return pl.pallas_call(
        _matmul_residual_kernel,
        grid=(m // tm, n // tn),
        in_specs=[pl.BlockSpec((tm, k), lambda i, j: (i, 0)),
                  pl.BlockSpec((None, k, tn), lambda i, j: (layer, 0, j)),
                  pl.BlockSpec((tm, tn), lambda i, j: (i, j))],
        out_specs=pl.BlockSpec((tm, tn), lambda i, j: (i, j)),
        out_shape=jax.ShapeDtypeStruct((m, n), F32),
        compiler_params=_params(("parallel", "parallel")),
        name="matmul_residual",
    )(a, w, res)


def _mlp_kernel(x_hbm, g_ref, wu_ref, xt_ref, wd_ref, o_ref, h_scr, u_scr, x_buf, sem, *, nf, tf, kc, splits):
    i, j = pl.program_id(0), pl.program_id(1)
    tm, d = h_scr.shape
    n_chunks = pl.num_programs(1) - nf
    rows = x_buf.shape[1]
    nxt = jnp.minimum(i + 1, pl.num_programs(0) - 1)

    def chunk_copy(tile, s):
        r0 = pl.multiple_of(tile * tm + s * rows, rows)
        return pltpu.make_async_copy(x_hbm.at[pl.ds(r0, rows), :], x_buf.at[s % 2], sem.at[s % 2])

    def norm_chunk(s):
        x = x_buf[s % 2]
        ms = jnp.mean(x * x, axis=-1, keepdims=True)
        h_scr[pl.ds(pl.multiple_of(s * rows, rows), rows), :] = (x * lax.rsqrt(ms + NORM_EPS) * g_ref[...]).astype(BF16)

    @pl.when((j == 0) & (i == 0))
    def _():
        def first_tile(s, carry):
            chunk_copy(0, s).start()
            chunk_copy(0, s).wait()
            norm_chunk(s)
            return carry

        lax.fori_loop(0, n_chunks, first_tile, 0)

    @pl.when(j == nf - 1)
    def _():
        chunk_copy(nxt, 0).start()

    @pl.when(j < nf)
    def _():
        u = _dot(h_scr[:, 0:kc], wu_ref[0:kc, :].astype(BF16))
        for c in range(1, d // kc):
            u = u + _dot(h_scr[:, c * kc:(c + 1) * kc], wu_ref[c * kc:(c + 1) * kc, :].astype(BF16))
        u = jnp.maximum(u, 0.0)
        u_scr[j] = (u * u).astype(BF16)

    for part in range(splits):
        @pl.when((j >= nf) & ((j - nf) % splits == part))
        def _(part=part):
            s = j - nf
            chunk_copy(nxt, s).wait()

            @pl.when(s + 1 < n_chunks)
            def _():
                chunk_copy(nxt, s + 1).start()

            norm_chunk(s)
            acc = xt_ref[...] if part == 0 else o_ref[...]
            per = nf // splits
            for f in range(per):
                for c in range(tf // kc):
                    piece = slice(f * tf + c * kc, f * tf + (c + 1) * kc)
                    acc = acc + _dot(u_scr[part * per + f, :, c * kc:(c + 1) * kc], wd_ref[piece, :].astype(BF16))
            o_ref[...] = acc


def mlp_block(x, gain, w_up, w_down, layer, tm=1024, tf=512, tn=512, kc=512, splits=2):
    m, d = x.shape
    ff = w_up.shape[2]
    nf, nn = ff // tf, d // tn
    n_chunks = nn * splits
    down = lambda j: jnp.maximum(j - nf, 0)
    return pl.pallas_call(
        functools.partial(_mlp_kernel, nf=nf, tf=tf, kc=kc, splits=splits),
        grid=(m // tm, nf + n_chunks),
        in_specs=[pl.BlockSpec(memory_space=pl.ANY),
                  pl.BlockSpec((1, d), lambda i, j: (0, 0)),
                  pl.BlockSpec((None, d, tf), lambda i, j: (layer, 0, jnp.minimum(j, nf - 1))),
                  pl.BlockSpec((tm, tn), lambda i, j: (i, down(j) // splits)),
                  pl.BlockSpec((None, ff // splits, tn), lambda i, j: (layer, down(j) % splits, down(j) // splits))],
        out_specs=pl.BlockSpec((tm, tn), lambda i, j: (i, down(j) // splits)),
        out_shape=jax.ShapeDtypeStruct((m, d), F32),
        scratch_shapes=[pltpu.VMEM((tm, d), BF16), pltpu.VMEM((nf, tm, tf), BF16),
                        pltpu.VMEM((2, tm // n_chunks, d), F32), pltpu.SemaphoreType.DMA((2,))],
        compiler_params=_params(("arbitrary", "arbitrary"), MLP_VMEM_LIMIT),
        name="mlp_block",
    )(x, gain.reshape(1, d), w_up, x, w_down)


def rope_tables(pos, head_dim):
    rot = head_dim // ROPE_FRACTION
    half = rot // 2
    inv_freq = jnp.power(jnp.float32(ROPE_THETA), -jnp.arange(half, dtype=F32) / half)
    ang = pos.astype(F32)[:, None] * inv_freq[None, :]
    cos, sin = jnp.cos(ang), jnp.sin(ang)
    n = pos.shape[0]
    zeros = jnp.zeros((n, head_dim - rot), F32)
    c = jnp.concatenate([cos, cos, jnp.ones((n, head_dim - rot), F32)], axis=-1)
    s1 = jnp.concatenate([-sin, jnp.zeros((n, half), F32), zeros], axis=-1)
    s2 = jnp.concatenate([jnp.zeros((n, half), F32), sin, zeros], axis=-1)
    reps = LANES // head_dim
    return tuple(jnp.tile(t, (1, reps)) for t in (c, s1, s2))


def _norm_rope(x, gain, c, s1, s2, half, ms=None):
    if ms is None:
        ms = jnp.mean(x * x, axis=-1, keepdims=True)
    y = x * lax.rsqrt(ms + NORM_EPS) * gain
    return y * c + pltpu.roll(y, LANES - half, 1) * s1 + pltpu.roll(y, half, 1) * s2


def _gelu_tanh(x):
    return 0.5 * x * (1.0 + jnp.tanh(np.sqrt(2.0 / np.pi) * (x + 0.044715 * (x * x * x))))


def _nsa_compress_kernel(x_ref, pe_ref, w1_ref, w2_ref, g_ref, c_ref, s1_ref, s2_ref, o_ref):
    n_rows = x_ref.shape[0] // NSA_CMP_STRIDE
    d = NSA_HEAD_DIM
    y0 = jnp.zeros((n_rows, d), F32)
    y1 = jnp.zeros((n_rows, d), F32)
    for r in range(NSA_CMP_STRIDE):
        xr = x_ref[pl.ds(r, n_rows, stride=NSA_CMP_STRIDE), :]
        a0 = (xr + pe_ref[r:r + 1, :]).astype(BF16)
        a1 = (xr + pe_ref[NSA_CMP_STRIDE + r:NSA_CMP_STRIDE + r + 1, :]).astype(BF16)
        y0 = y0 + _dot(a0, w1_ref[r * d:(r + 1) * d, :].astype(BF16))
        y1 = y1 + _dot(a1, w1_ref[(NSA_CMP_STRIDE + r) * d:(NSA_CMP_STRIDE + r + 1) * d, :].astype(BF16))
    pre = y0 + pltpu.roll(y1, n_rows - 1, 0)
    out = _dot(_gelu_tanh(pre).astype(BF16), w2_ref[...].astype(BF16))
    roped = _norm_rope(out, g_ref[...], c_ref[...], s1_ref[...], s2_ref[...],
                       NSA_HEAD_DIM // ROPE_FRACTION // 2, None)
    o_ref[...] = jnp.where(pl.program_id(0) == 0, roped * NSA_SCALE2, out).astype(BF16)


def nsa_compress(proj, pe, w1, w2, k_gain, tables):
    b, s, _ = proj.shape
    g, d = NSA_KV_GROUPS, NSA_HEAD_DIM
    n_rows = s // NSA_CMP_STRIDE
    col_blk = NSA_HEADS
    tab_spec = pl.BlockSpec((n_rows, d), lambda kv, bi, gi: (0, 0))
    return pl.pallas_call(
        _nsa_compress_kernel,
        grid=(2, b, g),
        in_specs=[pl.BlockSpec((None, s, d), lambda kv, bi, gi: (bi, 0, col_blk + kv * g + gi)),
                  pl.BlockSpec((None, NSA_CMP_BLOCK, d), lambda kv, bi, gi: (kv, 0, 0)),
                  pl.BlockSpec((None, NSA_CMP_BLOCK * d, d), lambda kv, bi, gi: (kv, 0, 0)),
                  pl.BlockSpec((None, d, d), lambda kv, bi, gi: (kv, 0, 0)),
                  pl.BlockSpec((1, d), lambda kv, bi, gi: (0, 0)),
                  tab_spec, tab_spec, tab_spec],
        out_specs=pl.BlockSpec((None, None, None, n_rows, d), lambda kv, bi, gi: (kv, bi, gi, 0, 0)),
        out_shape=jax.ShapeDtypeStruct((2, b, g, n_rows, d), BF16),
        compiler_params=_params(("parallel", "parallel", "parallel")),
        name="nsa_compress",
    )(proj, pe, w1, w2, k_gain.reshape(1, d), *tables)


def _nsa_attn_kernel(q_ref, kc_ref, vc_ref, ksr_ref, vsr_ref, kwr_ref, vwr_ref, gate_ref, ovt_ref,
                     qg_ref, kg_ref, c_ref, s1_ref, s2_ref, o_ref, ks_ref, vs_ref, kw_ref, vw_ref, q_scr, score_ref, rank_ref,
                     *, tq, tk, seq):
    gi = pl.program_id(1)
    qi = pl.program_id(2)
    d = NSA_HEAD_DIM
    hpg = NSA_HPG
    n_sel = seq // NSA_SEL_BLOCK
    n_top = min(NSA_TOP_N, n_sel)
    half = d // ROPE_FRACTION // 2

    def tables(rows):
        return c_ref[rows, :], s1_ref[rows, :], s2_ref[rows, :]

    scale2 = NSA_SCALE2

    @pl.when(qi == 0)
    def _():
        chunk = 512
        lane = lax.broadcasted_iota(jnp.int32, (chunk, d), 1)
        row = lax.broadcasted_iota(jnp.int32, (chunk, d), 0)

        def body(i, carry):
            r0 = pl.multiple_of(i * chunk, chunk)
            rows = pl.ds(r0, chunk)
            ks = _norm_rope(ksr_ref[rows, :], kg_ref[0:1, :], *tables(rows), half, None) * scale2
            kw = _norm_rope(kwr_ref[rows, :], kg_ref[1:2, :], *tables(rows), half, None) * scale2
            ks_ref[rows, 0:d] = ks.astype(BF16)
            ks_ref[rows, d:2 * d] = jnp.where((r0 + row) // NSA_SEL_BLOCK == lane, NSA_MASK_NEG, 0.0).astype(BF16)
            kw_ref[rows, :] = kw.astype(BF16)
            vs_ref[rows, :] = vsr_ref[rows, :].astype(BF16)
            vw_ref[rows, :] = vwr_ref[rows, :].astype(BF16)
            return carry

        lax.fori_loop(0, seq // chunk, body, 0)

    q_rows = 128

    def q_body(i, carry):
        r0 = pl.multiple_of(i * q_rows, q_rows)
        tabs = tables(pl.ds(pl.multiple_of(qi * tq, tq) + r0, q_rows))
        for h in range(hpg):
            y = _norm_rope(q_ref[pl.ds(r0, q_rows), h * d:(h + 1) * d], qg_ref[...], *tabs, half, None)
            q_scr[pl.ds(h * tq + r0, q_rows), :] = y.astype(BF16)
        return carry

    lax.fori_loop(0, tq // q_rows, q_body, 0)
    q4 = q_scr[...]
    pos = qi * tq + lax.broadcasted_iota(jnp.int32, (tq, 1), 0)
    pos_l = qi * tq + lax.broadcasted_iota(jnp.int32, (1, tq), 1)

    def masked_scores(qs, k, bias=None):
        sc = _dot(qs, k, NT_DIMS)
        sc = sc.reshape(-1, tq, sc.shape[-1])
        return sc if bias is None else sc + bias[None]

    def stack(t):
        return t.reshape(t.shape[0] * tq, -1)

    span = NSA_WINDOW + tq
    w0 = pl.multiple_of(jnp.clip(qi * tq - NSA_WINDOW, 0, seq - span), tq)
    vw = vw_ref[pl.ds(w0, span), :]
    kp = w0 + lax.broadcasted_iota(jnp.int32, (1, span), 1)
    sw = masked_scores(q4, kw_ref[pl.ds(w0, span), :],
                       jnp.where((kp <= pos) & (kp > pos - NSA_WINDOW), 0.0, NEG_INF))
    pw = jnp.exp2(sw - jnp.max(sw, axis=-1, keepdims=True))
    o_win = _dot(stack(pw).astype(BF16), vw) / stack(jnp.sum(pw, axis=-1, keepdims=True))

    n_c = kc_ref.shape[0]
    cend = lax.broadcasted_iota(jnp.int32, (1, n_c), 1) * NSA_CMP_STRIDE + (NSA_CMP_BLOCK - 1)
    s = masked_scores(q4, kc_ref[...], jnp.where(cend <= pos, 0.0, NEG_INF))
    p = jnp.exp2(s - jnp.max(s, axis=-1, keepdims=True))
    p = p / jnp.sum(p, axis=-1, keepdims=True)
    p = jnp.where((pos >= NSA_CMP_BLOCK - 1)[None], p, 0.0)
    o_cmp = _dot(stack(p).astype(BF16), vc_ref[...])
    p_sum = jnp.sum(p, axis=0)
    imp_t = _dot(ovt_ref[...], p_sum, NT_DIMS, precision=lax.Precision.HIGHEST)

    j = lax.broadcasted_iota(jnp.int32, (n_sel, 1), 0)
    bq = pos_l // NSA_SEL_BLOCK
    forced = (j == 0) | (j == bq) | (j == bq - 1)
    score = jnp.where(j <= bq, imp_t + jnp.where(forced, NSA_FORCE_BONUS, 0.0), NEG_INF)
    sub = 8
    n_grp = n_sel // sub
    score_ref[...] = score
    rank_ref[...] = jnp.zeros_like(rank_ref)
    j_loc = lax.broadcasted_iota(jnp.int32, (sub, tq), 0)
    last_blk = (qi * tq + tq - 1) // NSA_SEL_BLOCK
    for gp in range(n_grp):
        @pl.when(gp * sub <= last_blk)
        def _(gp=gp):
            groups = [score_ref[v * sub:(v + 1) * sub, :] for v in range(n_grp)]
            ranks = [rank_ref[v * sub:(v + 1) * sub, :] for v in range(n_grp)]
            for jp in range(gp * sub, (gp + 1) * sub):
                row = score_ref[jp:jp + 1, :]
                for v, sg in enumerate(groups):
                    if v * sub > jp:
                        beats = row >= sg
                    elif (v + 1) * sub - 1 <= jp:
                        beats = row > sg
                    else:
                        beats = (row > sg) | ((row == sg) & (j_loc > jp - v * sub))
                    ranks[v] = ranks[v] + jnp.where(beats, 1.0, 0.0)
            for v in range(n_grp):
                rank_ref[v * sub:(v + 1) * sub, :] = ranks[v]
    not_sel = jnp.where(rank_ref[...] < n_top, 0.0, 1.0)
    not_sel = jnp.concatenate([not_sel, jnp.zeros((LANES - n_sel, tq), F32)], axis=0).T.astype(BF16)

    q_aug = jnp.concatenate([q4, jnp.concatenate([not_sel] * hpg, axis=0)], axis=1)

    def sel_tile(kt, carry, bias):
        m, l, acc = carry
        k0 = pl.multiple_of(kt * tk, tk)
        sc = masked_scores(q_aug, ks_ref[pl.ds(k0, tk), :], bias)
        m_new = jnp.maximum(m, jnp.max(sc, axis=-1, keepdims=True))
        alpha = jnp.exp2(m - m_new)
        pe = jnp.exp2(sc - m_new)
        l = alpha * l + jnp.sum(pe, axis=-1, keepdims=True)
        acc = stack(alpha) * acc + _dot(stack(pe).astype(BF16), vs_ref[pl.ds(k0, tk), :])
        return m_new, l, acc

    last = (qi * tq + tq - 1) // tk
    init = (jnp.full((hpg, tq, 1), NEG_INF, F32), jnp.zeros((hpg, tq, 1), F32), jnp.zeros((hpg * tq, d), F32))
    carry = lax.fori_loop(0, last, lambda kt, c: sel_tile(kt, c, None), init)
    key_pos = last * tk + lax.broadcasted_iota(jnp.int32, (1, tk), 1)
    _, l_sel, acc_sel = sel_tile(last, carry, jnp.where(key_pos <= pos, 0.0, NEG_INF))
    o_sel = acc_sel / stack(l_sel)

    gt = jax.nn.sigmoid(gate_ref[...])
    lane = lax.broadcasted_iota(jnp.int32, (1, LANES), 1)
    for h in range(hpg):
        rows = slice(h * tq, (h + 1) * tq)
        o_h = jnp.zeros((tq, d), F32)
        for br, o_br in enumerate((o_cmp, o_sel, o_win)):
            g_col = jnp.sum(jnp.where(lane == gi * (3 * hpg) + h * 3 + br, gt, 0.0), axis=-1, keepdims=True)
            o_h = o_h + g_col * o_br[rows]
        o_ref[:, h * d:(h + 1) * d] = o_h.astype(BF16)


def selection_overlap(n_rows, n_sel):
    n_cmp = n_rows - 1
    c0 = np.arange(n_rows) * NSA_CMP_STRIDE
    s0 = np.arange(n_sel) * NSA_SEL_BLOCK
    ov = np.minimum(c0[:, None] + NSA_CMP_BLOCK, s0[None, :] + NSA_SEL_BLOCK) - np.maximum(c0[:, None], s0[None, :])
    ov = np.clip(ov, 0, None) / NSA_CMP_BLOCK
    ov[n_cmp:] = 0.0
    return jnp.asarray(ov, dtype=F32)


def nsa_attention(proj, kv_cmp, gate, q_gain, k_gains, tabs, tq=256, tk=512):
    b, s, _ = proj.shape
    g, d, hpg = NSA_KV_GROUPS, NSA_HEAD_DIM, NSA_HPG
    n_rows = s // NSA_CMP_STRIDE
    n_sel = s // NSA_SEL_BLOCK
    seg_blk = lambda i: NSA_HEADS + i * g
    cmp_spec = lambda kv: pl.BlockSpec((None, None, None, n_rows, d), lambda bi, gi, qi: (kv, bi, gi, 0, 0))
    seq_spec = lambda i: pl.BlockSpec((None, s, d), lambda bi, gi, qi: (bi, 0, seg_blk(i) + gi))
    const = lambda shape: pl.BlockSpec(shape, lambda bi, gi, qi: (0, 0))
    return pl.pallas_call(
        functools.partial(_nsa_attn_kernel, tq=tq, tk=tk, seq=s),
        grid=(b, g, s // tq),
        in_specs=[pl.BlockSpec((None, tq, hpg * d), lambda bi, gi, qi: (bi, qi, gi)),
                  cmp_spec(0), cmp_spec(1),
                  seq_spec(2), seq_spec(3), seq_spec(4), seq_spec(5),
                  pl.BlockSpec((None, tq, LANES), lambda bi, gi, qi: (bi, qi, 0)),
                  const((n_sel, n_rows)), const((1, d)), const((2, d)),
                  const((s, d)), const((s, d)), const((s, d))],
        out_specs=pl.BlockSpec((None, tq, hpg * d), lambda bi, gi, qi: (bi, qi, gi)),
        out_shape=jax.ShapeDtypeStruct((b, s, NSA_HEADS * d), BF16),
        scratch_shapes=[pltpu.VMEM((s, 2 * d), BF16)] + [pltpu.VMEM((s, d), BF16)] * 3
                       + [pltpu.VMEM((hpg * tq, d), BF16)] + [pltpu.VMEM((n_sel, tq), F32)] * 2,
        compiler_params=_params(("parallel", "parallel", "arbitrary")),
        name="nsa_attention",
    )(proj, kv_cmp, kv_cmp, proj, proj, proj, proj, gate, selection_overlap(n_rows, n_sel).T,
      q_gain.reshape(1, d), k_gains, *tabs)


def nsa_mixer(x2, b, s, gain, w_in, w_out, layer, q_norm, k_norm, cmp_pe, cmp_w1, cmp_w2):
    d, g = NSA_HEAD_DIM, NSA_KV_GROUPS
    n_main = (NSA_HEADS + 6 * g) * d
    w_gate = jnp.pad(w_in[layer, :, n_main:], ((0, 0), (0, LANES - 3 * NSA_HEADS)))
    proj, gate = norm_matmul(x2, gain, w_in, layer, n_main, w_gate)
    proj = proj.reshape(b, s, n_main)
    gate = gate.reshape(b, s, LANES)
    tabs = rope_tables(jnp.arange(s), d)
    cmp_end = jnp.arange(s // NSA_CMP_STRIDE) * NSA_CMP_STRIDE + (NSA_CMP_BLOCK - 1)
    kv_cmp = nsa_compress(proj, cmp_pe, cmp_w1, cmp_w2, k_norm[0], rope_tables(cmp_end, d))
    o = nsa_attention(proj, kv_cmp, gate, q_norm, k_norm[1:3], tabs)
    return matmul_residual(o.reshape(b * s, NSA_HEADS * d), w_out, layer, x2)


def _log_sigmoid(z):
    return jnp.minimum(z, 0.0) - jnp.log1p(jnp.exp(-jnp.abs(z)))


def _gla_kernel(q_ref, k_ref, v_ref, r_ref, glr_ref, wg_ref, bg_ref, on_ref, o_ref, state_ref, *, rows, hps):
    c = GLA_CHUNK
    sub = GLA_SUB
    dk, dv = GLA_KEY_DIM, GLA_VAL_DIM

    @pl.when(pl.program_id(2) == 0)
    def _():
        state_ref[...] = jnp.zeros_like(state_ref)

    tri = jnp.where(lax.broadcasted_iota(jnp.int32, (c, c), 0) >= lax.broadcasted_iota(jnp.int32, (c, c), 1), 1.0, 0.0)
    sub_row = lax.broadcasted_iota(jnp.int32, (sub, 1), 0)

    def head_chunk(hh, r0):
        kcols = slice(hh * dk, (hh + 1) * dk)
        vcols = slice(hh * dv, (hh + 1) * dv)
        qc = q_ref[pl.ds(r0, c), kcols] * (dk ** -0.5)
        kc = k_ref[pl.ds(r0, c), kcols]
        vc = v_ref[pl.ds(r0, c), vcols]
        z = _dot(glr_ref[pl.ds(r0, c), :], wg_ref[:, kcols], precision=lax.Precision.HIGHEST) + bg_ref[:, kcols]
        log_a = _log_sigmoid(z) / GLA_TAU
        cum = _dot(tri, log_a, precision=lax.Precision.HIGHEST)
        vb = vc.astype(BF16)

        state = state_ref[hh]
        o_inter = _dot((qc * jnp.exp(cum)).astype(BF16), state.astype(BF16), NT_DIMS)

        parts = []
        for bi in range(c // sub):
            lo = bi * sub
            q_i, k_i, v_i, c_i = qc[lo:lo + sub], kc[lo:lo + sub], vc[lo:lo + sub], cum[lo:lo + sub]
            o_i = o_inter[lo:lo + sub]
            if bi > 0:
                c0 = c_i[0:1]
                qs = (q_i * jnp.exp(c_i - c0)).astype(BF16)
                kp = (kc[:lo] * jnp.exp(c0 - cum[:lo])).astype(BF16)
                att = _dot(qs, kp, NT_DIMS)
                o_i = o_i + _dot(att.astype(BF16), vb[:lo])
            for jj in range(sub):
                e = jnp.exp(jnp.where(sub_row >= jj, c_i - c_i[jj:jj + 1], NEG_INF))
                col = jnp.sum(q_i * k_i[jj:jj + 1] * e, axis=-1, keepdims=True)
                o_i = o_i + col * v_i[jj:jj + 1]
            parts.append(o_i)
        o = jnp.concatenate(parts, axis=0)

        last = cum[c - 1:c]
        kd = (kc * jnp.exp(last - cum)).astype(BF16)
        state_ref[hh] = state * jnp.exp(last) + _dot(vb, kd, TN_DIMS)

        ms = jnp.mean(o * o, axis=-1, keepdims=True)
        on = o * lax.rsqrt(ms + NORM_EPS) * on_ref[...]
        r = r_ref[pl.ds(r0, c), vcols]
        o_ref[pl.ds(r0, c), vcols] = (on * (r * jax.nn.sigmoid(r))).astype(BF16)

    def chunk(ci, carry):
        r0 = pl.multiple_of(ci * c, c)
        for hh in range(hps):
            head_chunk(hh, r0)
        return carry

    lax.fori_loop(0, rows // c, chunk, 0)


def gla_attention(proj, glr, w_gate_up, b_gate, o_norm, rows=512, hps=4):
    b, s, _ = proj.shape
    nh, dk, dv = GLA_HEADS, GLA_KEY_DIM, GLA_VAL_DIM
    wg = jnp.pad(w_gate_up, ((0, LANES - GLA_GATE_RANK), (0, 0)))
    kw, vw = hps * dk, hps * dv
    k_blk = nh * dk // kw
    v_blk = 2 * nh * dk // vw
    r_blk = (2 * nh * dk + nh * dv) // vw
    return pl.pallas_call(
        functools.partial(_gla_kernel, rows=rows, hps=hps),
        grid=(b, nh // hps, s // rows),
        in_specs=[pl.BlockSpec((None, rows, kw), lambda bi, h, ci: (bi, ci, h)),
                  pl.BlockSpec((None, rows, kw), lambda bi, h, ci: (bi, ci, k_blk + h)),
                  pl.BlockSpec((None, rows, vw), lambda bi, h, ci: (bi, ci, v_blk + h)),
                  pl.BlockSpec((None, rows, vw), lambda bi, h, ci: (bi, ci, r_blk + h)),
                  pl.BlockSpec((None, rows, LANES), lambda bi, h, ci: (bi, ci, 0)),
                  pl.BlockSpec((LANES, kw), lambda bi, h, ci: (0, h)),
                  pl.BlockSpec((1, kw), lambda bi, h, ci: (0, h)),
                  pl.BlockSpec((1, dv), lambda bi, h, ci: (0, 0))],
        out_specs=pl.BlockSpec((None, rows, vw), lambda bi, h, ci: (bi, ci, h)),
        out_shape=jax.ShapeDtypeStruct((b, s, nh * dv), BF16),
        scratch_shapes=[pltpu.VMEM((hps, dv, dk), F32)],
        compiler_params=_params(("parallel", "parallel", "arbitrary")),
        name="gla_attention",
    )(proj, proj, proj, proj, glr, wg, b_gate.reshape(1, nh * dk), o_norm.reshape(1, dv))


def gla_mixer(x2, b, s, gain, w_in, w_gate_up, b_gate, o_norm, w_out, layer):
    nh, dk, dv = GLA_HEADS, GLA_KEY_DIM, GLA_VAL_DIM
    n_qkv = 2 * nh * dk + nh * dv
    w_r = w_in[layer, :, n_qkv + GLA_GATE_RANK:]
    w_glr = jnp.pad(w_in[layer, :, n_qkv:n_qkv + GLA_GATE_RANK], ((0, 0), (0, LANES - GLA_GATE_RANK)))
    proj, glr = norm_matmul(x2, gain, w_in, layer, n_qkv, w_glr, w_tail=w_r)
    o = gla_attention(proj.reshape(b, s, -1), glr.reshape(b, s, LANES), w_gate_up, b_gate, o_norm)
    return matmul_residual(o.reshape(b * s, nh * dv), w_out, layer, x2)


def _swa_heads(x, gain, tabs, seg, out_scale):
    blk = x.shape[0]
    d = SWA_HEAD_DIM
    n = x.shape[1] // LANES
    tiles = jnp.concatenate([x[:, t * LANES:(t + 1) * LANES] for t in range(n)], axis=0)
    sq = tiles * tiles
    hi = sq.astype(BF16)
    lo = (sq - hi.astype(F32)).astype(BF16)
    ms = (_dot(hi, seg) + _dot(lo, seg)) * (1.0 / d)
    heads = []
    for t in range(n):
        rows = slice(t * blk, (t + 1) * blk)
        y = _norm_rope(tiles[rows], gain, *tabs, d // ROPE_FRACTION // 2, ms[rows]) * out_scale
        heads += [y[:, u * d:(u + 1) * d] for u in range(LANES // d)]
    return heads


def _swa_kernel(q_ref, kp_ref, kc_ref, vp_ref, vc_ref, sink_ref, qg_ref, kg_ref, seg_ref,
                cc_ref, s1c_ref, s2c_ref, cp_ref, s1p_ref, s2p_ref, o_ref, *, blk):
    qi = pl.program_id(1)
    hpg, d, g = SWA_HPG, SWA_HEAD_DIM, SWA_KV_HEADS
    seg = seg_ref[...]
    tabs_c = (cc_ref[...], s1c_ref[...], s2c_ref[...])
    tabs_p = (cp_ref[...], s1p_ref[...], s2p_ref[...])
    q_heads = _swa_heads(q_ref[...], qg_ref[...], tabs_c, seg, d ** -0.5)
    kp_heads = _swa_heads(kp_ref[...], kg_ref[...], tabs_p, seg, 1.0)
    kc_heads = _swa_heads(kc_ref[...], kg_ref[...], tabs_c, seg, 1.0)
    vp, vc = vp_ref[...], vc_ref[...]

    qp = qi * blk + lax.broadcasted_iota(jnp.int32, (blk, 1), 0)
    kpos = (qi - 1) * blk + lax.broadcasted_iota(jnp.int32, (1, 2 * blk), 1)
    bias = jnp.where((kpos <= qp) & (kpos > qp - SWA_WINDOW) & (kpos >= 0), 0.0, NEG_INF)

    for gi in range(g):
        q = jnp.concatenate(q_heads[gi * hpg:(gi + 1) * hpg], axis=0).astype(BF16)
        k = jnp.concatenate([kp_heads[gi], kc_heads[gi]], axis=0).astype(BF16)
        v = jnp.concatenate([vp[:, gi * d:(gi + 1) * d], vc[:, gi * d:(gi + 1) * d]], axis=0).astype(BF16)
        s = _dot(q, k, NT_DIMS).reshape(hpg, blk, 2 * blk) + bias[None]
        sink = sink_ref[gi]
        m = jnp.maximum(jnp.max(s, axis=-1, keepdims=True), sink)
        p = jnp.exp(s - m)
        denom = jnp.sum(p, axis=-1, keepdims=True) + jnp.exp(sink - m)
        o = _dot(p.reshape(hpg * blk, 2 * blk).astype(BF16), v).reshape(hpg, blk, d) / denom
        o_ref[:, gi * hpg * d:(gi + 1) * hpg * d] = jnp.concatenate([o[h] for h in range(hpg)], axis=1).astype(BF16)


def swa_attention(proj, sinks, q_gain, k_gain, tabs, blk=128):
    b, s, _ = proj.shape
    g, hpg, d = SWA_KV_HEADS, SWA_HPG, SWA_HEAD_DIM
    nq, nkv = SWA_HEADS * d, g * d
    k_blk = nq // nkv
    prev = lambda qi: jnp.maximum(qi - 1, 0)
    kv_spec = lambda col, row: pl.BlockSpec((None, blk, nkv), lambda bi, qi: (bi, row(qi), col))
    const = lambda shape: pl.BlockSpec(shape, lambda bi, qi: (0,) * len(shape))
    tab_spec = lambda row: pl.BlockSpec((blk, LANES), lambda bi, qi: (row(qi), 0))
    cur = lambda qi: qi
    seg_id = np.arange(LANES) // d
    seg = jnp.asarray(seg_id[:, None] == seg_id[None, :], dtype=BF16)
    tile_gain = lambda gn: jnp.tile(gn.reshape(1, d), (1, LANES // d))
    return pl.pallas_call(
        functools.partial(_swa_kernel, blk=blk),
        grid=(b, s // blk),
        in_specs=[pl.BlockSpec((None, blk, nq), lambda bi, qi: (bi, qi, 0)),
                  kv_spec(k_blk, prev), kv_spec(k_blk, cur), kv_spec(k_blk + 1, prev), kv_spec(k_blk + 1, cur),
                  const((g, hpg, 1, 1)), const((1, LANES)), const((1, LANES)), const((LANES, LANES)),
                  tab_spec(cur), tab_spec(cur), tab_spec(cur), tab_spec(prev), tab_spec(prev), tab_spec(prev)],
        out_specs=pl.BlockSpec((None, blk, nq), lambda bi, qi: (bi, qi, 0)),
        out_shape=jax.ShapeDtypeStruct((b, s, nq), BF16),
        compiler_params=_params(("parallel", "parallel")),
        name="swa_attention",
    )(proj, proj, proj, proj, proj, sinks.astype(F32).reshape(g, hpg, 1, 1), tile_gain(q_gain), tile_gain(k_gain),
      seg, *tabs, *tabs)


def swa_mixer(x2, b, s, gain, w_in, w_out, layer, q_norm, k_norm, sinks):
    d, g = SWA_HEAD_DIM, SWA_KV_HEADS
    n_in = (SWA_HEADS + 2 * g) * d
    (proj,) = norm_matmul(x2, gain, w_in, layer, n_in)
    o = swa_attention(proj.reshape(b, s, n_in), sinks, q_norm, k_norm, rope_tables(jnp.arange(s), d))
    return matmul_residual(o.reshape(b * s, SWA_HEADS * d), w_out, layer, x2)


def kernel(x, norm_mix, norm_mlp, mlp_w_up, mlp_w_down, nsa_w_in, nsa_w_out, nsa_q_norm, nsa_k_norm, nsa_cmp_pe, nsa_cmp_w1, nsa_cmp_w2, gla_w_in, gla_w_gate_up, gla_b_gate, gla_o_norm, gla_w_out, swa_w_in, swa_w_out, swa_q_norm, swa_k_norm, swa_sinks):
    b, s, d = x.shape
    x2 = x.reshape(b * s, d)
    ia = ib = ic = 0
    for i in range(norm_mix.shape[0]):
        kind = i % N_MIXERS
        if kind == 0:
            x2 = nsa_mixer(x2, b, s, norm_mix[i], nsa_w_in, nsa_w_out, ia, nsa_q_norm[ia], nsa_k_norm[ia],
                           nsa_cmp_pe[ia], nsa_cmp_w1[ia], nsa_cmp_w2[ia])
            ia += 1
        elif kind == 1:
            x2 = gla_mixer(x2, b, s, norm_mix[i], gla_w_in, gla_w_gate_up[ib], gla_b_gate[ib],
                           gla_o_norm[ib], gla_w_out, ib)
            ib += 1
        else:
            x2 = swa_mixer(x2, b, s, norm_mix[i], swa_w_in, swa_w_out, ic, swa_q_norm[ic], swa_k_norm[ic],
                           swa_sinks[ic])
            ic += 1
        x2 = mlp_block(x2, norm_mlp[i], mlp_w_up, mlp_w_down, i)
    return x2.reshape(b, s, d)
```

```python
import functools

import numpy as np
import jax
import jax.numpy as jnp
from jax import lax
from jax.experimental import pallas as pl
from jax.experimental.pallas import tpu as pltpu

F32 = jnp.float32
BF16 = jnp.bfloat16

NORM_EPS = 1e-6
ROPE_THETA = 500000.0
ROPE_FRACTION = 4
NEG_INF = -1e30
N_MIXERS = 3

NSA_HEAD_DIM = 128
NSA_HEADS = 16
NSA_KV_GROUPS = 4
NSA_HPG = NSA_HEADS // NSA_KV_GROUPS
NSA_CMP_BLOCK = 32
NSA_CMP_STRIDE = 16
NSA_SEL_BLOCK = 64
NSA_TOP_N = 16
NSA_WINDOW = 512
NSA_FORCE_BONUS = 1e4
NSA_SCALE2 = float(NSA_HEAD_DIM ** -0.5 * np.log2(np.e))
NSA_MASK_NEG = -2.0 ** 100

GLA_HEADS = 4
GLA_KEY_DIM = 256
GLA_VAL_DIM = 512
GLA_GATE_RANK = 16
GLA_TAU = 16.0
GLA_CHUNK = 64
GLA_SUB = 16

SWA_HEAD_DIM = 64
SWA_HEADS = 32
SWA_KV_HEADS = 4
SWA_HPG = SWA_HEADS // SWA_KV_HEADS
SWA_WINDOW = 128

LANES = 128
VMEM_LIMIT = 56 * 1024 * 1024
MLP_VMEM_LIMIT = 62 * 1024 * 1024

NT_DIMS = (((1,), (1,)), ((), ()))
TN_DIMS = (((0,), (0,)), ((), ()))


def _params(sem, vmem_limit=VMEM_LIMIT):
    return pltpu.CompilerParams(dimension_semantics=sem, vmem_limit_bytes=vmem_limit)


def _dot(a, b, dims=None, precision=None):
    if dims is None:
        return jnp.dot(a, b, preferred_element_type=F32, precision=precision)
    return lax.dot_general(a, b, dims, preferred_element_type=F32, precision=precision)


def _rms_rows_to(h_scr, x_ref, g_ref, rows):
    n = x_ref.shape[0] // rows

    def body(i, c):
        r0 = pl.multiple_of(i * rows, rows)
        x = x_ref[pl.ds(r0, rows), :]
        ms = jnp.mean(x * x, axis=-1, keepdims=True)
        h_scr[pl.ds(r0, rows), :] = (x * lax.rsqrt(ms + NORM_EPS) * g_ref[...]).astype(BF16)
        return c

    lax.fori_loop(0, n, body, 0)


def _row_tile_copy(x_hbm, x_buf, sem, i):
    tm = x_buf.shape[0]
    return pltpu.make_async_copy(x_hbm.at[pl.ds(pl.multiple_of(i * tm, tm), tm), :], x_buf, sem)


def _fetch_and_norm(x_hbm, x_buf, sem, g_ref, h_scr, prefetch_step):
    i, j = pl.program_id(0), pl.program_id(1)

    @pl.when(j == 0)
    def _():
        @pl.when(i == 0)
        def _():
            _row_tile_copy(x_hbm, x_buf, sem, 0).start()

        _row_tile_copy(x_hbm, x_buf, sem, i).wait()
        _rms_rows_to(h_scr, x_buf, g_ref, 128)

    @pl.when((j == prefetch_step) & (i + 1 < pl.num_programs(0)))
    def _():
        _row_tile_copy(x_hbm, x_buf, sem, i + 1).start()


def _norm_matmul_kernel(x_hbm, g_ref, w_ref, *rest, has_extra, n_head_tiles):
    rest = list(rest)
    wt_ref = rest.pop(0) if n_head_tiles is not None else None
    if has_extra:
        wx_ref, o_ref, ox_ref, h_scr, x_buf, sem = rest
    else:
        o_ref, h_scr, x_buf, sem = rest
    j = pl.program_id(1)
    _fetch_and_norm(x_hbm, x_buf, sem, g_ref, h_scr, prefetch_step=1)

    if has_extra:
        @pl.when(j == 0)
        def _():
            ox_ref[...] = _dot(h_scr[...], wx_ref[...].astype(BF16))

    if wt_ref is None:
        o_ref[...] = _dot(h_scr[...], w_ref[...].astype(BF16))
    else:
        @pl.when(j < n_head_tiles)
        def _():
            o_ref[...] = _dot(h_scr[...], w_ref[...].astype(BF16))

        @pl.when(j >= n_head_tiles)
        def _():
            o_ref[...] = _dot(h_scr[...], wt_ref[...].astype(BF16))


def norm_matmul(x, gain, w, layer, n_cols, w_extra=None, w_tail=None, tm=2048, tn=512):
    m, d = x.shape
    tm = min(tm, m)
    has_extra = w_extra is not None
    n_head = n_cols // tn
    n_tail = 0 if w_tail is None else w_tail.shape[1] // tn
    in_specs = [pl.BlockSpec(memory_space=pl.ANY),
                pl.BlockSpec((1, d), lambda i, j: (0, 0)),
                pl.BlockSpec((None, d, tn), lambda i, j: (layer, 0, jnp.minimum(j, n_head - 1)))]
    out_shape = [jax.ShapeDtypeStruct((m, (n_head + n_tail) * tn), F32)]
    out_specs = [pl.BlockSpec((tm, tn), lambda i, j: (i, j))]
    args = [x, gain.reshape(1, d), w]
    if w_tail is not None:
        in_specs.append(pl.BlockSpec((d, tn), lambda i, j: (0, jnp.maximum(j - n_head, 0))))
        args.append(w_tail)
    if has_extra:
        in_specs.append(pl.BlockSpec((d, LANES), lambda i, j: (0, 0)))
        out_shape.append(jax.ShapeDtypeStruct((m, LANES), F32))
        out_specs.append(pl.BlockSpec((tm, LANES), lambda i, j: (i, 0)))
        args.append(w_extra)
    return pl.pallas_call(
        functools.partial(_norm_matmul_kernel, has_extra=has_extra,
                          n_head_tiles=None if w_tail is None else n_head),
        grid=(m // tm, n_head + n_tail),
        in_specs=in_specs, out_specs=out_specs, out_shape=out_shape,
        scratch_shapes=[pltpu.VMEM((tm, d), BF16), pltpu.VMEM((tm, d), F32), pltpu.SemaphoreType.DMA(())],
        compiler_params=_params(("arbitrary", "arbitrary")),
        name="norm_matmul",
    )(*args)


def _matmul_residual_kernel(a_ref, w_ref, r_ref, o_ref):
    o_ref[...] = r_ref[...] + _dot(a_ref[...], w_ref[...].astype(BF16))


def matmul_residual(a, w, layer, res, tm=2048, tn=512):
    m, k = a.shape
    tm = min(tm, m)
    n = w.shape[2]
    return pl.pallas_call(
        _matmul_residual_kernel,
        grid=(m // tm, n // tn),
        in_specs=[pl.BlockSpec((tm, k), lambda i, j: (i, 0)),
                  pl.BlockSpec((None, k, tn), lambda i, j: (layer, 0, j)),
                  pl.BlockSpec((tm, tn), lambda i, j: (i, j))],
        out_specs=pl.BlockSpec((tm, tn), lambda i, j: (i, j)),
        out_shape=jax.ShapeDtypeStruct((m, n), F32),
        compiler_params=_params(("parallel", "parallel")),
        name="matmul_residual",
    )(a, w, res)


def _mlp_kernel(x_hbm, g_ref, wu_ref, xt_ref, wd_ref, o_ref, h_scr, u_scr, x_buf, sem, *, nf, tf, splits):
    j = pl.program_id(1)
    d = h_scr.shape[1]
    per = nf // splits
    _fetch_and_norm(x_hbm, x_buf, sem, g_ref, h_scr, prefetch_step=nf)

    @pl.when(j < nf)
    def _():
        u = _dot(h_scr[:, 0:tf], wu_ref[0:tf, :].astype(BF16))
        for c in range(1, d // tf):
            u = u + _dot(h_scr[:, c * tf:(c + 1) * tf], wu_ref[c * tf:(c + 1) * tf, :].astype(BF16))
        u = jnp.maximum(u, 0.0)
        u_scr[j] = (u * u).astype(BF16)

    for part in range(splits):
        @pl.when((j >= nf) & ((j - nf) % splits == part))
        def _(part=part):
            acc = xt_ref[...] if part == 0 else o_ref[...]
            for f in range(per):
                acc = acc + _dot(u_scr[part * per + f], wd_ref[f * tf:(f + 1) * tf, :].astype(BF16))
            o_ref[...] = acc


def mlp_block(x, gain, w_up, w_down, layer, tm=1024, tf=512, tn=512, splits=2):
    m, d = x.shape
    ff = w_up.shape[2]
    nf, nn = ff // tf, d // tn
    down = lambda j: jnp.maximum(j - nf, 0)
    return pl.pallas_call(
        functools.partial(_mlp_kernel, nf=nf, tf=tf, splits=splits),
        grid=(m // tm, nf + nn * splits),
        in_specs=[pl.BlockSpec(memory_space=pl.ANY),
                  pl.BlockSpec((1, d), lambda i, j: (0, 0)),
                  pl.BlockSpec((None, d, tf), lambda i, j: (layer, 0, jnp.minimum(j, nf - 1))),
                  pl.BlockSpec((tm, tn), lambda i, j: (i, down(j) // splits)),
                  pl.BlockSpec((None, ff // splits, tn), lambda i, j: (layer, down(j) % splits, down(j) // splits))],
        out_specs=pl.BlockSpec((tm, tn), lambda i, j: (i, down(j) // splits)),
        out_shape=jax.ShapeDtypeStruct((m, d), F32),
        scratch_shapes=[pltpu.VMEM((tm, d), BF16), pltpu.VMEM((nf, tm, tf), BF16), pltpu.VMEM((tm, d), F32),
                        pltpu.SemaphoreType.DMA(())],
        compiler_params=_params(("arbitrary", "arbitrary"), MLP_VMEM_LIMIT),
        name="mlp_block",
    )(x, gain.reshape(1, d), w_up, x, w_down)


def rope_tables(pos, head_dim):
    rot = head_dim // ROPE_FRACTION
    half = rot // 2
    inv_freq = jnp.power(jnp.float32(ROPE_THETA), -jnp.arange(half, dtype=F32) / half)
    ang = pos.astype(F32)[:, None] * inv_freq[None, :]
    cos, sin = jnp.cos(ang), jnp.sin(ang)
    n = pos.shape[0]
    zeros = jnp.zeros((n, head_dim - rot), F32)
    c = jnp.concatenate([cos, cos, jnp.ones((n, head_dim - rot), F32)], axis=-1)
    s1 = jnp.concatenate([-sin, jnp.zeros((n, half), F32), zeros], axis=-1)
    s2 = jnp.concatenate([jnp.zeros((n, half), F32), sin, zeros], axis=-1)
    reps = LANES // head_dim
    return tuple(jnp.tile(t, (1, reps)) for t in (c, s1, s2))


def _norm_rope(x, gain, c, s1, s2, half, ms=None):
    if ms is None:
        ms = jnp.mean(x * x, axis=-1, keepdims=True)
    y = x * lax.rsqrt(ms + NORM_EPS) * gain
    return y * c + pltpu.roll(y, LANES - half, 1) * s1 + pltpu.roll(y, half, 1) * s2


def _gelu_tanh(x):
    return 0.5 * x * (1.0 + jnp.tanh(np.sqrt(2.0 / np.pi) * (x + 0.044715 * (x * x * x))))


def _nsa_compress_kernel(x_ref, pe_ref, w1_ref, w2_ref, g_ref, c_ref, s1_ref, s2_ref, o_ref):
    n_rows = x_ref.shape[0] // NSA_CMP_STRIDE
    d = NSA_HEAD_DIM
    y0 = jnp.zeros((n_rows, d), F32)
    y1 = jnp.zeros((n_rows, d), F32)
    for r in range(NSA_CMP_STRIDE):
        xr = x_ref[pl.ds(r, n_rows, stride=NSA_CMP_STRIDE), :]
        a0 = (xr + pe_ref[r:r + 1, :]).astype(BF16)
        a1 = (xr + pe_ref[NSA_CMP_STRIDE + r:NSA_CMP_STRIDE + r + 1, :]).astype(BF16)
        y0 = y0 + _dot(a0, w1_ref[r * d:(r + 1) * d, :].astype(BF16))
        y1 = y1 + _dot(a1, w1_ref[(NSA_CMP_STRIDE + r) * d:(NSA_CMP_STRIDE + r + 1) * d, :].astype(BF16))
    pre = y0 + pltpu.roll(y1, n_rows - 1, 0)
    out = _dot(_gelu_tanh(pre).astype(BF16), w2_ref[...].astype(BF16))
    roped = _norm_rope(out, g_ref[...], c_ref[...], s1_ref[...], s2_ref[...],
                       NSA_HEAD_DIM // ROPE_FRACTION // 2, None)
    o_ref[...] = jnp.where(pl.program_id(0) == 0, roped * NSA_SCALE2, out).astype(BF16)


def nsa_compress(proj, pe, w1, w2, k_gain, tables):
    b, s, _ = proj.shape
    g, d = NSA_KV_GROUPS, NSA_HEAD_DIM
    n_rows = s // NSA_CMP_STRIDE
    col_blk = NSA_HEADS
    tab_spec = pl.BlockSpec((n_rows, d), lambda kv, bi, gi: (0, 0))
    return pl.pallas_call(
        _nsa_compress_kernel,
        grid=(2, b, g),
        in_specs=[pl.BlockSpec((None, s, d), lambda kv, bi, gi: (bi, 0, col_blk + kv * g + gi)),
                  pl.BlockSpec((None, NSA_CMP_BLOCK, d), lambda kv, bi, gi: (kv, 0, 0)),
                  pl.BlockSpec((None, NSA_CMP_BLOCK * d, d), lambda kv, bi, gi: (kv, 0, 0)),
                  pl.BlockSpec((None, d, d), lambda kv, bi, gi: (kv, 0, 0)),
                  pl.BlockSpec((1, d), lambda kv, bi, gi: (0, 0)),
                  tab_spec, tab_spec, tab_spec],
        out_specs=pl.BlockSpec((None, None, None, n_rows, d), lambda kv, bi, gi: (kv, bi, gi, 0, 0)),
        out_shape=jax.ShapeDtypeStruct((2, b, g, n_rows, d), BF16),
        compiler_params=_params(("parallel", "parallel", "parallel")),
        name="nsa_compress",
    )(proj, pe, w1, w2, k_gain.reshape(1, d), *tables)


def _nsa_attn_kernel(q_ref, kc_ref, vc_ref, ksr_ref, vsr_ref, kwr_ref, vwr_ref, gate_ref, ovt_ref,
                     qg_ref, kg_ref, c_ref, s1_ref, s2_ref, o_ref, ks_ref, vs_ref, kw_ref, vw_ref, q_scr, score_ref, rank_ref,
                     *, tq, tk, seq):
    gi = pl.program_id(1)
    qi = pl.program_id(2)
    d = NSA_HEAD_DIM
    hpg = NSA_HPG
    n_sel = seq // NSA_SEL_BLOCK
    n_top = min(NSA_TOP_N, n_sel)
    half = d // ROPE_FRACTION // 2

    def tables(rows):
        return c_ref[rows, :], s1_ref[rows, :], s2_ref[rows, :]

    scale2 = NSA_SCALE2

    @pl.when(qi == 0)
    def _():
        chunk = 512
        lane = lax.broadcasted_iota(jnp.int32, (chunk, d), 1)
        row = lax.broadcasted_iota(jnp.int32, (chunk, d), 0)

        def body(i, carry):
            r0 = pl.multiple_of(i * chunk, chunk)
            rows = pl.ds(r0, chunk)
            ks = _norm_rope(ksr_ref[rows, :], kg_ref[0:1, :], *tables(rows), half, None) * scale2
            kw = _norm_rope(kwr_ref[rows, :], kg_ref[1:2, :], *tables(rows), half, None) * scale2
            ks_ref[rows, 0:d] = ks.astype(BF16)
            ks_ref[rows, d:2 * d] = jnp.where((r0 + row) // NSA_SEL_BLOCK == lane, NSA_MASK_NEG, 0.0).astype(BF16)
            kw_ref[rows, :] = kw.astype(BF16)
            vs_ref[rows, :] = vsr_ref[rows, :].astype(BF16)
            vw_ref[rows, :] = vwr_ref[rows, :].astype(BF16)
            return carry

        lax.fori_loop(0, seq // chunk, body, 0)

    q_rows = 128

    def q_body(i, carry):
        r0 = pl.multiple_of(i * q_rows, q_rows)
        tabs = tables(pl.ds(pl.multiple_of(qi * tq, tq) + r0, q_rows))
        for h in range(hpg):
            y = _norm_rope(q_ref[pl.ds(r0, q_rows), h * d:(h + 1) * d], qg_ref[...], *tabs, half, None)
            q_scr[pl.ds(h * tq + r0, q_rows), :] = y.astype(BF16)
        return carry

    lax.fori_loop(0, tq // q_rows, q_body, 0)
    q4 = q_scr[...]
    pos = qi * tq + lax.broadcasted_iota(jnp.int32, (tq, 1), 0)
    pos_l = qi * tq + lax.broadcasted_iota(jnp.int32, (1, tq), 1)

    def masked_scores(qs, k, bias=None):
        sc = _dot(qs, k, NT_DIMS)
        sc = sc.reshape(-1, tq, sc.shape[-1])
        return sc if bias is None else sc + bias[None]

    def stack(t):
        return t.reshape(t.shape[0] * tq, -1)

    span = NSA_WINDOW + tq
    w0 = pl.multiple_of(jnp.clip(qi * tq - NSA_WINDOW, 0, seq - span), tq)
    vw = vw_ref[pl.ds(w0, span), :]
    kp = w0 + lax.broadcasted_iota(jnp.int32, (1, span), 1)
    sw = masked_scores(q4, kw_ref[pl.ds(w0, span), :],
                       jnp.where((kp <= pos) & (kp > pos - NSA_WINDOW), 0.0, NEG_INF))
    pw = jnp.exp2(sw - jnp.max(sw, axis=-1, keepdims=True))
    o_win = _dot(stack(pw).astype(BF16), vw) / stack(jnp.sum(pw, axis=-1, keepdims=True))

    n_c = kc_ref.shape[0]
    cend = lax.broadcasted_iota(jnp.int32, (1, n_c), 1) * NSA_CMP_STRIDE + (NSA_CMP_BLOCK - 1)
    s = masked_scores(q4, kc_ref[...], jnp.where(cend <= pos, 0.0, NEG_INF))
    p = jnp.exp2(s - jnp.max(s, axis=-1, keepdims=True))
    p = p / jnp.sum(p, axis=-1, keepdims=True)
    p = jnp.where((pos >= NSA_CMP_BLOCK - 1)[None], p, 0.0)
    o_cmp = _dot(stack(p).astype(BF16), vc_ref[...])
    p_sum = jnp.sum(p, axis=0)
    imp_t = _dot(ovt_ref[...], p_sum, NT_DIMS, precision=lax.Precision.HIGHEST)

    j = lax.broadcasted_iota(jnp.int32, (n_sel, 1), 0)
    bq = pos_l // NSA_SEL_BLOCK
    forced = (j == 0) | (j == bq) | (j == bq - 1)
    score = jnp.where(j <= bq, imp_t + jnp.where(forced, NSA_FORCE_BONUS, 0.0), NEG_INF)
    sub = 8
    n_grp = n_sel // sub
    score_ref[...] = score
    rank_ref[...] = jnp.zeros_like(rank_ref)
    j_loc = lax.broadcasted_iota(jnp.int32, (sub, tq), 0)
    last_blk = (qi * tq + tq - 1) // NSA_SEL_BLOCK
    for gp in range(n_grp):
        @pl.when(gp * sub <= last_blk)
        def _(gp=gp):
            groups = [score_ref[v * sub:(v + 1) * sub, :] for v in range(n_grp)]
            ranks = [rank_ref[v * sub:(v + 1) * sub, :] for v in range(n_grp)]
            for jp in range(gp * sub, (gp + 1) * sub):
                row = score_ref[jp:jp + 1, :]
                for v, sg in enumerate(groups):
                    if v * sub > jp:
                        beats = row >= sg
                    elif (v + 1) * sub - 1 <= jp:
                        beats = row > sg
                    else:
                        beats = (row > sg) | ((row == sg) & (j_loc > jp - v * sub))
                    ranks[v] = ranks[v] + jnp.where(beats, 1.0, 0.0)
            for v in range(n_grp):
                rank_ref[v * sub:(v + 1) * sub, :] = ranks[v]
    not_sel = jnp.where(rank_ref[...] < n_top, 0.0, 1.0)
    not_sel = jnp.concatenate([not_sel, jnp.zeros((LANES - n_sel, tq), F32)], axis=0).T.astype(BF16)

    q_aug = jnp.concatenate([q4, jnp.concatenate([not_sel] * hpg, axis=0)], axis=1)

    def sel_tile(kt, carry, bias):
        m, l, acc = carry
        k0 = pl.multiple_of(kt * tk, tk)
        sc = masked_scores(q_aug, ks_ref[pl.ds(k0, tk), :], bias)
        m_new = jnp.maximum(m, jnp.max(sc, axis=-1, keepdims=True))
        alpha = jnp.exp2(m - m_new)
        pe = jnp.exp2(sc - m_new)
        l = alpha * l + jnp.sum(pe, axis=-1, keepdims=True)
        acc = stack(alpha) * acc + _dot(stack(pe).astype(BF16), vs_ref[pl.ds(k0, tk), :])
        return m_new, l, acc

    last = (qi * tq + tq - 1) // tk
    init = (jnp.full((hpg, tq, 1), NEG_INF, F32), jnp.zeros((hpg, tq, 1), F32), jnp.zeros((hpg * tq, d), F32))
    pairs = last // 2
    carry = lax.fori_loop(0, pairs, lambda kp_, c: sel_tile(2 * kp_ + 1, sel_tile(2 * kp_, c, None), None), init)
    carry = lax.fori_loop(2 * pairs, last, lambda kt, c: sel_tile(kt, c, None), carry)
    key_pos = last * tk + lax.broadcasted_iota(jnp.int32, (1, tk), 1)
    _, l_sel, acc_sel = sel_tile(last, carry, jnp.where(key_pos <= pos, 0.0, NEG_INF))
    o_sel = acc_sel / stack(l_sel)

    gt = jax.nn.sigmoid(gate_ref[...])
    lane = lax.broadcasted_iota(jnp.int32, (1, LANES), 1)
    for h in range(hpg):
        rows = slice(h * tq, (h + 1) * tq)
        o_h = jnp.zeros((tq, d), F32)
        for br, o_br in enumerate((o_cmp, o_sel, o_win)):
            g_col = jnp.sum(jnp.where(lane == gi * (3 * hpg) + h * 3 + br, gt, 0.0), axis=-1, keepdims=True)
            o_h = o_h + g_col * o_br[rows]
        o_ref[:, h * d:(h + 1) * d] = o_h.astype(BF16)


def selection_overlap(n_rows, n_sel):
    n_cmp = n_rows - 1
    c0 = np.arange(n_rows) * NSA_CMP_STRIDE
    s0 = np.arange(n_sel) * NSA_SEL_BLOCK
    ov = np.minimum(c0[:, None] + NSA_CMP_BLOCK, s0[None, :] + NSA_SEL_BLOCK) - np.maximum(c0[:, None], s0[None, :])
    ov = np.clip(ov, 0, None) / NSA_CMP_BLOCK
    ov[n_cmp:] = 0.0
    return jnp.asarray(ov, dtype=F32)


def nsa_attention(proj, kv_cmp, gate, q_gain, k_gains, tabs, tq=256, tk=512):
    b, s, _ = proj.shape
    g, d, hpg = NSA_KV_GROUPS, NSA_HEAD_DIM, NSA_HPG
    n_rows = s // NSA_CMP_STRIDE
    n_sel = s // NSA_SEL_BLOCK
    seg_blk = lambda i: NSA_HEADS + i * g
    cmp_spec = lambda kv: pl.BlockSpec((None, None, None, n_rows, d), lambda bi, gi, qi: (kv, bi, gi, 0, 0))
    seq_spec = lambda i: pl.BlockSpec((None, s, d), lambda bi, gi, qi: (bi, 0, seg_blk(i) + gi))
    const = lambda shape: pl.BlockSpec(shape, lambda bi, gi, qi: (0, 0))
    return pl.pallas_call(
        functools.partial(_nsa_attn_kernel, tq=tq, tk=tk, seq=s),
        grid=(b, g, s // tq),
        in_specs=[pl.BlockSpec((None, tq, hpg * d), lambda bi, gi, qi: (bi, qi, gi)),
                  cmp_spec(0), cmp_spec(1),
                  seq_spec(2), seq_spec(3), seq_spec(4), seq_spec(5),
                  pl.BlockSpec((None, tq, LANES), lambda bi, gi, qi: (bi, qi, 0)),
                  const((n_sel, n_rows)), const((1, d)), const((2, d)),
                  const((s, d)), const((s, d)), const((s, d))],
        out_specs=pl.BlockSpec((None, tq, hpg * d), lambda bi, gi, qi: (bi, qi, gi)),
        out_shape=jax.ShapeDtypeStruct((b, s, NSA_HEADS * d), BF16),
        scratch_shapes=[pltpu.VMEM((s, 2 * d), BF16)] + [pltpu.VMEM((s, d), BF16)] * 3
                       + [pltpu.VMEM((hpg * tq, d), BF16)] + [pltpu.VMEM((n_sel, tq), F32)] * 2,
        compiler_params=_params(("parallel", "parallel", "arbitrary")),
        name="nsa_attention",
    )(proj, kv_cmp, kv_cmp, proj, proj, proj, proj, gate, selection_overlap(n_rows, n_sel).T,
      q_gain.reshape(1, d), k_gains, *tabs)


def nsa_mixer(x2, b, s, gain, w_in, w_out, layer, q_norm, k_norm, cmp_pe, cmp_w1, cmp_w2):
    d, g = NSA_HEAD_DIM, NSA_KV_GROUPS
    n_main = (NSA_HEADS + 6 * g) * d
    w_gate = jnp.pad(w_in[layer, :, n_main:], ((0, 0), (0, LANES - 3 * NSA_HEADS)))
    proj, gate = norm_matmul(x2, gain, w_in, layer, n_main, w_gate)
    proj = proj.reshape(b, s, n_main)
    gate = gate.reshape(b, s, LANES)
    tabs = rope_tables(jnp.arange(s), d)
    cmp_end = jnp.arange(s // NSA_CMP_STRIDE) * NSA_CMP_STRIDE + (NSA_CMP_BLOCK - 1)
    kv_cmp = nsa_compress(proj, cmp_pe, cmp_w1, cmp_w2, k_norm[0], rope_tables(cmp_end, d))
    o = nsa_attention(proj, kv_cmp, gate, q_norm, k_norm[1:3], tabs)
    return matmul_residual(o.reshape(b * s, NSA_HEADS * d), w_out, layer, x2)


def _log_sigmoid(z):
    return jnp.minimum(z, 0.0) - jnp.log1p(jnp.exp(-jnp.abs(z)))


def _gla_kernel(q_ref, k_ref, v_ref, r_ref, glr_ref, wg_ref, bg_ref, on_ref, o_ref, state_ref, *, rows, hps):
    c = GLA_CHUNK
    sub = GLA_SUB
    dk, dv = GLA_KEY_DIM, GLA_VAL_DIM

    @pl.when(pl.program_id(2) == 0)
    def _():
        state_ref[...] = jnp.zeros_like(state_ref)

    tri = jnp.where(lax.broadcasted_iota(jnp.int32, (c, c), 0) >= lax.broadcasted_iota(jnp.int32, (c, c), 1), 1.0, 0.0)
    sub_row = lax.broadcasted_iota(jnp.int32, (sub, 1), 0)

    def head_chunk(hh, r0):
        kcols = slice(hh * dk, (hh + 1) * dk)
        vcols = slice(hh * dv, (hh + 1) * dv)
        qc = q_ref[pl.ds(r0, c), kcols] * (dk ** -0.5)
        kc = k_ref[pl.ds(r0, c), kcols]
        vc = v_ref[pl.ds(r0, c), vcols]
        z = _dot(glr_ref[pl.ds(r0, c), :], wg_ref[:, kcols], precision=lax.Precision.HIGHEST) + bg_ref[:, kcols]
        log_a = _log_sigmoid(z) / GLA_TAU
        cum = _dot(tri, log_a, precision=lax.Precision.HIGHEST)
        vb = vc.astype(BF16)

        state = state_ref[hh]
        o_inter = _dot((qc * jnp.exp(cum)).astype(BF16), state.astype(BF16), NT_DIMS)

        parts = []
        for bi in range(c // sub):
            lo = bi * sub
            q_i, k_i, v_i, c_i = qc[lo:lo + sub], kc[lo:lo + sub], vc[lo:lo + sub], cum[lo:lo + sub]
            o_i = o_inter[lo:lo + sub]
            if bi > 0:
                c0 = c_i[0:1]
                qs = (q_i * jnp.exp(c_i - c0)).astype(BF16)
                kp = (kc[:lo] * jnp.exp(c0 - cum[:lo])).astype(BF16)
                att = _dot(qs, kp, NT_DIMS)
                o_i = o_i + _dot(att.astype(BF16), vb[:lo])
            for jj in range(sub):
                e = jnp.exp(jnp.where(sub_row >= jj, c_i - c_i[jj:jj + 1], NEG_INF))
                col = jnp.sum(q_i * k_i[jj:jj + 1] * e, axis=-1, keepdims=True)
                o_i = o_i + col * v_i[jj:jj + 1]
            parts.append(o_i)
        o = jnp.concatenate(parts, axis=0)

        last = cum[c - 1:c]
        kd = (kc * jnp.exp(last - cum)).astype(BF16)
        state_ref[hh] = state * jnp.exp(last) + _dot(vb, kd, TN_DIMS)

        ms = jnp.mean(o * o, axis=-1, keepdims=True)
        on = o * lax.rsqrt(ms + NORM_EPS) * on_ref[...]
        r = r_ref[pl.ds(r0, c), vcols]
        o_ref[pl.ds(r0, c), vcols] = (on * (r * jax.nn.sigmoid(r))).astype(BF16)

    def chunk(ci, carry):
        r0 = pl.multiple_of(ci * c, c)
        for hh in range(hps):
            head_chunk(hh, r0)
        return carry

    lax.fori_loop(0, rows // c, chunk, 0)


def gla_attention(proj, glr, w_gate_up, b_gate, o_norm, rows=512, hps=4):
    b, s, _ = proj.shape
    nh, dk, dv = GLA_HEADS, GLA_KEY_DIM, GLA_VAL_DIM
    wg = jnp.pad(w_gate_up, ((0, LANES - GLA_GATE_RANK), (0, 0)))
    kw, vw = hps * dk, hps * dv
    k_blk = nh * dk // kw
    v_blk = 2 * nh * dk // vw
    r_blk = (2 * nh * dk + nh * dv) // vw
    return pl.pallas_call(
        functools.partial(_gla_kernel, rows=rows, hps=hps),
        grid=(b, nh // hps, s // rows),
        in_specs=[pl.BlockSpec((None, rows, kw), lambda bi, h, ci: (bi, ci, h)),
                  pl.BlockSpec((None, rows, kw), lambda bi, h, ci: (bi, ci, k_blk + h)),
                  pl.BlockSpec((None, rows, vw), lambda bi, h, ci: (bi, ci, v_blk + h)),
                  pl.BlockSpec((None, rows, vw), lambda bi, h, ci: (bi, ci, r_blk + h)),
                  pl.BlockSpec((None, rows, LANES), lambda bi, h, ci: (bi, ci, 0)),
                  pl.BlockSpec((LANES, kw), lambda bi, h, ci: (0, h)),
                  pl.BlockSpec((1, kw), lambda bi, h, ci: (0, h)),
                  pl.BlockSpec((1, dv), lambda bi, h, ci: (0, 0))],
        out_specs=pl.BlockSpec((None, rows, vw), lambda bi, h, ci: (bi, ci, h)),
        out_shape=jax.ShapeDtypeStruct((b, s, nh * dv), BF16),
        scratch_shapes=[pltpu.VMEM((hps, dv, dk), F32)],
        compiler_params=_params(("parallel", "parallel", "arbitrary")),
        name="gla_attention",
    )(proj, proj, proj, proj, glr, wg, b_gate.reshape(1, nh * dk), o_norm.reshape(1, dv))


def gla_mixer(x2, b, s, gain, w_in, w_gate_up, b_gate, o_norm, w_out, layer):
    nh, dk, dv = GLA_HEADS, GLA_KEY_DIM, GLA_VAL_DIM
    n_qkv = 2 * nh * dk + nh * dv
    w_r = w_in[layer, :, n_qkv + GLA_GATE_RANK:]
    w_glr = jnp.pad(w_in[layer, :, n_qkv:n_qkv + GLA_GATE_RANK], ((0, 0), (0, LANES - GLA_GATE_RANK)))
    proj, glr = norm_matmul(x2, gain, w_in, layer, n_qkv, w_glr, w_tail=w_r)
    o = gla_attention(proj.reshape(b, s, -1), glr.reshape(b, s, LANES), w_gate_up, b_gate, o_norm)
    return matmul_residual(o.reshape(b * s, nh * dv), w_out, layer, x2)


def _swa_heads(x, gain, tabs, seg, out_scale):
    blk = x.shape[0]
    d = SWA_HEAD_DIM
    n = x.shape[1] // LANES
    tiles = jnp.concatenate([x[:, t * LANES:(t + 1) * LANES] for t in range(n)], axis=0)
    sq = tiles * tiles
    hi = sq.astype(BF16)
    lo = (sq - hi.astype(F32)).astype(BF16)
    ms = (_dot(hi, seg) + _dot(lo, seg)) * (1.0 / d)
    heads = []
    for t in range(n):
        rows = slice(t * blk, (t + 1) * blk)
        y = _norm_rope(tiles[rows], gain, *tabs, d // ROPE_FRACTION // 2, ms[rows]) * out_scale
        heads += [y[:, u * d:(u + 1) * d] for u in range(LANES // d)]
    return heads


def _swa_kernel(q_ref, kp_ref, kc_ref, vp_ref, vc_ref, sink_ref, qg_ref, kg_ref, seg_ref,
                cc_ref, s1c_ref, s2c_ref, cp_ref, s1p_ref, s2p_ref, o_ref, *, blk):
    qi = pl.program_id(1)
    hpg, d, g = SWA_HPG, SWA_HEAD_DIM, SWA_KV_HEADS
    seg = seg_ref[...]
    tabs_c = (cc_ref[...], s1c_ref[...], s2c_ref[...])
    tabs_p = (cp_ref[...], s1p_ref[...], s2p_ref[...])
    q_heads = _swa_heads(q_ref[...], qg_ref[...], tabs_c, seg, d ** -0.5)
    kp_heads = _swa_heads(kp_ref[...], kg_ref[...], tabs_p, seg, 1.0)
    kc_heads = _swa_heads(kc_ref[...], kg_ref[...], tabs_c, seg, 1.0)
    vp, vc = vp_ref[...], vc_ref[...]

    qp = qi * blk + lax.broadcasted_iota(jnp.int32, (blk, 1), 0)
    kpos = (qi - 1) * blk + lax.broadcasted_iota(jnp.int32, (1, 2 * blk), 1)
    bias = jnp.where((kpos <= qp) & (kpos > qp - SWA_WINDOW) & (kpos >= 0), 0.0, NEG_INF)

    for gi in range(g):
        q = jnp.concatenate(q_heads[gi * hpg:(gi + 1) * hpg], axis=0).astype(BF16)
        k = jnp.concatenate([kp_heads[gi], kc_heads[gi]], axis=0).astype(BF16)
        v = jnp.concatenate([vp[:, gi * d:(gi + 1) * d], vc[:, gi * d:(gi + 1) * d]], axis=0).astype(BF16)
        s = _dot(q, k, NT_DIMS).reshape(hpg, blk, 2 * blk) + bias[None]
        sink = sink_ref[gi]
        m = jnp.maximum(jnp.max(s, axis=-1, keepdims=True), sink)
        p = jnp.exp(s - m)
        denom = jnp.sum(p, axis=-1, keepdims=True) + jnp.exp(sink - m)
        o = _dot(p.reshape(hpg * blk, 2 * blk).astype(BF16), v).reshape(hpg, blk, d) / denom
        o_ref[:, gi * hpg * d:(gi + 1) * hpg * d] = jnp.concatenate([o[h] for h in range(hpg)], axis=1).astype(BF16)


def swa_attention(proj, sinks, q_gain, k_gain, tabs, blk=128):
    b, s, _ = proj.shape
    g, hpg, d = SWA_KV_HEADS, SWA_HPG, SWA_HEAD_DIM
    nq, nkv = SWA_HEADS * d, g * d
    k_blk = nq // nkv
    prev = lambda qi: jnp.maximum(qi - 1, 0)
    kv_spec = lambda col, row: pl.BlockSpec((None, blk, nkv), lambda bi, qi: (bi, row(qi), col))
    const = lambda shape: pl.BlockSpec(shape, lambda bi, qi: (0,) * len(shape))
    tab_spec = lambda row: pl.BlockSpec((blk, LANES), lambda bi, qi: (row(qi), 0))
    cur = lambda qi: qi
    seg_id = np.arange(LANES) // d
    seg = jnp.asarray(seg_id[:, None] == seg_id[None, :], dtype=BF16)
    tile_gain = lambda gn: jnp.tile(gn.reshape(1, d), (1, LANES // d))
    return pl.pallas_call(
        functools.partial(_swa_kernel, blk=blk),
        grid=(b, s // blk),
        in_specs=[pl.BlockSpec((None, blk, nq), lambda bi, qi: (bi, qi, 0)),
                  kv_spec(k_blk, prev), kv_spec(k_blk, cur), kv_spec(k_blk + 1, prev), kv_spec(k_blk + 1, cur),
                  const((g, hpg, 1, 1)), const((1, LANES)), const((1, LANES)), const((LANES, LANES)),
                  tab_spec(cur), tab_spec(cur), tab_spec(cur), tab_spec(prev), tab_spec(prev), tab_spec(prev)],
        out_specs=pl.BlockSpec((None, blk, nq), lambda bi, qi: (bi, qi, 0)),
        out_shape=jax.ShapeDtypeStruct((b, s, nq), BF16),
        compiler_params=_params(("parallel", "parallel")),
        name="swa_attention",
    )(proj, proj, proj, proj, proj, sinks.astype(F32).reshape(g, hpg, 1, 1), tile_gain(q_gain), tile_gain(k_gain),
      seg, *tabs, *tabs)


def swa_mixer(x2, b, s, gain, w_in, w_out, layer, q_norm, k_norm, sinks):
    d, g = SWA_HEAD_DIM, SWA_KV_HEADS
    n_in = (SWA_HEADS + 2 * g) * d
    (proj,) = norm_matmul(x2, gain, w_in, layer, n_in)
    o = swa_attention(proj.reshape(b, s, n_in), sinks, q_norm, k_norm, rope_tables(jnp.arange(s), d))
    return matmul_residual(o.reshape(b * s, SWA_HEADS * d), w_out, layer, x2)


def kernel(x, norm_mix, norm_mlp, mlp_w_up, mlp_w_down, nsa_w_in, nsa_w_out, nsa_q_norm, nsa_k_norm, nsa_cmp_pe, nsa_cmp_w1, nsa_cmp_w2, gla_w_in, gla_w_gate_up, gla_b_gate, gla_o_norm, gla_w_out, swa_w_in, swa_w_out, swa_q_norm, swa_k_norm, swa_sinks):
    b, s, d = x.shape
    x2 = x.reshape(b * s, d)
    ia = ib = ic = 0
    for i in range(norm_mix.shape[0]):
        kind = i % N_MIXERS
        if kind == 0:
            x2 = nsa_mixer(x2, b, s, norm_mix[i], nsa_w_in, nsa_w_out, ia, nsa_q_norm[ia], nsa_k_norm[ia],
                           nsa_cmp_pe[ia], nsa_cmp_w1[ia], nsa_cmp_w2[ia])
            ia += 1
        elif kind == 1:
            x2 = gla_mixer(x2, b, s, norm_mix[i], gla_w_in, gla_w_gate_up[ib], gla_b_gate[ib],
                           gla_o_norm[ib], gla_w_out, ib)
            ib += 1
        else:
            x2 = swa_mixer(x2, b, s, norm_mix[i], swa_w_in, swa_w_out, ic, swa_q_norm[ic], swa_k_norm[ic],
                           swa_sinks[ic])
            ic += 1
        x2 = mlp_block(x2, norm_mlp[i], mlp_w_up, mlp_w_down, i)
    return x2.reshape(b, s, d)
```

```python
import functools

import numpy as np
import jax
import jax.numpy as jnp
from jax import lax
from jax.experimental import pallas as pl
from jax.experimental.pallas import tpu as pltpu

F32 = jnp.float32
BF16 = jnp.bfloat16

NORM_EPS = 1e-6
ROPE_THETA = 500000.0
ROPE_FRACTION = 4
NEG_INF = -1e30
N_MIXERS = 3

NSA_HEAD_DIM = 128
NSA_HEADS = 16
NSA_KV_GROUPS = 4
NSA_HPG = NSA_HEADS // NSA_KV_GROUPS
NSA_CMP_BLOCK = 32
NSA_CMP_STRIDE = 16
NSA_SEL_BLOCK = 64
NSA_TOP_N = 16
NSA_WINDOW = 512
NSA_FORCE_BONUS = 1e4
NSA_SCALE2 = float(NSA_HEAD_DIM ** -0.5 * np.log2(np.e))
NSA_MASK_NEG = -2.0 ** 100

GLA_HEADS = 4
GLA_KEY_DIM = 256
GLA_VAL_DIM = 512
GLA_GATE_RANK = 16
GLA_TAU = 16.0
GLA_CHUNK = 64
GLA_SUB = 16

SWA_HEAD_DIM = 64
SWA_HEADS = 32
SWA_KV_HEADS = 4
SWA_HPG = SWA_HEADS // SWA_KV_HEADS
SWA_WINDOW = 128

LANES = 128
VMEM_LIMIT = 56 * 1024 * 1024
MLP_VMEM_LIMIT = 62 * 1024 * 1024

NT_DIMS = (((1,), (1,)), ((), ()))
TN_DIMS = (((0,), (0,)), ((), ()))


def _params(sem, vmem_limit=VMEM_LIMIT):
    return pltpu.CompilerParams(dimension_semantics=sem, vmem_limit_bytes=vmem_limit)


def _dot(a, b, dims=None, precision=None):
    if dims is None:
        return jnp.dot(a, b, preferred_element_type=F32, precision=precision)
    return lax.dot_general(a, b, dims, preferred_element_type=F32, precision=precision)


def _rms_rows_to(h_scr, x_ref, g_ref, rows):
    n = x_ref.shape[0] // rows

    def body(i, c):
        r0 = pl.multiple_of(i * rows, rows)
        x = x_ref[pl.ds(r0, rows), :]
        ms = jnp.mean(x * x, axis=-1, keepdims=True)
        h_scr[pl.ds(r0, rows), :] = (x * lax.rsqrt(ms + NORM_EPS) * g_ref[...]).astype(BF16)
        return c

    lax.fori_loop(0, n, body, 0)


def _row_tile_copy(x_hbm, x_buf, sem, i):
    tm = x_buf.shape[0]
    return pltpu.make_async_copy(x_hbm.at[pl.ds(pl.multiple_of(i * tm, tm), tm), :], x_buf, sem)


def _fetch_and_norm(x_hbm, x_buf, sem, g_ref, h_scr, prefetch_step):
    i, j = pl.program_id(0), pl.program_id(1)

    @pl.when(j == 0)
    def _():
        @pl.when(i == 0)
        def _():
            _row_tile_copy(x_hbm, x_buf, sem, 0).start()

        _row_tile_copy(x_hbm, x_buf, sem, i).wait()
        _rms_rows_to(h_scr, x_buf, g_ref, 128)

    @pl.when((j == prefetch_step) & (i + 1 < pl.num_programs(0)))
    def _():
        _row_tile_copy(x_hbm, x_buf, sem, i + 1).start()


def _norm_matmul_kernel(x_hbm, g_ref, w_ref, *rest, has_extra, n_head_tiles):
    rest = list(rest)
    wt_ref = rest.pop(0) if n_head_tiles is not None else None
    if has_extra:
        wx_ref, o_ref, ox_ref, h_scr, x_buf, sem = rest
    else:
        o_ref, h_scr, x_buf, sem = rest
    j = pl.program_id(1)
    _fetch_and_norm(x_hbm, x_buf, sem, g_ref, h_scr, prefetch_step=1)

    if has_extra:
        @pl.when(j == 0)
        def _():
            ox_ref[...] = _dot(h_scr[...], wx_ref[...].astype(BF16))

    if wt_ref is None:
        o_ref[...] = _dot(h_scr[...], w_ref[...].astype(BF16))
    else:
        @pl.when(j < n_head_tiles)
        def _():
            o_ref[...] = _dot(h_scr[...], w_ref[...].astype(BF16))

        @pl.when(j >= n_head_tiles)
        def _():
            o_ref[...] = _dot(h_scr[...], wt_ref[...].astype(BF16))


def norm_matmul(x, gain, w, layer, n_cols, w_extra=None, w_tail=None, tm=2048, tn=512):
    m, d = x.shape
    tm = min(tm, m)
    has_extra = w_extra is not None
    n_head = n_cols // tn
    n_tail = 0 if w_tail is None else w_tail.shape[1] // tn
    in_specs = [pl.BlockSpec(memory_space=pl.ANY),
                pl.BlockSpec((1, d), lambda i, j: (0, 0)),
                pl.BlockSpec((None, d, tn), lambda i, j: (layer, 0, jnp.minimum(j, n_head - 1)))]
    out_shape = [jax.ShapeDtypeStruct((m, (n_head + n_tail) * tn), F32)]
    out_specs = [pl.BlockSpec((tm, tn), lambda i, j: (i, j))]
    args = [x, gain.reshape(1, d), w]
    if w_tail is not None:
        in_specs.append(pl.BlockSpec((d, tn), lambda i, j: (0, jnp.maximum(j - n_head, 0))))
        args.append(w_tail)
    if has_extra:
        in_specs.append(pl.BlockSpec((d, LANES), lambda i, j: (0, 0)))
        out_shape.append(jax.ShapeDtypeStruct((m, LANES), F32))
        out_specs.append(pl.BlockSpec((tm, LANES), lambda i, j: (i, 0)))
        args.append(w_extra)
    return pl.pallas_call(
        functools.partial(_norm_matmul_kernel, has_extra=has_extra,
                          n_head_tiles=None if w_tail is None else n_head),
        grid=(m // tm, n_head + n_tail),
        in_specs=in_specs, out_specs=out_specs, out_shape=out_shape,
        scratch_shapes=[pltpu.VMEM((tm, d), BF16), pltpu.VMEM((tm, d), F32), pltpu.SemaphoreType.DMA(())],
        compiler_params=_params(("arbitrary", "arbitrary")),
        name="norm_matmul",
    )(*args)


def _matmul_residual_kernel(a_ref, w_ref, r_ref, o_ref):
    o_ref[...] = r_ref[...] + _dot(a_ref[...], w_ref[...].astype(BF16))


def matmul_residual(a, w, layer, res, tm=2048, tn=512):
    m, k = a.shape
    tm = min(tm, m)
    n = w.shape[2]
    return pl.pallas_call(
        _matmul_residual_kernel,
        grid=(m // tm, n // tn),
        in_specs=[pl.BlockSpec((tm, k), lambda i, j: (i, 0)),
                  pl.BlockSpec((None, k, tn), lambda i, j: (layer, 0, j)),
                  pl.BlockSpec((tm, tn), lambda i, j: (i, j))],
        out_specs=pl.BlockSpec((tm, tn), lambda i, j: (i, j)),
        out_shape=jax.ShapeDtypeStruct((m, n), F32),
        compiler_params=_params(("parallel", "parallel")),
        name="matmul_residual",
    )(a, w, res)


def _mlp_kernel(x_hbm, g_ref, wu_ref, xt_ref, wd_ref, o_ref, h_scr, u_scr, x_buf, sem, *, nf, tf, splits):
    j = pl.program_id(1)
    d = h_scr.shape[1]
    per = nf // splits
    _fetch_and_norm(x_hbm, x_buf, sem, g_ref, h_scr, prefetch_step=nf)

    @pl.when(j < nf)
    def _():
        u = _dot(h_scr[:, 0:tf], wu_ref[0:tf, :].astype(BF16))
        for c in range(1, d // tf):
            u = u + _dot(h_scr[:, c * tf:(c + 1) * tf], wu_ref[c * tf:(c + 1) * tf, :].astype(BF16))
        u = jnp.maximum(u, 0.0)
        u_scr[j] = (u * u).astype(BF16)

    for part in range(splits):
        @pl.when((j >= nf) & ((j - nf) % splits == part))
        def _(part=part):
            acc = xt_ref[...] if part == 0 else o_ref[...]
            for f in range(per):
                acc = acc + _dot(u_scr[part * per + f], wd_ref[f * tf:(f + 1) * tf, :].astype(BF16))
            o_ref[...] = acc


def mlp_block(x, gain, w_up, w_down, layer, tm=1024, tf=512, tn=512, splits=2):
    m, d = x.shape
    ff = w_up.shape[2]
    nf, nn = ff // tf, d // tn
    down = lambda j: jnp.maximum(j - nf, 0)
    return pl.pallas_call(
        functools.partial(_mlp_kernel, nf=nf, tf=tf, splits=splits),
        grid=(m // tm, nf + nn * splits),
        in_specs=[pl.BlockSpec(memory_space=pl.ANY),
                  pl.BlockSpec((1, d), lambda i, j: (0, 0)),
                  pl.BlockSpec((None, d, tf), lambda i, j: (layer, 0, jnp.minimum(j, nf - 1))),
                  pl.BlockSpec((tm, tn), lambda i, j: (i, down(j) // splits)),
                  pl.BlockSpec((None, ff // splits, tn), lambda i, j: (layer, down(j) % splits, down(j) // splits))],
        out_specs=pl.BlockSpec((tm, tn), lambda i, j: (i, down(j) // splits)),
        out_shape=jax.ShapeDtypeStruct((m, d), F32),
        scratch_shapes=[pltpu.VMEM((tm, d), BF16), pltpu.VMEM((nf, tm, tf), BF16), pltpu.VMEM((tm, d), F32),
                        pltpu.SemaphoreType.DMA(())],
        compiler_params=_params(("arbitrary", "arbitrary"), MLP_VMEM_LIMIT),
        name="mlp_block",
    )(x, gain.reshape(1, d), w_up, x, w_down)


def rope_tables(pos, head_dim):
    rot = head_dim // ROPE_FRACTION
    half = rot // 2
    inv_freq = jnp.power(jnp.float32(ROPE_THETA), -jnp.arange(half, dtype=F32) / half)
    ang = pos.astype(F32)[:, None] * inv_freq[None, :]
    cos, sin = jnp.cos(ang), jnp.sin(ang)
    n = pos.shape[0]
    zeros = jnp.zeros((n, head_dim - rot), F32)
    c = jnp.concatenate([cos, cos, jnp.ones((n, head_dim - rot), F32)], axis=-1)
    s1 = jnp.concatenate([-sin, jnp.zeros((n, half), F32), zeros], axis=-1)
    s2 = jnp.concatenate([jnp.zeros((n, half), F32), sin, zeros], axis=-1)
    reps = LANES // head_dim
    return tuple(jnp.tile(t, (1, reps)) for t in (c, s1, s2))


def _norm_rope(x, gain, c, s1, s2, half, ms=None):
    if ms is None:
        ms = jnp.mean(x * x, axis=-1, keepdims=True)
    y = x * lax.rsqrt(ms + NORM_EPS) * gain
    return y * c + pltpu.roll(y, LANES - half, 1) * s1 + pltpu.roll(y, half, 1) * s2


def _gelu_tanh(x):
    return 0.5 * x * (1.0 + jnp.tanh(np.sqrt(2.0 / np.pi) * (x + 0.044715 * (x * x * x))))


def _nsa_compress_kernel(x_ref, pe_ref, w1_ref, w2_ref, g_ref, c_ref, s1_ref, s2_ref, o_ref):
    n_rows = x_ref.shape[0] // NSA_CMP_STRIDE
    d = NSA_HEAD_DIM
    y0 = jnp.zeros((n_rows, d), F32)
    y1 = jnp.zeros((n_rows, d), F32)
    for r in range(NSA_CMP_STRIDE):
        xr = x_ref[pl.ds(r, n_rows, stride=NSA_CMP_STRIDE), :]
        a0 = (xr + pe_ref[r:r + 1, :]).astype(BF16)
        a1 = (xr + pe_ref[NSA_CMP_STRIDE + r:NSA_CMP_STRIDE + r + 1, :]).astype(BF16)
        y0 = y0 + _dot(a0, w1_ref[r * d:(r + 1) * d, :].astype(BF16))
        y1 = y1 + _dot(a1, w1_ref[(NSA_CMP_STRIDE + r) * d:(NSA_CMP_STRIDE + r + 1) * d, :].astype(BF16))
    pre = y0 + pltpu.roll(y1, n_rows - 1, 0)
    out = _dot(_gelu_tanh(pre).astype(BF16), w2_ref[...].astype(BF16))
    roped = _norm_rope(out, g_ref[...], c_ref[...], s1_ref[...], s2_ref[...],
                       NSA_HEAD_DIM // ROPE_FRACTION // 2, None)
    o_ref[...] = jnp.where(pl.program_id(0) == 0, roped * NSA_SCALE2, out).astype(BF16)


def nsa_compress(proj, pe, w1, w2, k_gain, tables):
    b, s, _ = proj.shape
    g, d = NSA_KV_GROUPS, NSA_HEAD_DIM
    n_rows = s // NSA_CMP_STRIDE
    col_blk = NSA_HEADS
    tab_spec = pl.BlockSpec((n_rows, d), lambda kv, bi, gi: (0, 0))
    return pl.pallas_call(
        _nsa_compress_kernel,
        grid=(2, b, g),
        in_specs=[pl.BlockSpec((None, s, d), lambda kv, bi, gi: (bi, 0, col_blk + kv * g + gi)),
                  pl.BlockSpec((None, NSA_CMP_BLOCK, d), lambda kv, bi, gi: (kv, 0, 0)),
                  pl.BlockSpec((None, NSA_CMP_BLOCK * d, d), lambda kv, bi, gi: (kv, 0, 0)),
                  pl.BlockSpec((None, d, d), lambda kv, bi, gi: (kv, 0, 0)),
                  pl.BlockSpec((1, d), lambda kv, bi, gi: (0, 0)),
                  tab_spec, tab_spec, tab_spec],
        out_specs=pl.BlockSpec((None, None, None, n_rows, d), lambda kv, bi, gi: (kv, bi, gi, 0, 0)),
        out_shape=jax.ShapeDtypeStruct((2, b, g, n_rows, d), BF16),
        compiler_params=_params(("parallel", "parallel", "parallel")),
        name="nsa_compress",
    )(proj, pe, w1, w2, k_gain.reshape(1, d), *tables)


def _nsa_attn_kernel(q_ref, kc_ref, vc_ref, ksr_ref, vsr_ref, kwr_ref, vwr_ref, gate_ref, ovt_ref,
                     qg_ref, kg_ref, c_ref, s1_ref, s2_ref, o_ref, ks_ref, vs_ref, kw_ref, vw_ref, q_scr, score_ref, rank_ref,
                     *, tq, tk, seq):
    gi = pl.program_id(1)
    qi = pl.program_id(2)
    d = NSA_HEAD_DIM
    hpg = NSA_HPG
    n_sel = seq // NSA_SEL_BLOCK
    n_top = min(NSA_TOP_N, n_sel)
    half = d // ROPE_FRACTION // 2

    def tables(rows):
        return c_ref[rows, :], s1_ref[rows, :], s2_ref[rows, :]

    scale2 = NSA_SCALE2

    @pl.when(qi == 0)
    def _():
        chunk = 512
        lane = lax.broadcasted_iota(jnp.int32, (chunk, d), 1)
        row = lax.broadcasted_iota(jnp.int32, (chunk, d), 0)

        def body(i, carry):
            r0 = pl.multiple_of(i * chunk, chunk)
            rows = pl.ds(r0, chunk)
            ks = _norm_rope(ksr_ref[rows, :], kg_ref[0:1, :], *tables(rows), half, None) * scale2
            kw = _norm_rope(kwr_ref[rows, :], kg_ref[1:2, :], *tables(rows), half, None) * scale2
            ks_ref[rows, 0:d] = ks.astype(BF16)
            ks_ref[rows, d:2 * d] = jnp.where((r0 + row) // NSA_SEL_BLOCK == lane, NSA_MASK_NEG, 0.0).astype(BF16)
            kw_ref[rows, :] = kw.astype(BF16)
            vs_ref[rows, :] = vsr_ref[rows, :].astype(BF16)
            vw_ref[rows, :] = vwr_ref[rows, :].astype(BF16)
            return carry

        lax.fori_loop(0, seq // chunk, body, 0)

    q_rows = 128

    def q_body(i, carry):
        r0 = pl.multiple_of(i * q_rows, q_rows)
        tabs = tables(pl.ds(pl.multiple_of(qi * tq, tq) + r0, q_rows))
        for h in range(hpg):
            y = _norm_rope(q_ref[pl.ds(r0, q_rows), h * d:(h + 1) * d], qg_ref[...], *tabs, half, None)
            q_scr[pl.ds(h * tq + r0, q_rows), :] = y.astype(BF16)
        return carry

    lax.fori_loop(0, tq // q_rows, q_body, 0)
    q4 = q_scr[...]
    pos = qi * tq + lax.broadcasted_iota(jnp.int32, (tq, 1), 0)
    pos_l = qi * tq + lax.broadcasted_iota(jnp.int32, (1, tq), 1)

    def masked_scores(qs, k, bias=None):
        sc = _dot(qs, k, NT_DIMS)
        sc = sc.reshape(-1, tq, sc.shape[-1])
        return sc if bias is None else sc + bias[None]

    def stack(t):
        return t.reshape(t.shape[0] * tq, -1)

    span = NSA_WINDOW + tq
    w0 = pl.multiple_of(jnp.clip(qi * tq - NSA_WINDOW, 0, seq - span), tq)
    vw = vw_ref[pl.ds(w0, span), :]
    kp = w0 + lax.broadcasted_iota(jnp.int32, (1, span), 1)
    sw = masked_scores(q4, kw_ref[pl.ds(w0, span), :],
                       jnp.where((kp <= pos) & (kp > pos - NSA_WINDOW), 0.0, NEG_INF))
    pw = jnp.exp2(sw - jnp.max(sw, axis=-1, keepdims=True))
    o_win = _dot(stack(pw).astype(BF16), vw) / stack(jnp.sum(pw, axis=-1, keepdims=True))

    n_c = kc_ref.shape[0]
    cend = lax.broadcasted_iota(jnp.int32, (1, n_c), 1) * NSA_CMP_STRIDE + (NSA_CMP_BLOCK - 1)
    s = masked_scores(q4, kc_ref[...], jnp.where(cend <= pos, 0.0, NEG_INF))
    p = jnp.exp2(s - jnp.max(s, axis=-1, keepdims=True))
    p = p / jnp.sum(p, axis=-1, keepdims=True)
    p = jnp.where((pos >= NSA_CMP_BLOCK - 1)[None], p, 0.0)
    o_cmp = _dot(stack(p).astype(BF16), vc_ref[...])
    p_sum = jnp.sum(p, axis=0)
    imp_t = _dot(ovt_ref[...], p_sum, NT_DIMS, precision=lax.Precision.HIGHEST)

    j = lax.broadcasted_iota(jnp.int32, (n_sel, 1), 0)
    bq = pos_l // NSA_SEL_BLOCK
    forced = (j == 0) | (j == bq) | (j == bq - 1)
    score = jnp.where(j <= bq, imp_t + jnp.where(forced, NSA_FORCE_BONUS, 0.0), NEG_INF)
    sub = 8
    n_grp = n_sel // sub
    score_ref[...] = score
    rank_ref[...] = jnp.zeros_like(rank_ref)
    j_loc = lax.broadcasted_iota(jnp.int32, (sub, tq), 0)
    last_blk = (qi * tq + tq - 1) // NSA_SEL_BLOCK
    for gp in range(n_grp):
        @pl.when(gp * sub <= last_blk)
        def _(gp=gp):
            groups = [score_ref[v * sub:(v + 1) * sub, :] for v in range(n_grp)]
            ranks = [rank_ref[v * sub:(v + 1) * sub, :] for v in range(n_grp)]
            for jp in range(gp * sub, (gp + 1) * sub):
                row = score_ref[jp:jp + 1, :]
                for v, sg in enumerate(groups):
                    if v * sub > jp:
                        beats = row >= sg
                    elif (v + 1) * sub - 1 <= jp:
                        beats = row > sg
                    else:
                        beats = (row > sg) | ((row == sg) & (j_loc > jp - v * sub))
                    ranks[v] = ranks[v] + jnp.where(beats, 1.0, 0.0)
            for v in range(n_grp):
                rank_ref[v * sub:(v + 1) * sub, :] = ranks[v]
    not_sel = jnp.where(rank_ref[...] < n_top, 0.0, 1.0)
    not_sel = jnp.concatenate([not_sel, jnp.zeros((LANES - n_sel, tq), F32)], axis=0).T.astype(BF16)

    q_aug = jnp.concatenate([q4, jnp.concatenate([not_sel] * hpg, axis=0)], axis=1)

    def sel_tile(kt, carry, bias):
        m, l, acc = carry
        k0 = pl.multiple_of(kt * tk, tk)
        sc = masked_scores(q_aug, ks_ref[pl.ds(k0, tk), :], bias)
        m_new = jnp.maximum(m, jnp.max(sc, axis=-1, keepdims=True))
        alpha = jnp.exp2(m - m_new)
        pe = jnp.exp2(sc - m_new)
        l = alpha * l + jnp.sum(pe, axis=-1, keepdims=True)
        acc = stack(alpha) * acc + _dot(stack(pe).astype(BF16), vs_ref[pl.ds(k0, tk), :])
        return m_new, l, acc

    last = (qi * tq + tq - 1) // tk
    init = (jnp.full((hpg, tq, 1), NEG_INF, F32), jnp.zeros((hpg, tq, 1), F32), jnp.zeros((hpg * tq, d), F32))
    pairs = last // 2
    carry = lax.fori_loop(0, pairs, lambda kp_, c: sel_tile(2 * kp_ + 1, sel_tile(2 * kp_, c, None), None), init)
    carry = lax.fori_loop(2 * pairs, last, lambda kt, c: sel_tile(kt, c, None), carry)
    key_pos = last * tk + lax.broadcasted_iota(jnp.int32, (1, tk), 1)
    _, l_sel, acc_sel = sel_tile(last, carry, jnp.where(key_pos <= pos, 0.0, NEG_INF))
    o_sel = acc_sel / stack(l_sel)

    gt = jax.nn.sigmoid(gate_ref[...])
    lane = lax.broadcasted_iota(jnp.int32, (1, LANES), 1)
    for h in range(hpg):
        rows = slice(h * tq, (h + 1) * tq)
        o_h = jnp.zeros((tq, d), F32)
        for br, o_br in enumerate((o_cmp, o_sel, o_win)):
            g_col = jnp.sum(jnp.where(lane == gi * (3 * hpg) + h * 3 + br, gt, 0.0), axis=-1, keepdims=True)
            o_h = o_h + g_col * o_br[rows]
        o_ref[:, h * d:(h + 1) * d] = o_h.astype(BF16)


def selection_overlap(n_rows, n_sel):
    n_cmp = n_rows - 1
    c0 = np.arange(n_rows) * NSA_CMP_STRIDE
    s0 = np.arange(n_sel) * NSA_SEL_BLOCK
    ov = np.minimum(c0[:, None] + NSA_CMP_BLOCK, s0[None, :] + NSA_SEL_BLOCK) - np.maximum(c0[:, None], s0[None, :])
    ov = np.clip(ov, 0, None) / NSA_CMP_BLOCK
    ov[n_cmp:] = 0.0
    return jnp.asarray(ov, dtype=F32)


def nsa_attention(proj, kv_cmp, gate, q_gain, k_gains, tabs, tq=512, tk=512):
    b, s, _ = proj.shape
    g, d, hpg = NSA_KV_GROUPS, NSA_HEAD_DIM, NSA_HPG
    n_rows = s // NSA_CMP_STRIDE
    n_sel = s // NSA_SEL_BLOCK
    seg_blk = lambda i: NSA_HEADS + i * g
    cmp_spec = lambda kv: pl.BlockSpec((None, None, None, n_rows, d), lambda bi, gi, qi: (kv, bi, gi, 0, 0))
    seq_spec = lambda i: pl.BlockSpec((None, s, d), lambda bi, gi, qi: (bi, 0, seg_blk(i) + gi))
    const = lambda shape: pl.BlockSpec(shape, lambda bi, gi, qi: (0, 0))
    return pl.pallas_call(
        functools.partial(_nsa_attn_kernel, tq=tq, tk=tk, seq=s),
        grid=(b, g, s // tq),
        in_specs=[pl.BlockSpec((None, tq, hpg * d), lambda bi, gi, qi: (bi, qi, gi)),
                  cmp_spec(0), cmp_spec(1),
                  seq_spec(2), seq_spec(3), seq_spec(4), seq_spec(5),
                  pl.BlockSpec((None, tq, LANES), lambda bi, gi, qi: (bi, qi, 0)),
                  const((n_sel, n_rows)), const((1, d)), const((2, d)),
                  const((s, d)), const((s, d)), const((s, d))],
        out_specs=pl.BlockSpec((None, tq, hpg * d), lambda bi, gi, qi: (bi, qi, gi)),
        out_shape=jax.ShapeDtypeStruct((b, s, NSA_HEADS * d), BF16),
        scratch_shapes=[pltpu.VMEM((s, 2 * d), BF16)] + [pltpu.VMEM((s, d), BF16)] * 3
                       + [pltpu.VMEM((hpg * tq, d), BF16)] + [pltpu.VMEM((n_sel, tq), F32)] * 2,
        compiler_params=_params(("parallel", "parallel", "arbitrary")),
        name="nsa_attention",
    )(proj, kv_cmp, kv_cmp, proj, proj, proj, proj, gate, selection_overlap(n_rows, n_sel).T,
      q_gain.reshape(1, d), k_gains, *tabs)


def nsa_mixer(x2, b, s, gain, w_in, w_out, layer, q_norm, k_norm, cmp_pe, cmp_w1, cmp_w2):
    d, g = NSA_HEAD_DIM, NSA_KV_GROUPS
    n_main = (NSA_HEADS + 6 * g) * d
    w_gate = jnp.pad(w_in[layer, :, n_main:], ((0, 0), (0, LANES - 3 * NSA_HEADS)))
    proj, gate = norm_matmul(x2, gain, w_in, layer, n_main, w_gate)
    proj = proj.reshape(b, s, n_main)
    gate = gate.reshape(b, s, LANES)
    tabs = rope_tables(jnp.arange(s), d)
    cmp_end = jnp.arange(s // NSA_CMP_STRIDE) * NSA_CMP_STRIDE + (NSA_CMP_BLOCK - 1)
    kv_cmp = nsa_compress(proj, cmp_pe, cmp_w1, cmp_w2, k_norm[0], rope_tables(cmp_end, d))
    o = nsa_attention(proj, kv_cmp, gate, q_norm, k_norm[1:3], tabs)
    return matmul_residual(o.reshape(b * s, NSA_HEADS * d), w_out, layer, x2)


def _log_sigmoid(z):
    return jnp.minimum(z, 0.0) - jnp.log1p(jnp.exp(-jnp.abs(z)))


def _gla_kernel(q_ref, k_ref, v_ref, r_ref, glr_ref, wg_ref, bg_ref, on_ref, o_ref, state_ref, *, rows, hps):
    c = GLA_CHUNK
    sub = GLA_SUB
    dk, dv = GLA_KEY_DIM, GLA_VAL_DIM

    @pl.when(pl.program_id(2) == 0)
    def _():
        state_ref[...] = jnp.zeros_like(state_ref)

    tri = jnp.where(lax.broadcasted_iota(jnp.int32, (c, c), 0) >= lax.broadcasted_iota(jnp.int32, (c, c), 1), 1.0, 0.0)
    sub_row = lax.broadcasted_iota(jnp.int32, (sub, 1), 0)

    def head_chunk(hh, r0):
        kcols = slice(hh * dk, (hh + 1) * dk)
        vcols = slice(hh * dv, (hh + 1) * dv)
        qc = q_ref[pl.ds(r0, c), kcols] * (dk ** -0.5)
        kc = k_ref[pl.ds(r0, c), kcols]
        vc = v_ref[pl.ds(r0, c), vcols]
        z = _dot(glr_ref[pl.ds(r0, c), :], wg_ref[:, kcols], precision=lax.Precision.HIGHEST) + bg_ref[:, kcols]
        log_a = _log_sigmoid(z) / GLA_TAU
        cum = _dot(tri, log_a, precision=lax.Precision.HIGHEST)
        vb = vc.astype(BF16)

        state = state_ref[hh]
        o_inter = _dot((qc * jnp.exp(cum)).astype(BF16), state.astype(BF16), NT_DIMS)

        parts = []
        for bi in range(c // sub):
            lo = bi * sub
            q_i, k_i, v_i, c_i = qc[lo:lo + sub], kc[lo:lo + sub], vc[lo:lo + sub], cum[lo:lo + sub]
            o_i = o_inter[lo:lo + sub]
            if bi > 0:
                c0 = c_i[0:1]
                qs = (q_i * jnp.exp(c_i - c0)).astype(BF16)
                kp = (kc[:lo] * jnp.exp(c0 - cum[:lo])).astype(BF16)
                att = _dot(qs, kp, NT_DIMS)
                o_i = o_i + _dot(att.astype(BF16), vb[:lo])
            for jj in range(sub):
                e = jnp.exp(jnp.where(sub_row >= jj, c_i - c_i[jj:jj + 1], NEG_INF))
                col = jnp.sum(q_i * k_i[jj:jj + 1] * e, axis=-1, keepdims=True)
                o_i = o_i + col * v_i[jj:jj + 1]
            parts.append(o_i)
        o = jnp.concatenate(parts, axis=0)

        last = cum[c - 1:c]
        kd = (kc * jnp.exp(last - cum)).astype(BF16)
        state_ref[hh] = state * jnp.exp(last) + _dot(vb, kd, TN_DIMS)

        ms = jnp.mean(o * o, axis=-1, keepdims=True)
        on = o * lax.rsqrt(ms + NORM_EPS) * on_ref[...]
        r = r_ref[pl.ds(r0, c), vcols]
        o_ref[pl.ds(r0, c), vcols] = (on * (r * jax.nn.sigmoid(r))).astype(BF16)

    def chunk(ci, carry):
        r0 = pl.multiple_of(ci * c, c)
        for hh in range(hps):
            head_chunk(hh, r0)
        return carry

    lax.fori_loop(0, rows // c, chunk, 0)


def gla_attention(proj, glr, w_gate_up, b_gate, o_norm, rows=512, hps=4):
    b, s, _ = proj.shape
    nh, dk, dv = GLA_HEADS, GLA_KEY_DIM, GLA_VAL_DIM
    wg = jnp.pad(w_gate_up, ((0, LANES - GLA_GATE_RANK), (0, 0)))
    kw, vw = hps * dk, hps * dv
    k_blk = nh * dk // kw
    v_blk = 2 * nh * dk // vw
    r_blk = (2 * nh * dk + nh * dv) // vw
    return pl.pallas_call(
        functools.partial(_gla_kernel, rows=rows, hps=hps),
        grid=(b, nh // hps, s // rows),
        in_specs=[pl.BlockSpec((None, rows, kw), lambda bi, h, ci: (bi, ci, h)),
                  pl.BlockSpec((None, rows, kw), lambda bi, h, ci: (bi, ci, k_blk + h)),
                  pl.BlockSpec((None, rows, vw), lambda bi, h, ci: (bi, ci, v_blk + h)),
                  pl.BlockSpec((None, rows, vw), lambda bi, h, ci: (bi, ci, r_blk + h)),
                  pl.BlockSpec((None, rows, LANES), lambda bi, h, ci: (bi, ci, 0)),
                  pl.BlockSpec((LANES, kw), lambda bi, h, ci: (0, h)),
                  pl.BlockSpec((1, kw), lambda bi, h, ci: (0, h)),
                  pl.BlockSpec((1, dv), lambda bi, h, ci: (0, 0))],
        out_specs=pl.BlockSpec((None, rows, vw), lambda bi, h, ci: (bi, ci, h)),
        out_shape=jax.ShapeDtypeStruct((b, s, nh * dv), BF16),
        scratch_shapes=[pltpu.VMEM((hps, dv, dk), F32)],
        compiler_params=_params(("parallel", "parallel", "arbitrary")),
        name="gla_attention",
    )(proj, proj, proj, proj, glr, wg, b_gate.reshape(1, nh * dk), o_norm.reshape(1, dv))


def gla_mixer(x2, b, s, gain, w_in, w_gate_up, b_gate, o_norm, w_out, layer):
    nh, dk, dv = GLA_HEADS, GLA_KEY_DIM, GLA_VAL_DIM
    n_qkv = 2 * nh * dk + nh * dv
    w_r = w_in[layer, :, n_qkv + GLA_GATE_RANK:]
    w_glr = jnp.pad(w_in[layer, :, n_qkv:n_qkv + GLA_GATE_RANK], ((0, 0), (0, LANES - GLA_GATE_RANK)))
    proj, glr = norm_matmul(x2, gain, w_in, layer, n_qkv, w_glr, w_tail=w_r)
    o = gla_attention(proj.reshape(b, s, -1), glr.reshape(b, s, LANES), w_gate_up, b_gate, o_norm)
    return matmul_residual(o.reshape(b * s, nh * dv), w_out, layer, x2)


def _swa_heads(x, gain, tabs, seg, out_scale):
    blk = x.shape[0]
    d = SWA_HEAD_DIM
    n = x.shape[1] // LANES
    tiles = jnp.concatenate([x[:, t * LANES:(t + 1) * LANES] for t in range(n)], axis=0)
    sq = tiles * tiles
    hi = sq.astype(BF16)
    lo = (sq - hi.astype(F32)).astype(BF16)
    ms = (_dot(hi, seg) + _dot(lo, seg)) * (1.0 / d)
    heads = []
    for t in range(n):
        rows = slice(t * blk, (t + 1) * blk)
        y = _norm_rope(tiles[rows], gain, *tabs, d // ROPE_FRACTION // 2, ms[rows]) * out_scale
        heads += [y[:, u * d:(u + 1) * d] for u in range(LANES // d)]
    return heads


def _swa_kernel(q_ref, kp_ref, kc_ref, vp_ref, vc_ref, sink_ref, qg_ref, kg_ref, seg_ref,
                cc_ref, s1c_ref, s2c_ref, cp_ref, s1p_ref, s2p_ref, o_ref, *, blk):
    qi = pl.program_id(1)
    hpg, d, g = SWA_HPG, SWA_HEAD_DIM, SWA_KV_HEADS
    seg = seg_ref[...]
    tabs_c = (cc_ref[...], s1c_ref[...], s2c_ref[...])
    tabs_p = (cp_ref[...], s1p_ref[...], s2p_ref[...])
    q_heads = _swa_heads(q_ref[...], qg_ref[...], tabs_c, seg, d ** -0.5)
    kp_heads = _swa_heads(kp_ref[...], kg_ref[...], tabs_p, seg, 1.0)
    kc_heads = _swa_heads(kc_ref[...], kg_ref[...], tabs_c, seg, 1.0)
    vp, vc = vp_ref[...], vc_ref[...]

    qp = qi * blk + lax.broadcasted_iota(jnp.int32, (blk, 1), 0)
    kpos = (qi - 1) * blk + lax.broadcasted_iota(jnp.int32, (1, 2 * blk), 1)
    bias = jnp.where((kpos <= qp) & (kpos > qp - SWA_WINDOW) & (kpos >= 0), 0.0, NEG_INF)

    for gi in range(g):
        q = jnp.concatenate(q_heads[gi * hpg:(gi + 1) * hpg], axis=0).astype(BF16)
        k = jnp.concatenate([kp_heads[gi], kc_heads[gi]], axis=0).astype(BF16)
        v = jnp.concatenate([vp[:, gi * d:(gi + 1) * d], vc[:, gi * d:(gi + 1) * d]], axis=0).astype(BF16)
        s = _dot(q, k, NT_DIMS).reshape(hpg, blk, 2 * blk) + bias[None]
        sink = sink_ref[gi]
        m = jnp.maximum(jnp.max(s, axis=-1, keepdims=True), sink)
        p = jnp.exp(s - m)
        denom = jnp.sum(p, axis=-1, keepdims=True) + jnp.exp(sink - m)
        o = _dot(p.reshape(hpg * blk, 2 * blk).astype(BF16), v).reshape(hpg, blk, d) / denom
        o_ref[:, gi * hpg * d:(gi + 1) * hpg * d] = jnp.concatenate([o[h] for h in range(hpg)], axis=1).astype(BF16)


def swa_attention(proj, sinks, q_gain, k_gain, tabs, blk=128):
    b, s, _ = proj.shape
    g, hpg, d = SWA_KV_HEADS, SWA_HPG, SWA_HEAD_DIM
    nq, nkv = SWA_HEADS * d, g * d
    k_blk = nq // nkv
    prev = lambda qi: jnp.maximum(qi - 1, 0)
    kv_spec = lambda col, row: pl.BlockSpec((None, blk, nkv), lambda bi, qi: (bi, row(qi), col))
    const = lambda shape: pl.BlockSpec(shape, lambda bi, qi: (0,) * len(shape))
    tab_spec = lambda row: pl.BlockSpec((blk, LANES), lambda bi, qi: (row(qi), 0))
    cur = lambda qi: qi
    seg_id = np.arange(LANES) // d
    seg = jnp.asarray(seg_id[:, None] == seg_id[None, :], dtype=BF16)
    tile_gain = lambda gn: jnp.tile(gn.reshape(1, d), (1, LANES // d))
    return pl.pallas_call(
        functools.partial(_swa_kernel, blk=blk),
        grid=(b, s // blk),
        in_specs=[pl.BlockSpec((None, blk, nq), lambda bi, qi: (bi, qi, 0)),
                  kv_spec(k_blk, prev), kv_spec(k_blk, cur), kv_spec(k_blk + 1, prev), kv_spec(k_blk + 1, cur),
                  const((g, hpg, 1, 1)), const((1, LANES)), const((1, LANES)), const((LANES, LANES)),
                  tab_spec(cur), tab_spec(cur), tab_spec(cur), tab_spec(prev), tab_spec(prev), tab_spec(prev)],
        out_specs=pl.BlockSpec((None, blk, nq), lambda bi, qi: (bi, qi, 0)),
        out_shape=jax.ShapeDtypeStruct((b, s, nq), BF16),
        compiler_params=_params(("parallel", "parallel")),
        name="swa_attention",
    )(proj, proj, proj, proj, proj, sinks.astype(F32).reshape(g, hpg, 1, 1), tile_gain(q_gain), tile_gain(k_gain),
      seg, *tabs, *tabs)


def swa_mixer(x2, b, s, gain, w_in, w_out, layer, q_norm, k_norm, sinks):
    d, g = SWA_HEAD_DIM, SWA_KV_HEADS
    n_in = (SWA_HEADS + 2 * g) * d
    (proj,) = norm_matmul(x2, gain, w_in, layer, n_in)
    o = swa_attention(proj.reshape(b, s, n_in), sinks, q_norm, k_norm, rope_tables(jnp.arange(s), d))
    return matmul_residual(o.reshape(b * s, SWA_HEADS * d), w_out, layer, x2)


def kernel(x, norm_mix, norm_mlp, mlp_w_up, mlp_w_down, nsa_w_in, nsa_w_out, nsa_q_norm, nsa_k_norm, nsa_cmp_pe, nsa_cmp_w1, nsa_cmp_w2, gla_w_in, gla_w_gate_up, gla_b_gate, gla_o_norm, gla_w_out, swa_w_in, swa_w_out, swa_q_norm, swa_k_norm, swa_sinks):
    b, s, d = x.shape
    x2 = x.reshape(b * s, d)
    ia = ib = ic = 0
    for i in range(norm_mix.shape[0]):
        kind = i % N_MIXERS
        if kind == 0:
            x2 = nsa_mixer(x2, b, s, norm_mix[i], nsa_w_in, nsa_w_out, ia, nsa_q_norm[ia], nsa_k_norm[ia],
                           nsa_cmp_pe[ia], nsa_cmp_w1[ia], nsa_cmp_w2[ia])
            ia += 1
        elif kind == 1:
            x2 = gla_mixer(x2, b, s, norm_mix[i], gla_w_in, gla_w_gate_up[ib], gla_b_gate[ib],
                           gla_o_norm[ib], gla_w_out, ib)
            ib += 1
        else:
            x2 = swa_mixer(x2, b, s, norm_mix[i], swa_w_in, swa_w_out, ic, swa_q_norm[ic], swa_k_norm[ic],
                           swa_sinks[ic])
            ic += 1
        x2 = mlp_block(x2, norm_mlp[i], mlp_w_up, mlp_w_down, i)
    return x2.reshape(b, s, d)
```

```python
import functools

import numpy as np
import jax
import jax.numpy as jnp
from jax import lax
from jax.experimental import pallas as pl
from jax.experimental.pallas import tpu as pltpu

F32 = jnp.float32
BF16 = jnp.bfloat16

NORM_EPS = 1e-6
ROPE_THETA = 500000.0
ROPE_FRACTION = 4
NEG_INF = -1e30
N_MIXERS = 3

NSA_HEAD_DIM = 128
NSA_HEADS = 16
NSA_KV_GROUPS = 4
NSA_HPG = NSA_HEADS // NSA_KV_GROUPS
NSA_CMP_BLOCK = 32
NSA_CMP_STRIDE = 16
NSA_SEL_BLOCK = 64
NSA_TOP_N = 16
NSA_WINDOW = 512
NSA_FORCE_BONUS = 1e4
NSA_SCALE2 = float(NSA_HEAD_DIM ** -0.5 * np.log2(np.e))
NSA_MASK_NEG = -2.0 ** 100

GLA_HEADS = 4
GLA_KEY_DIM = 256
GLA_VAL_DIM = 512
GLA_GATE_RANK = 16
GLA_TAU = 16.0
GLA_CHUNK = 64
GLA_SUB = 16

SWA_HEAD_DIM = 64
SWA_HEADS = 32
SWA_KV_HEADS = 4
SWA_HPG = SWA_HEADS // SWA_KV_HEADS
SWA_WINDOW = 128

LANES = 128
VMEM_LIMIT = 56 * 1024 * 1024
MLP_VMEM_LIMIT = 62 * 1024 * 1024

NT_DIMS = (((1,), (1,)), ((), ()))
TN_DIMS = (((0,), (0,)), ((), ()))


def _params(sem, vmem_limit=VMEM_LIMIT):
    return pltpu.CompilerParams(dimension_semantics=sem, vmem_limit_bytes=vmem_limit)


def _dot(a, b, dims=None, precision=None):
    if dims is None:
        return jnp.dot(a, b, preferred_element_type=F32, precision=precision)
    return lax.dot_general(a, b, dims, preferred_element_type=F32, precision=precision)


def _rms_rows_to(h_scr, x_ref, g_ref, rows):
    n = x_ref.shape[0] // rows

    def body(i, c):
        r0 = pl.multiple_of(i * rows, rows)
        x = x_ref[pl.ds(r0, rows), :]
        ms = jnp.mean(x * x, axis=-1, keepdims=True)
        h_scr[pl.ds(r0, rows), :] = (x * lax.rsqrt(ms + NORM_EPS) * g_ref[...]).astype(BF16)
        return c

    lax.fori_loop(0, n, body, 0)


def _row_tile_copy(x_hbm, x_buf, sem, i):
    tm = x_buf.shape[0]
    return pltpu.make_async_copy(x_hbm.at[pl.ds(pl.multiple_of(i * tm, tm), tm), :], x_buf, sem)


def _fetch_and_norm(x_hbm, x_buf, sem, g_ref, h_scr, prefetch_step):
    i, j = pl.program_id(0), pl.program_id(1)

    @pl.when(j == 0)
    def _():
        @pl.when(i == 0)
        def _():
            _row_tile_copy(x_hbm, x_buf, sem, 0).start()

        _row_tile_copy(x_hbm, x_buf, sem, i).wait()
        _rms_rows_to(h_scr, x_buf, g_ref, 128)

    @pl.when((j == prefetch_step) & (i + 1 < pl.num_programs(0)))
    def _():
        _row_tile_copy(x_hbm, x_buf, sem, i + 1).start()


def _norm_matmul_kernel(x_hbm, g_ref, w_ref, *rest, has_extra, n_head_tiles):
    rest = list(rest)
    wt_ref = rest.pop(0) if n_head_tiles is not None else None
    if has_extra:
        wx_ref, o_ref, ox_ref, h_scr, x_buf, sem = rest
    else:
        o_ref, h_scr, x_buf, sem = rest
    j = pl.program_id(1)
    _fetch_and_norm(x_hbm, x_buf, sem, g_ref, h_scr, prefetch_step=1)

    if has_extra:
        @pl.when(j == 0)
        def _():
            ox_ref[...] = _dot(h_scr[...], wx_ref[...].astype(BF16))

    if wt_ref is None:
        o_ref[...] = _dot(h_scr[...], w_ref[...].astype(BF16))
    else:
        @pl.when(j < n_head_tiles)
        def _():
            o_ref[...] = _dot(h_scr[...], w_ref[...].astype(BF16))

        @pl.when(j >= n_head_tiles)
        def _():
            o_ref[...] = _dot(h_scr[...], wt_ref[...].astype(BF16))


def norm_matmul(x, gain, w, layer, n_cols, w_extra=None, w_tail=None, tm=2048, tn=512):
    m, d = x.shape
    tm = min(tm, m)
    has_extra = w_extra is not None
    n_head = n_cols // tn
    n_tail = 0 if w_tail is None else w_tail.shape[1] // tn
    in_specs = [pl.BlockSpec(memory_space=pl.ANY),
                pl.BlockSpec((1, d), lambda i, j: (0, 0)),
                pl.BlockSpec((None, d, tn), lambda i, j: (layer, 0, jnp.minimum(j, n_head - 1)))]
    out_shape = [jax.ShapeDtypeStruct((m, (n_head + n_tail) * tn), F32)]
    out_specs = [pl.BlockSpec((tm, tn), lambda i, j: (i, j))]
    args = [x, gain.reshape(1, d), w]
    if w_tail is not None:
        in_specs.append(pl.BlockSpec((d, tn), lambda i, j: (0, jnp.maximum(j - n_head, 0))))
        args.append(w_tail)
    if has_extra:
        in_specs.append(pl.BlockSpec((d, LANES), lambda i, j: (0, 0)))
        out_shape.append(jax.ShapeDtypeStruct((m, LANES), F32))
        out_specs.append(pl.BlockSpec((tm, LANES), lambda i, j: (i, 0)))
        args.append(w_extra)
    return pl.pallas_call(
        functools.partial(_norm_matmul_kernel, has_extra=has_extra,
                          n_head_tiles=None if w_tail is None else n_head),
        grid=(m // tm, n_head + n_tail),
        in_specs=in_specs, out_specs=out_specs, out_shape=out_shape,
        scratch_shapes=[pltpu.VMEM((tm, d), BF16), pltpu.VMEM((tm, d), F32), pltpu.SemaphoreType.DMA(())],
        compiler_params=_params(("arbitrary", "arbitrary")),
        name="norm_matmul",
    )(*args)


def _matmul_residual_kernel(a_ref, w_ref, r_ref, o_ref):
    o_ref[...] = r_ref[...] + _dot(a_ref[...], w_ref[...].astype(BF16))


def matmul_residual(a, w, layer, res, tm=2048, tn=512):
    m, k = a.shape
    tm = min(tm, m)
    n = w.shape[2]
    return pl.pallas_call(
        _matmul_residual_kernel,
        grid=(m // tm, n // tn),
        in_specs=[pl.BlockSpec((tm, k), lambda i, j: (i, 0)),
                  pl.BlockSpec((None, k, tn), lambda i, j: (layer, 0, j)),
                  pl.BlockSpec((tm, tn), lambda i, j: (i, j))],
        out_specs=pl.BlockSpec((tm, tn), lambda i, j: (i, j)),
        out_shape=jax.ShapeDtypeStruct((m, n), F32),
        compiler_params=_params(("parallel", "parallel")),
        name="matmul_residual",
    )(a, w, res)


def _mlp_kernel(x_hbm, g_ref, wu_ref, xt_ref, wd_ref, o_ref, h_scr, u_scr, x_buf, sem, *, nf, tf, splits):
    j = pl.program_id(1)
    d = h_scr.shape[1]
    per = nf // splits
    _fetch_and_norm(x_hbm, x_buf, sem, g_ref, h_scr, prefetch_step=nf)

    @pl.when(j < nf)
    def _():
        u = _dot(h_scr[:, 0:tf], wu_ref[0:tf, :].astype(BF16))
        for c in range(1, d // tf):
            u = u + _dot(h_scr[:, c * tf:(c + 1) * tf], wu_ref[c * tf:(c + 1) * tf, :].astype(BF16))
        u = jnp.maximum(u, 0.0)
        u_scr[j] = (u * u).astype(BF16)

    for part in range(splits):
        @pl.when((j >= nf) & ((j - nf) % splits == part))
        def _(part=part):
            acc = xt_ref[...] if part == 0 else o_ref[...]
            for f in range(per):
                acc = acc + _dot(u_scr[part * per + f], wd_ref[f * tf:(f + 1) * tf, :].astype(BF16))
            o_ref[...] = acc


def mlp_block(x, gain, w_up, w_down, layer, tm=1024, tf=512, tn=512, splits=2):
    m, d = x.shape
    ff = w_up.shape[2]
    nf, nn = ff // tf, d // tn
    down = lambda j: jnp.maximum(j - nf, 0)
    return pl.pallas_call(
        functools.partial(_mlp_kernel, nf=nf, tf=tf, splits=splits),
        grid=(m // tm, nf + nn * splits),
        in_specs=[pl.BlockSpec(memory_space=pl.ANY),
                  pl.BlockSpec((1, d), lambda i, j: (0, 0)),
                  pl.BlockSpec((None, d, tf), lambda i, j: (layer, 0, jnp.minimum(j, nf - 1))),
                  pl.BlockSpec((tm, tn), lambda i, j: (i, down(j) // splits)),
                  pl.BlockSpec((None, ff // splits, tn), lambda i, j: (layer, down(j) % splits, down(j) // splits))],
        out_specs=pl.BlockSpec((tm, tn), lambda i, j: (i, down(j) // splits)),
        out_shape=jax.ShapeDtypeStruct((m, d), F32),
        scratch_shapes=[pltpu.VMEM((tm, d), BF16), pltpu.VMEM((nf, tm, tf), BF16), pltpu.VMEM((tm, d), F32),
                        pltpu.SemaphoreType.DMA(())],
        compiler_params=_params(("arbitrary", "arbitrary"), MLP_VMEM_LIMIT),
        name="mlp_block",
    )(x, gain.reshape(1, d), w_up, x, w_down)


def rope_tables(pos, head_dim):
    rot = head_dim // ROPE_FRACTION
    half = rot // 2
    inv_freq = jnp.power(jnp.float32(ROPE_THETA), -jnp.arange(half, dtype=F32) / half)
    ang = pos.astype(F32)[:, None] * inv_freq[None, :]
    cos, sin = jnp.cos(ang), jnp.sin(ang)
    n = pos.shape[0]
    zeros = jnp.zeros((n, head_dim - rot), F32)
    c = jnp.concatenate([cos, cos, jnp.ones((n, head_dim - rot), F32)], axis=-1)
    s1 = jnp.concatenate([-sin, jnp.zeros((n, half), F32), zeros], axis=-1)
    s2 = jnp.concatenate([jnp.zeros((n, half), F32), sin, zeros], axis=-1)
    reps = LANES // head_dim
    return tuple(jnp.tile(t, (1, reps)) for t in (c, s1, s2))


def _norm_rope(x, gain, c, s1, s2, half, ms=None):
    if ms is None:
        ms = jnp.mean(x * x, axis=-1, keepdims=True)
    y = x * lax.rsqrt(ms + NORM_EPS) * gain
    return y * c + pltpu.roll(y, LANES - half, 1) * s1 + pltpu.roll(y, half, 1) * s2


def _gelu_tanh(x):
    return 0.5 * x * (1.0 + jnp.tanh(np.sqrt(2.0 / np.pi) * (x + 0.044715 * (x * x * x))))


def _nsa_compress_kernel(x_ref, pe_ref, w1_ref, w2_ref, g_ref, c_ref, s1_ref, s2_ref, o_ref):
    n_rows = x_ref.shape[0] // NSA_CMP_STRIDE
    d = NSA_HEAD_DIM
    y0 = jnp.zeros((n_rows, d), F32)
    y1 = jnp.zeros((n_rows, d), F32)
    for r in range(NSA_CMP_STRIDE):
        xr = x_ref[pl.ds(r, n_rows, stride=NSA_CMP_STRIDE), :]
        a0 = (xr + pe_ref[r:r + 1, :]).astype(BF16)
        a1 = (xr + pe_ref[NSA_CMP_STRIDE + r:NSA_CMP_STRIDE + r + 1, :]).astype(BF16)
        y0 = y0 + _dot(a0, w1_ref[r * d:(r + 1) * d, :].astype(BF16))
        y1 = y1 + _dot(a1, w1_ref[(NSA_CMP_STRIDE + r) * d:(NSA_CMP_STRIDE + r + 1) * d, :].astype(BF16))
    pre = y0 + pltpu.roll(y1, n_rows - 1, 0)
    out = _dot(_gelu_tanh(pre).astype(BF16), w2_ref[...].astype(BF16))
    roped = _norm_rope(out, g_ref[...], c_ref[...], s1_ref[...], s2_ref[...],
                       NSA_HEAD_DIM // ROPE_FRACTION // 2, None)
    o_ref[...] = jnp.where(pl.program_id(0) == 0, roped * NSA_SCALE2, out).astype(BF16)


def nsa_compress(proj, pe, w1, w2, k_gain, tables):
    b, s, _ = proj.shape
    g, d = NSA_KV_GROUPS, NSA_HEAD_DIM
    n_rows = s // NSA_CMP_STRIDE
    col_blk = NSA_HEADS
    tab_spec = pl.BlockSpec((n_rows, d), lambda kv, bi, gi: (0, 0))
    return pl.pallas_call(
        _nsa_compress_kernel,
        grid=(2, b, g),
        in_specs=[pl.BlockSpec((None, s, d), lambda kv, bi, gi: (bi, 0, col_blk + kv * g + gi)),
                  pl.BlockSpec((None, NSA_CMP_BLOCK, d), lambda kv, bi, gi: (kv, 0, 0)),
                  pl.BlockSpec((None, NSA_CMP_BLOCK * d, d), lambda kv, bi, gi: (kv, 0, 0)),
                  pl.BlockSpec((None, d, d), lambda kv, bi, gi: (kv, 0, 0)),
                  pl.BlockSpec((1, d), lambda kv, bi, gi: (0, 0)),
                  tab_spec, tab_spec, tab_spec],
        out_specs=pl.BlockSpec((None, None, None, n_rows, d), lambda kv, bi, gi: (kv, bi, gi, 0, 0)),
        out_shape=jax.ShapeDtypeStruct((2, b, g, n_rows, d), BF16),
        compiler_params=_params(("parallel", "parallel", "parallel")),
        name="nsa_compress",
    )(proj, pe, w1, w2, k_gain.reshape(1, d), *tables)


def _nsa_attn_kernel(q_ref, kc_ref, vc_ref, ksr_ref, vsr_ref, kwr_ref, vwr_ref, gate_ref, ovt_ref,
                     qg_ref, kg_ref, c_ref, s1_ref, s2_ref, o_ref, ks_ref, vs_ref, kw_ref, vw_ref, q_scr, score_ref, rank_ref,
                     *, tq, tk, seq):
    gi = pl.program_id(1)
    qi = pl.program_id(2)
    d = NSA_HEAD_DIM
    hpg = NSA_HPG
    n_sel = seq // NSA_SEL_BLOCK
    n_top = min(NSA_TOP_N, n_sel)
    half = d // ROPE_FRACTION // 2

    def tables(rows):
        return c_ref[rows, :], s1_ref[rows, :], s2_ref[rows, :]

    scale2 = NSA_SCALE2

    @pl.when(qi == 0)
    def _():
        chunk = 512
        lane = lax.broadcasted_iota(jnp.int32, (chunk, d), 1)
        row = lax.broadcasted_iota(jnp.int32, (chunk, d), 0)

        def body(i, carry):
            r0 = pl.multiple_of(i * chunk, chunk)
            rows = pl.ds(r0, chunk)
            ks = _norm_rope(ksr_ref[rows, :], kg_ref[0:1, :], *tables(rows), half, None) * scale2
            kw = _norm_rope(kwr_ref[rows, :], kg_ref[1:2, :], *tables(rows), half, None) * scale2
            ks_ref[rows, 0:d] = ks.astype(BF16)
            ks_ref[rows, d:2 * d] = jnp.where((r0 + row) // NSA_SEL_BLOCK == lane, NSA_MASK_NEG, 0.0).astype(BF16)
            kw_ref[rows, :] = kw.astype(BF16)
            vs_ref[rows, :] = vsr_ref[rows, :].astype(BF16)
            vw_ref[rows, :] = vwr_ref[rows, :].astype(BF16)
            return carry

        lax.fori_loop(0, seq // chunk, body, 0)

    q_rows = 128

    def q_body(i, carry):
        r0 = pl.multiple_of(i * q_rows, q_rows)
        tabs = tables(pl.ds(pl.multiple_of(qi * tq, tq) + r0, q_rows))
        for h in range(hpg):
            y = _norm_rope(q_ref[pl.ds(r0, q_rows), h * d:(h + 1) * d], qg_ref[...], *tabs, half, None)
            q_scr[pl.ds(h * tq + r0, q_rows), :] = y.astype(BF16)
        return carry

    lax.fori_loop(0, tq // q_rows, q_body, 0)
    q4 = q_scr[...]
    pos = qi * tq + lax.broadcasted_iota(jnp.int32, (tq, 1), 0)
    pos_l = qi * tq + lax.broadcasted_iota(jnp.int32, (1, tq), 1)

    def masked_scores(qs, k, bias=None):
        sc = _dot(qs, k, NT_DIMS)
        sc = sc.reshape(-1, tq, sc.shape[-1])
        return sc if bias is None else sc + bias[None]

    def stack(t):
        return t.reshape(t.shape[0] * tq, -1)

    span = NSA_WINDOW + tq
    w0 = pl.multiple_of(jnp.clip(qi * tq - NSA_WINDOW, 0, seq - span), tq)
    vw = vw_ref[pl.ds(w0, span), :]
    kp = w0 + lax.broadcasted_iota(jnp.int32, (1, span), 1)
    sw = masked_scores(q4, kw_ref[pl.ds(w0, span), :],
                       jnp.where((kp <= pos) & (kp > pos - NSA_WINDOW), 0.0, NEG_INF))
    pw = jnp.exp2(sw - jnp.max(sw, axis=-1, keepdims=True))
    o_win = _dot(stack(pw).astype(BF16), vw) / stack(jnp.sum(pw, axis=-1, keepdims=True))

    n_c = kc_ref.shape[0]
    cend = lax.broadcasted_iota(jnp.int32, (1, n_c), 1) * NSA_CMP_STRIDE + (NSA_CMP_BLOCK - 1)
    s = masked_scores(q4, kc_ref[...], jnp.where(cend <= pos, 0.0, NEG_INF))
    p = jnp.exp2(s - jnp.max(s, axis=-1, keepdims=True))
    p = p / jnp.sum(p, axis=-1, keepdims=True)
    p = jnp.where((pos >= NSA_CMP_BLOCK - 1)[None], p, 0.0)
    o_cmp = _dot(stack(p).astype(BF16), vc_ref[...])
    p_sum = jnp.sum(p, axis=0)
    imp_t = _dot(ovt_ref[...], p_sum, NT_DIMS, precision=lax.Precision.HIGHEST)

    j = lax.broadcasted_iota(jnp.int32, (n_sel, 1), 0)
    bq = pos_l // NSA_SEL_BLOCK
    forced = (j == 0) | (j == bq) | (j == bq - 1)
    score = jnp.where(j <= bq, imp_t + jnp.where(forced, NSA_FORCE_BONUS, 0.0), NEG_INF)
    sub = 8
    n_grp = n_sel // sub
    score_ref[...] = score
    rank_ref[...] = jnp.zeros_like(rank_ref)
    j_loc = lax.broadcasted_iota(jnp.int32, (sub, tq), 0)
    last_blk = (qi * tq + tq - 1) // NSA_SEL_BLOCK
    for gp in range(n_grp):
        @pl.when(gp * sub <= last_blk)
        def _(gp=gp):
            groups = [score_ref[v * sub:(v + 1) * sub, :] for v in range(n_grp)]
            ranks = [rank_ref[v * sub:(v + 1) * sub, :] for v in range(n_grp)]
            for jp in range(gp * sub, (gp + 1) * sub):
                row = score_ref[jp:jp + 1, :]
                for v, sg in enumerate(groups):
                    if v * sub > jp:
                        beats = row >= sg
                    elif (v + 1) * sub - 1 <= jp:
                        beats = row > sg
                    else:
                        beats = (row > sg) | ((row == sg) & (j_loc > jp - v * sub))
                    ranks[v] = ranks[v] + jnp.where(beats, 1.0, 0.0)
            for v in range(n_grp):
                rank_ref[v * sub:(v + 1) * sub, :] = ranks[v]
    not_sel = jnp.where(rank_ref[...] < n_top, 0.0, 1.0)
    not_sel = jnp.concatenate([not_sel, jnp.zeros((LANES - n_sel, tq), F32)], axis=0).T.astype(BF16)

    q_aug = jnp.concatenate([q4, jnp.concatenate([not_sel] * hpg, axis=0)], axis=1)

    def sel_tile(kt, carry, bias):
        m, l, acc = carry
        k0 = pl.multiple_of(kt * tk, tk)
        sc = masked_scores(q_aug, ks_ref[pl.ds(k0, tk), :], bias)
        m_new = jnp.maximum(m, jnp.max(sc, axis=-1, keepdims=True))
        alpha = jnp.exp2(m - m_new)
        pe = jnp.exp2(sc - m_new)
        l = alpha * l + jnp.sum(pe, axis=-1, keepdims=True)
        acc = stack(alpha) * acc + _dot(stack(pe).astype(BF16), vs_ref[pl.ds(k0, tk), :])
        return m_new, l, acc

    last = (qi * tq + tq - 1) // tk
    init = (jnp.full((hpg, tq, 1), NEG_INF, F32), jnp.zeros((hpg, tq, 1), F32), jnp.zeros((hpg * tq, d), F32))
    pairs = last // 2
    carry = lax.fori_loop(0, pairs, lambda kp_, c: sel_tile(2 * kp_ + 1, sel_tile(2 * kp_, c, None), None), init)
    key_pos = last * tk + lax.broadcasted_iota(jnp.int32, (1, tk), 1)
    carry = sel_tile(last, carry, jnp.where(key_pos <= pos, 0.0, NEG_INF))
    left_over = last - 2 * pairs == 1
    lo_bias = jnp.zeros((tq, tk), F32) + jnp.where(left_over, 0.0, NEG_INF)
    _, l_sel, acc_sel = sel_tile(jnp.maximum(last - 1, 0), carry, lo_bias)
    o_sel = acc_sel / stack(l_sel)

    gt = jax.nn.sigmoid(gate_ref[...])
    lane = lax.broadcasted_iota(jnp.int32, (1, LANES), 1)
    for h in range(hpg):
        rows = slice(h * tq, (h + 1) * tq)
        o_h = jnp.zeros((tq, d), F32)
        for br, o_br in enumerate((o_cmp, o_sel, o_win)):
            g_col = jnp.sum(jnp.where(lane == gi * (3 * hpg) + h * 3 + br, gt, 0.0), axis=-1, keepdims=True)
            o_h = o_h + g_col * o_br[rows]
        o_ref[:, h * d:(h + 1) * d] = o_h.astype(BF16)


def selection_overlap(n_rows, n_sel):
    n_cmp = n_rows - 1
    c0 = np.arange(n_rows) * NSA_CMP_STRIDE
    s0 = np.arange(n_sel) * NSA_SEL_BLOCK
    ov = np.minimum(c0[:, None] + NSA_CMP_BLOCK, s0[None, :] + NSA_SEL_BLOCK) - np.maximum(c0[:, None], s0[None, :])
    ov = np.clip(ov, 0, None) / NSA_CMP_BLOCK
    ov[n_cmp:] = 0.0
    return jnp.asarray(ov, dtype=F32)


def nsa_attention(proj, kv_cmp, gate, q_gain, k_gains, tabs, tq=256, tk=512):
    b, s, _ = proj.shape
    g, d, hpg = NSA_KV_GROUPS, NSA_HEAD_DIM, NSA_HPG
    n_rows = s // NSA_CMP_STRIDE
    n_sel = s // NSA_SEL_BLOCK
    seg_blk = lambda i: NSA_HEADS + i * g
    cmp_spec = lambda kv: pl.BlockSpec((None, None, None, n_rows, d), lambda bi, gi, qi: (kv, bi, gi, 0, 0))
    seq_spec = lambda i: pl.BlockSpec((None, s, d), lambda bi, gi, qi: (bi, 0, seg_blk(i) + gi))
    const = lambda shape: pl.BlockSpec(shape, lambda bi, gi, qi: (0, 0))
    return pl.pallas_call(
        functools.partial(_nsa_attn_kernel, tq=tq, tk=tk, seq=s),
        grid=(b, g, s // tq),
        in_specs=[pl.BlockSpec((None, tq, hpg * d), lambda bi, gi, qi: (bi, qi, gi)),
                  cmp_spec(0), cmp_spec(1),
                  seq_spec(2), seq_spec(3), seq_spec(4), seq_spec(5),
                  pl.BlockSpec((None, tq, LANES), lambda bi, gi, qi: (bi, qi, 0)),
                  const((n_sel, n_rows)), const((1, d)), const((2, d)),
                  const((s, d)), const((s, d)), const((s, d))],
        out_specs=pl.BlockSpec((None, tq, hpg * d), lambda bi, gi, qi: (bi, qi, gi)),
        out_shape=jax.ShapeDtypeStruct((b, s, NSA_HEADS * d), BF16),
        scratch_shapes=[pltpu.VMEM((s, 2 * d), BF16)] + [pltpu.VMEM((s, d), BF16)] * 3
                       + [pltpu.VMEM((hpg * tq, d), BF16)] + [pltpu.VMEM((n_sel, tq), F32)] * 2,
        compiler_params=_params(("parallel", "parallel", "arbitrary")),
        name="nsa_attention",
    )(proj, kv_cmp, kv_cmp, proj, proj, proj, proj, gate, selection_overlap(n_rows, n_sel).T,
      q_gain.reshape(1, d), k_gains, *tabs)


def nsa_mixer(x2, b, s, gain, w_in, w_out, layer, q_norm, k_norm, cmp_pe, cmp_w1, cmp_w2):
    d, g = NSA_HEAD_DIM, NSA_KV_GROUPS
    n_main = (NSA_HEADS + 6 * g) * d
    w_gate = jnp.pad(w_in[layer, :, n_main:], ((0, 0), (0, LANES - 3 * NSA_HEADS)))
    proj, gate = norm_matmul(x2, gain, w_in, layer, n_main, w_gate)
    proj = proj.reshape(b, s, n_main)
    gate = gate.reshape(b, s, LANES)
    tabs = rope_tables(jnp.arange(s), d)
    cmp_end = jnp.arange(s // NSA_CMP_STRIDE) * NSA_CMP_STRIDE + (NSA_CMP_BLOCK - 1)
    kv_cmp = nsa_compress(proj, cmp_pe, cmp_w1, cmp_w2, k_norm[0], rope_tables(cmp_end, d))
    o = nsa_attention(proj, kv_cmp, gate, q_norm, k_norm[1:3], tabs)
    return matmul_residual(o.reshape(b * s, NSA_HEADS * d), w_out, layer, x2)


def _log_sigmoid(z):
    return jnp.minimum(z, 0.0) - jnp.log1p(jnp.exp(-jnp.abs(z)))


def _gla_kernel(q_ref, k_ref, v_ref, r_ref, glr_ref, wg_ref, bg_ref, on_ref, o_ref, state_ref, *, rows, hps):
    c = GLA_CHUNK
    sub = GLA_SUB
    dk, dv = GLA_KEY_DIM, GLA_VAL_DIM

    @pl.when(pl.program_id(2) == 0)
    def _():
        state_ref[...] = jnp.zeros_like(state_ref)

    tri = jnp.where(lax.broadcasted_iota(jnp.int32, (c, c), 0) >= lax.broadcasted_iota(jnp.int32, (c, c), 1), 1.0, 0.0)
    sub_row = lax.broadcasted_iota(jnp.int32, (sub, 1), 0)

    def head_chunk(hh, r0):
        kcols = slice(hh * dk, (hh + 1) * dk)
        vcols = slice(hh * dv, (hh + 1) * dv)
        qc = q_ref[pl.ds(r0, c), kcols] * (dk ** -0.5)
        kc = k_ref[pl.ds(r0, c), kcols]
        vc = v_ref[pl.ds(r0, c), vcols]
        z = _dot(glr_ref[pl.ds(r0, c), :], wg_ref[:, kcols], precision=lax.Precision.HIGHEST) + bg_ref[:, kcols]
        log_a = _log_sigmoid(z) / GLA_TAU
        cum = _dot(tri, log_a, precision=lax.Precision.HIGHEST)
        vb = vc.astype(BF16)

        state = state_ref[hh]
        o_inter = _dot((qc * jnp.exp(cum)).astype(BF16), state.astype(BF16), NT_DIMS)

        parts = []
        for bi in range(c // sub):
            lo = bi * sub
            q_i, k_i, v_i, c_i = qc[lo:lo + sub], kc[lo:lo + sub], vc[lo:lo + sub], cum[lo:lo + sub]
            o_i = o_inter[lo:lo + sub]
            if bi > 0:
                c0 = c_i[0:1]
                qs = (q_i * jnp.exp(c_i - c0)).astype(BF16)
                kp = (kc[:lo] * jnp.exp(c0 - cum[:lo])).astype(BF16)
                att = _dot(qs, kp, NT_DIMS)
                o_i = o_i + _dot(att.astype(BF16), vb[:lo])
            for jj in range(sub):
                e = jnp.exp(jnp.where(sub_row >= jj, c_i - c_i[jj:jj + 1], NEG_INF))
                col = jnp.sum(q_i * k_i[jj:jj + 1] * e, axis=-1, keepdims=True)
                o_i = o_i + col * v_i[jj:jj + 1]
            parts.append(o_i)
        o = jnp.concatenate(parts, axis=0)

        last = cum[c - 1:c]
        kd = (kc * jnp.exp(last - cum)).astype(BF16)
        state_ref[hh] = state * jnp.exp(last) + _dot(vb, kd, TN_DIMS)

        ms = jnp.mean(o * o, axis=-1, keepdims=True)
        on = o * lax.rsqrt(ms + NORM_EPS) * on_ref[...]
        r = r_ref[pl.ds(r0, c), vcols]
        o_ref[pl.ds(r0, c), vcols] = (on * (r * jax.nn.sigmoid(r))).astype(BF16)

    def chunk(ci, carry):
        r0 = pl.multiple_of(ci * c, c)
        for hh in range(hps):
            head_chunk(hh, r0)
        return carry

    lax.fori_loop(0, rows // c, chunk, 0)


def gla_attention(proj, glr, w_gate_up, b_gate, o_norm, rows=512, hps=4):
    b, s, _ = proj.shape
    nh, dk, dv = GLA_HEADS, GLA_KEY_DIM, GLA_VAL_DIM
    wg = jnp.pad(w_gate_up, ((0, LANES - GLA_GATE_RANK), (0, 0)))
    kw, vw = hps * dk, hps * dv
    k_blk = nh * dk // kw
    v_blk = 2 * nh * dk // vw
    r_blk = (2 * nh * dk + nh * dv) // vw
    return pl.pallas_call(
        functools.partial(_gla_kernel, rows=rows, hps=hps),
        grid=(b, nh // hps, s // rows),
        in_specs=[pl.BlockSpec((None, rows, kw), lambda bi, h, ci: (bi, ci, h)),
                  pl.BlockSpec((None, rows, kw), lambda bi, h, ci: (bi, ci, k_blk + h)),
                  pl.BlockSpec((None, rows, vw), lambda bi, h, ci: (bi, ci, v_blk + h)),
                  pl.BlockSpec((None, rows, vw), lambda bi, h, ci: (bi, ci, r_blk + h)),
                  pl.BlockSpec((None, rows, LANES), lambda bi, h, ci: (bi, ci, 0)),
                  pl.BlockSpec((LANES, kw), lambda bi, h, ci: (0, h)),
                  pl.BlockSpec((1, kw), lambda bi, h, ci: (0, h)),
                  pl.BlockSpec((1, dv), lambda bi, h, ci: (0, 0))],
        out_specs=pl.BlockSpec((None, rows, vw), lambda bi, h, ci: (bi, ci, h)),
        out_shape=jax.ShapeDtypeStruct((b, s, nh * dv), BF16),
        scratch_shapes=[pltpu.VMEM((hps, dv, dk), F32)],
        compiler_params=_params(("parallel", "parallel", "arbitrary")),
        name="gla_attention",
    )(proj, proj, proj, proj, glr, wg, b_gate.reshape(1, nh * dk), o_norm.reshape(1, dv))


def gla_mixer(x2, b, s, gain, w_in, w_gate_up, b_gate, o_norm, w_out, layer):
    nh, dk, dv = GLA_HEADS, GLA_KEY_DIM, GLA_VAL_DIM
    n_qkv = 2 * nh * dk + nh * dv
    w_r = w_in[layer, :, n_qkv + GLA_GATE_RANK:]
    w_glr = jnp.pad(w_in[layer, :, n_qkv:n_qkv + GLA_GATE_RANK], ((0, 0), (0, LANES - GLA_GATE_RANK)))
    proj, glr = norm_matmul(x2, gain, w_in, layer, n_qkv, w_glr, w_tail=w_r)
    o = gla_attention(proj.reshape(b, s, -1), glr.reshape(b, s, LANES), w_gate_up, b_gate, o_norm)
    return matmul_residual(o.reshape(b * s, nh * dv), w_out, layer, x2)


def _swa_heads(x, gain, tabs, seg, out_scale):
    blk = x.shape[0]
    d = SWA_HEAD_DIM
    n = x.shape[1] // LANES
    tiles = jnp.concatenate([x[:, t * LANES:(t + 1) * LANES] for t in range(n)], axis=0)
    sq = tiles * tiles
    hi = sq.astype(BF16)
    lo = (sq - hi.astype(F32)).astype(BF16)
    ms = (_dot(hi, seg) + _dot(lo, seg)) * (1.0 / d)
    heads = []
    for t in range(n):
        rows = slice(t * blk, (t + 1) * blk)
        y = _norm_rope(tiles[rows], gain, *tabs, d // ROPE_FRACTION // 2, ms[rows]) * out_scale
        heads += [y[:, u * d:(u + 1) * d] for u in range(LANES // d)]
    return heads


def _swa_kernel(q_ref, kp_ref, kc_ref, vp_ref, vc_ref, sink_ref, qg_ref, kg_ref, seg_ref,
                cc_ref, s1c_ref, s2c_ref, cp_ref, s1p_ref, s2p_ref, o_ref, *, blk):
    qi = pl.program_id(1)
    hpg, d, g = SWA_HPG, SWA_HEAD_DIM, SWA_KV_HEADS
    seg = seg_ref[...]
    tabs_c = (cc_ref[...], s1c_ref[...], s2c_ref[...])
    tabs_p = (cp_ref[...], s1p_ref[...], s2p_ref[...])
    q_heads = _swa_heads(q_ref[...], qg_ref[...], tabs_c, seg, d ** -0.5)
    kp_heads = _swa_heads(kp_ref[...], kg_ref[...], tabs_p, seg, 1.0)
    kc_heads = _swa_heads(kc_ref[...], kg_ref[...], tabs_c, seg, 1.0)
    vp, vc = vp_ref[...], vc_ref[...]

    qp = qi * blk + lax.broadcasted_iota(jnp.int32, (blk, 1), 0)
    kpos = (qi - 1) * blk + lax.broadcasted_iota(jnp.int32, (1, 2 * blk), 1)
    bias = jnp.where((kpos <= qp) & (kpos > qp - SWA_WINDOW) & (kpos >= 0), 0.0, NEG_INF)

    for gi in range(g):
        q = jnp.concatenate(q_heads[gi * hpg:(gi + 1) * hpg], axis=0).astype(BF16)
        k = jnp.concatenate([kp_heads[gi], kc_heads[gi]], axis=0).astype(BF16)
        v = jnp.concatenate([vp[:, gi * d:(gi + 1) * d], vc[:, gi * d:(gi + 1) * d]], axis=0).astype(BF16)
        s = _dot(q, k, NT_DIMS).reshape(hpg, blk, 2 * blk) + bias[None]
        sink = sink_ref[gi]
        m = jnp.maximum(jnp.max(s, axis=-1, keepdims=True), sink)
        p = jnp.exp(s - m)
        denom = jnp.sum(p, axis=-1, keepdims=True) + jnp.exp(sink - m)
        o = _dot(p.reshape(hpg * blk, 2 * blk).astype(BF16), v).reshape(hpg, blk, d) / denom
        o_ref[:, gi * hpg * d:(gi + 1) * hpg * d] = jnp.concatenate([o[h] for h in range(hpg)], axis=1).astype(BF16)


def swa_attention(proj, sinks, q_gain, k_gain, tabs, blk=128):
    b, s, _ = proj.shape
    g, hpg, d = SWA_KV_HEADS, SWA_HPG, SWA_HEAD_DIM
    nq, nkv = SWA_HEADS * d, g * d
    k_blk = nq // nkv
    prev = lambda qi: jnp.maximum(qi - 1, 0)
    kv_spec = lambda col, row: pl.BlockSpec((None, blk, nkv), lambda bi, qi: (bi, row(qi), col))
    const = lambda shape: pl.BlockSpec(shape, lambda bi, qi: (0,) * len(shape))
    tab_spec = lambda row: pl.BlockSpec((blk, LANES), lambda bi, qi: (row(qi), 0))
    cur = lambda qi: qi
    seg_id = np.arange(LANES) // d
    seg = jnp.asarray(seg_id[:, None] == seg_id[None, :], dtype=BF16)
    tile_gain = lambda gn: jnp.tile(gn.reshape(1, d), (1, LANES // d))
    return pl.pallas_call(
        functools.partial(_swa_kernel, blk=blk),
        grid=(b, s // blk),
        in_specs=[pl.BlockSpec((None, blk, nq), lambda bi, qi: (bi, qi, 0)),
                  kv_spec(k_blk, prev), kv_spec(k_blk, cur), kv_spec(k_blk + 1, prev), kv_spec(k_blk + 1, cur),
                  const((g, hpg, 1, 1)), const((1, LANES)), const((1, LANES)), const((LANES, LANES)),
                  tab_spec(cur), tab_spec(cur), tab_spec(cur), tab_spec(prev), tab_spec(prev), tab_spec(prev)],
        out_specs=pl.BlockSpec((None, blk, nq), lambda bi, qi: (bi, qi, 0)),
        out_shape=jax.ShapeDtypeStruct((b, s, nq), BF16),
        compiler_params=_params(("parallel", "parallel")),
        name="swa_attention",
    )(proj, proj, proj, proj, proj, sinks.astype(F32).reshape(g, hpg, 1, 1), tile_gain(q_gain), tile_gain(k_gain),
      seg, *tabs, *tabs)


def swa_mixer(x2, b, s, gain, w_in, w_out, layer, q_norm, k_norm, sinks):
    d, g = SWA_HEAD_DIM, SWA_KV_HEADS
    n_in = (SWA_HEADS + 2 * g) * d
    (proj,) = norm_matmul(x2, gain, w_in, layer, n_in)
    o = swa_attention(proj.reshape(b, s, n_in), sinks, q_norm, k_norm, rope_tables(jnp.arange(s), d))
    return matmul_residual(o.reshape(b * s, SWA_HEADS * d), w_out, layer, x2)


def kernel(x, norm_mix, norm_mlp, mlp_w_up, mlp_w_down, nsa_w_in, nsa_w_out, nsa_q_norm, nsa_k_norm, nsa_cmp_pe, nsa_cmp_w1, nsa_cmp_w2, gla_w_in, gla_w_gate_up, gla_b_gate, gla_o_norm, gla_w_out, swa_w_in, swa_w_out, swa_q_norm, swa_k_norm, swa_sinks):
    b, s, d = x.shape
    x2 = x.reshape(b * s, d)
    ia = ib = ic = 0
    for i in range(norm_mix.shape[0]):
        kind = i % N_MIXERS
        if kind == 0:
            x2 = nsa_mixer(x2, b, s, norm_mix[i], nsa_w_in, nsa_w_out, ia, nsa_q_norm[ia], nsa_k_norm[ia],
                           nsa_cmp_pe[ia], nsa_cmp_w1[ia], nsa_cmp_w2[ia])
            ia += 1
        elif kind == 1:
            x2 = gla_mixer(x2, b, s, norm_mix[i], gla_w_in, gla_w_gate_up[ib], gla_b_gate[ib],
                           gla_o_norm[ib], gla_w_out, ib)
            ib += 1
        else:
            x2 = swa_mixer(x2, b, s, norm_mix[i], swa_w_in, swa_w_out, ic, swa_q_norm[ic], swa_k_norm[ic],
                           swa_sinks[ic])
            ic += 1
        x2 = mlp_block(x2, norm_mlp[i], mlp_w_up, mlp_w_down, i)
    return x2.reshape(b, s, d)
```
